```python
import math
import jax, jax.numpy as jnp
from jax import lax
import numpy as np

D_MODEL = 1024
BATCH = 1
SEQ = 16384
DEPTH = 2
DEC_BATCH = 128
DEC_SEQ = 8
PAST_LEN = 16384
PAGE_SIZE = 128

N_EVEN = (DEPTH + 1) // 2
N_ODD = DEPTH // 2
W_A = D_MODEL // 2
DK_A = 128
H_A = W_A // DK_A
DV_A = W_A // H_A
CHUNK_A = 64
W_B = D_MODEL // 2
D_B = 64
H_B = W_B // (2 * D_B)
KVH_B = 2
G_B = H_B // KVH_B
Q_BLOCK = 128
HD_C = 64
H_C = D_MODEL // HD_C
KVH_C = 2
G_C = H_C // KVH_C
WINDOW = 128
EPS = 1e-6
F32 = jnp.float32

SPLIT_A = (H_A * DK_A, H_A * DK_A, H_A * DV_A, H_A * DV_A,
           H_B * 2 * D_B, KVH_B * 2 * D_B, KVH_B * 2 * D_B, H_B * 2 * D_B)
SPLIT_C = (H_C * HD_C, KVH_C * HD_C, KVH_C * HD_C, H_C * HD_C)

kernel_name = 'hybrid_hgrn2_diffattn_swa_step'


def split_cols(a, sizes):
    idx = np.cumsum(sizes)[:-1].tolist()
    return jnp.split(a, idx, axis=-1)


def rms_norm(x, g):
    xf = x.astype(F32)
    y = xf * lax.rsqrt(jnp.mean(xf * xf, axis=-1, keepdims=True) + EPS)
    return (y * g.astype(F32)).astype(x.dtype)


def alibi_slopes(n):
    return 2.0 ** (-8.0 * jnp.arange(1, n + 1, dtype=F32) / n)


def gla_chunked(q, k, v, logf, s0):
    B, T, H, K = q.shape
    C = CHUNK_A if T % CHUNK_A == 0 else math.gcd(T, CHUNK_A)
    n = T // C

    def to_chunks(a):
        return a.reshape(B, n, C, *a.shape[2:]).swapaxes(0, 1)

    causal = jnp.tril(jnp.ones((C, C), bool))[None, :, :, None, None]

    def step(S, xs):
        qc, kc, vc, lc = xs
        b = jnp.cumsum(lc, axis=1)
        o = jnp.einsum('bchk,bhkv->bchv', qc * jnp.exp(b), S)
        diff = b[:, :, None] - b[:, None, :]
        decay = jnp.exp(jnp.where(causal, diff, -jnp.inf))
        A = jnp.einsum('bthk,btshk,bshk->bths', qc, decay, kc)
        o = o + jnp.einsum('bths,bshv->bthv', A, vc)
        b_last = b[:, -1]
        S = jnp.exp(b_last)[..., None] * S + jnp.einsum(
            'bshk,bshv->bhkv', kc * jnp.exp(b_last[:, None] - b), vc)
        return S, o

    S, o = lax.scan(step, s0, (to_chunks(q), to_chunks(k), to_chunks(v), to_chunks(logf)))
    return o.swapaxes(0, 1).reshape(B, T, H, v.shape[-1]), S


def diff_attend(q, k, v, q_pos, k_pos, lam):
    Tq = q.shape[0]
    qg = q.reshape(Tq, KVH_B, G_B, 2, D_B)
    logits = jnp.einsum('qngcd,kncd->ngcqk', qg, k).astype(F32) * (D_B ** -0.5)
    dist = (q_pos[:, None] - k_pos[None, :]).astype(F32)
    slopes = alibi_slopes(H_B).reshape(KVH_B, G_B, 1, 1, 1)
    logits = jnp.where(dist >= 0, logits - slopes * dist, -jnp.inf)
    p = jax.nn.softmax(logits, axis=-1)
    w = p[:, :, 0] - lam * p[:, :, 1]
    out = jnp.einsum('ngqk,knv->qngv', w.astype(v.dtype), v)
    return out.reshape(Tq, H_B, 2 * D_B)


def diff_prompt_attend(q, k, v, lam):
    B, S = q.shape[:2]
    nb = S // Q_BLOCK
    qb = q.reshape(B, nb, Q_BLOCK, H_B, 2, D_B).swapaxes(0, 1)
    qpos = jnp.arange(S, dtype=jnp.int32).reshape(nb, Q_BLOCK)
    kpos = jnp.arange(S, dtype=jnp.int32)
    core = jax.vmap(diff_attend, in_axes=(0, 0, 0, None, None, None))
    out = lax.map(lambda xs: core(xs[0], k, v, xs[1], kpos, lam), (qb, qpos))
    return out.swapaxes(0, 1).reshape(B, S, H_B, 2 * D_B)


def diff_sample_attend(q, k, v, lam, cache_k, cache_v, page_table, e, past):
    T = q.shape[1]
    kpos = jnp.arange(past + T, dtype=jnp.int32)
    qpos = past + jnp.arange(T, dtype=jnp.int32)

    def one(xs):
        qi, ki, vi, pages = xs
        kp = cache_k[e, pages].reshape(past, KVH_B, 2, D_B)
        vp = cache_v[e, pages].reshape(past, KVH_B, 2 * D_B)
        k_all = jnp.concatenate([kp, ki.astype(kp.dtype)], axis=0)
        v_all = jnp.concatenate([vp, vi.astype(vp.dtype)], axis=0)
        return diff_attend(qi, k_all, v_all, qpos, kpos, lam)

    return lax.map(one, (q, k, v, page_table))


def swa_attend(q, k, v, q_pos, k_pos, sinks):
    Tq = q.shape[0]
    qg = q.reshape(Tq, KVH_C, G_C, HD_C)
    logits = jnp.einsum('qngd,knd->ngqk', qg, k).astype(F32) * (HD_C ** -0.5)
    dist = q_pos[:, None] - k_pos[None, :]
    valid = (dist >= 0) & (dist <= WINDOW) & (k_pos[None, :] >= 0)
    slopes = alibi_slopes(H_C).reshape(KVH_C, G_C, 1, 1)
    logits = jnp.where(valid, logits - slopes * dist.astype(F32), -jnp.inf)
    sink = sinks.astype(F32).reshape(KVH_C, G_C, 1, 1)
    m = jnp.maximum(jnp.max(logits, axis=-1, keepdims=True), sink)
    p = jnp.exp(logits - m)
    w = p / (jnp.sum(p, axis=-1, keepdims=True) + jnp.exp(sink - m))
    out = jnp.einsum('ngqk,knd->qngd', w.astype(v.dtype), v)
    return out.reshape(Tq, H_C, HD_C)


def swa_prompt_attend(q, k, v, sinks):
    B, S = q.shape[:2]
    nb = S // WINDOW
    qb = q.reshape(B, nb, WINDOW, H_C, HD_C)
    kb = k.reshape(B, nb, WINDOW, KVH_C, HD_C)
    vb = v.reshape(B, nb, WINDOW, KVH_C, HD_C)

    def band(a):
        prev = jnp.concatenate([jnp.zeros_like(a[:, :1]), a[:, :-1]], axis=1)
        return jnp.concatenate([prev, a], axis=2)

    pos = jnp.arange(S, dtype=jnp.int32).reshape(nb, WINDOW)
    kpos = jnp.concatenate([pos - WINDOW, pos], axis=1)
    core = jax.vmap(jax.vmap(swa_attend, in_axes=(0, 0, 0, 0, 0, None)),
                    in_axes=(0, 0, 0, None, None, None))
    return core(qb, band(kb), band(vb), pos, kpos, sinks).reshape(B, S, H_C, HD_C)


def swa_sample_attend(q, k, v, sinks, buf_k, buf_v, past):
    T = q.shape[1]
    k_all = jnp.concatenate([buf_k, k.astype(buf_k.dtype)], axis=1)
    v_all = jnp.concatenate([buf_v, v.astype(buf_v.dtype)], axis=1)
    kpos = past - WINDOW + jnp.arange(WINDOW + T, dtype=jnp.int32)
    qpos = past + jnp.arange(T, dtype=jnp.int32)
    return jax.vmap(swa_attend, in_axes=(0, 0, 0, None, None, None))(q, k_all, v_all, qpos, kpos, sinks)


def even_mixer(x, s0, attend, prm):
    (g_norm, w_in, w_out, lb, o_gain, q_gain, k_gain, subln, lq1, lk1, lq2, lk2, lam_init) = prm
    B, T, _ = x.shape
    h = rms_norm(x, g_norm)
    qa, fa, ia, ga, qb, kb, vb, gb = split_cols(h @ w_in, SPLIT_A)
    q = (jax.nn.silu(qa.astype(F32)) * (DK_A ** -0.5)).reshape(B, T, H_A, DK_A)
    f = lb + (1.0 - lb) * jax.nn.sigmoid(fa.astype(F32))
    logf = jnp.log(f).reshape(B, T, H_A, DK_A)
    kk = (1.0 - f).reshape(B, T, H_A, DK_A)
    vv = ia.astype(F32).reshape(B, T, H_A, DV_A)
    oa, s_new = gla_chunked(q, kk, vv, logf, s0.astype(F32))
    oa = rms_norm(oa, o_gain).astype(x.dtype).reshape(B, T, H_A * DV_A) * jax.nn.silu(ga)
    qb = rms_norm(qb.reshape(B, T, H_B, 2, D_B), q_gain)
    kb = rms_norm(kb.reshape(B, T, KVH_B, 2, D_B), k_gain)
    vb = vb.reshape(B, T, KVH_B, 2 * D_B)
    lam = (jnp.exp(jnp.sum(lq1.astype(F32) * lk1.astype(F32)))
           - jnp.exp(jnp.sum(lq2.astype(F32) * lk2.astype(F32))) + lam_init)
    ob = attend(qb, kb, vb, lam)
    ob = (rms_norm(ob, subln) * (1.0 - lam_init)).reshape(B, T, H_B * 2 * D_B) * jax.nn.silu(gb)
    y = jnp.concatenate([oa, ob], axis=-1) @ w_out
    return x + y, s_new.astype(x.dtype), kb, vb


def odd_mixer(x, attend, g_norm, w_in, w_out, q_gain, k_gain, snk):
    B, T, _ = x.shape
    h = rms_norm(x, g_norm)
    q, k, v, g = split_cols(h @ w_in, SPLIT_C)
    q = rms_norm(q.reshape(B, T, H_C, HD_C), q_gain)
    k = rms_norm(k.reshape(B, T, KVH_C, HD_C), k_gain)
    v = v.reshape(B, T, KVH_C, HD_C)
    o = attend(q, k, v, snk)
    y = (o.reshape(B, T, H_C * HD_C) * jax.nn.silu(g)) @ w_out
    return x + y, k, v


def setup_inputs(seed: int = 0) -> dict:
    key = jax.random.key(seed)
    ks = iter(jax.random.split(key, 40))

    def nrm(shape, s):
        return s * jax.random.normal(next(ks), shape, F32)

    n_pages = PAST_LEN // PAGE_SIZE
    n_used = DEC_BATCH * n_pages
    n_pool = n_used + max(1, n_used // 4)
    page_table = jax.random.permutation(next(ks), n_pool)[:n_used].reshape(DEC_BATCH, n_pages).astype(jnp.int32)
    in_a = sum(SPLIT_A)
    in_c = sum(SPLIT_C)
    return {
        'x_prompt': nrm((BATCH, SEQ, D_MODEL), 1.0),
        'x_sample': nrm((DEC_BATCH, DEC_SEQ, D_MODEL), 1.0),
        'state_hgrn': nrm((N_EVEN, DEC_BATCH, H_A, DK_A, DV_A), 1.0),
        'cache_k_diff': nrm((N_EVEN, n_pool, PAGE_SIZE, KVH_B, 2, D_B), 1.0),
        'cache_v_diff': nrm((N_EVEN, n_pool, PAGE_SIZE, KVH_B, 2 * D_B), 1.0),
        'cache_k_swa': nrm((N_ODD, DEC_BATCH, WINDOW, KVH_C, HD_C), 1.0),
        'cache_v_swa': nrm((N_ODD, DEC_BATCH, WINDOW, KVH_C, HD_C), 1.0),
        'page_table': page_table,
        'norm_a': 1.0 + nrm((N_EVEN, D_MODEL), 0.1),
        'w_in_a': nrm((N_EVEN, D_MODEL, in_a), D_MODEL ** -0.5),
        'w_out_a': nrm((N_EVEN, W_A + W_B, D_MODEL), (W_A + W_B) ** -0.5),
        'lb_logits': nrm((N_EVEN + 1, H_A * DK_A), 0.5),
        'hgrn_out_gain': 1.0 + nrm((N_EVEN, DV_A), 0.1),
        'diff_q_gain': 1.0 + nrm((N_EVEN, D_B), 0.1),
        'diff_k_gain': 1.0 + nrm((N_EVEN, D_B), 0.1),
        'diff_subln_gain': 1.0 + nrm((N_EVEN, 2 * D_B), 0.1),
        'lam_q1': nrm((N_EVEN, D_B), 0.1),
        'lam_k1': nrm((N_EVEN, D_B), 0.1),
        'lam_q2': nrm((N_EVEN, D_B), 0.1),
        'lam_k2': nrm((N_EVEN, D_B), 0.1),
        'norm_c': 1.0 + nrm((N_ODD, D_MODEL), 0.1),
        'w_in_c': nrm((N_ODD, D_MODEL, in_c), D_MODEL ** -0.5),
        'w_out_c': nrm((N_ODD, H_C * HD_C, D_MODEL), (H_C * HD_C) ** -0.5),
        'swa_q_gain': 1.0 + nrm((N_ODD, HD_C), 0.1),
        'swa_k_gain': 1.0 + nrm((N_ODD, HD_C), 0.1),
        'sinks': nrm((N_ODD, H_C), 1.0),
    }


def reference(x_prompt, x_sample, state_hgrn, cache_k_diff, cache_v_diff, cache_k_swa, cache_v_swa,
              page_table, norm_a, w_in_a, w_out_a, lb_logits, hgrn_out_gain, diff_q_gain, diff_k_gain,
              diff_subln_gain, lam_q1, lam_k1, lam_q2, lam_k2, norm_c, w_in_c, w_out_c,
              swa_q_gain, swa_k_gain, sinks):
    past = page_table.shape[1] * cache_k_diff.shape[2]
    lbs = jnp.cumsum(jax.nn.softmax(lb_logits.astype(F32), axis=0), axis=0)
    xp, xs = x_prompt, x_sample
    hs_p, hs_s, kdp, vdp, kds, vds = [], [], [], [], [], []
    ksp, vsp, kss, vss = [], [], [], []
    for l in range(DEPTH):
        if l % 2 == 0:
            e = l // 2
            lam_init = 0.8 - 0.6 * math.exp(-0.3 * l)
            prm = (norm_a[e], w_in_a[e], w_out_a[e], lbs[e], hgrn_out_gain[e], diff_q_gain[e],
                   diff_k_gain[e], diff_subln_gain[e], lam_q1[e], lam_k1[e], lam_q2[e], lam_k2[e], lam_init)
            s0 = jnp.zeros((xp.shape[0], H_A, DK_A, DV_A), F32)
            xp, s_p, k_p, v_p = even_mixer(xp, s0, diff_prompt_attend, prm)

            def sample_attend(q, k, v, lam, e=e):
                return diff_sample_attend(q, k, v, lam, cache_k_diff, cache_v_diff, page_table, e, past)

            xs, s_s, k_s, v_s = even_mixer(xs, state_hgrn[e], sample_attend, prm)
            hs_p.append(s_p); hs_s.append(s_s)
            kdp.append(k_p); vdp.append(v_p); kds.append(k_s); vds.append(v_s)
        else:
            o = l // 2
            xp, k_p, v_p = odd_mixer(xp, swa_prompt_attend, norm_c[o], w_in_c[o], w_out_c[o],
                                     swa_q_gain[o], swa_k_gain[o], sinks[o])

            def sample_attend_c(q, k, v, snk, o=o):
                return swa_sample_attend(q, k, v, snk, cache_k_swa[o], cache_v_swa[o], past)

            xs, k_s, v_s = odd_mixer(xs, sample_attend_c, norm_c[o], w_in_c[o], w_out_c[o],
                                     swa_q_gain[o], swa_k_gain[o], sinks[o])
            ksp.append(k_p[:, -WINDOW:]); vsp.append(v_p[:, -WINDOW:])
            kss.append(jnp.concatenate([cache_k_swa[o], k_s.astype(cache_k_swa.dtype)], axis=1)[:, -WINDOW:])
            vss.append(jnp.concatenate([cache_v_swa[o], v_s.astype(cache_v_swa.dtype)], axis=1)[:, -WINDOW:])
    return (xp, xs, jnp.stack(hs_p), jnp.stack(hs_s), jnp.stack(kdp), jnp.stack(vdp),
            jnp.stack(kds), jnp.stack(vds), jnp.stack(ksp), jnp.stack(vsp), jnp.stack(kss), jnp.stack(vss))
```

```python
import functools
import math

import jax
import jax.numpy as jnp
import numpy as np
from jax import lax
from jax.experimental import pallas as pl
from jax.experimental.pallas import tpu as pltpu

F32 = jnp.float32
BF16 = jnp.bfloat16

EPS = 1e-6
LOG2E = 1.4426950408889634
NEG_BIG = -1e30
LANES = 128
VMEM_LIMIT_BYTES = 56 * 1024 * 1024

DK_A = 128
DV_A = 128
D_B = 64
KVH_B = 2
G_B = 2
HD_C = 64
KVH_C = 2
WINDOW = 128
GLA_SUB = 16


def _dot(a, b):
    return jnp.dot(a, b, preferred_element_type=F32)


def _dot_nt(a, b):
    return lax.dot_general(a, b, (((1,), (1,)), ((), ())), preferred_element_type=F32)


def _dot_tn(a, b):
    return lax.dot_general(a, b, (((0,), (0,)), ((), ())), preferred_element_type=F32)


def _silu(x):
    return x * jax.nn.sigmoid(x)


def _rms_rows(x, gain):
    return x * lax.rsqrt(jnp.mean(x * x, axis=-1, keepdims=True) + EPS) * gain


def _split3(x):
    x1 = x.astype(BF16)
    r1 = x - x1.astype(F32)
    x2 = r1.astype(BF16)
    x3 = (r1 - x2.astype(F32)).astype(BF16)
    return x1, x2, x3


def _group_meansq(x, bd):
    s = x * x
    hi = s.astype(BF16)
    lo = (s - hi.astype(F32)).astype(BF16)
    return _dot(hi, bd) + _dot(lo, bd)


def _group_rms(x, gain, bd):
    wb = bd.shape[0]
    outs = []
    for j in range(x.shape[1] // wb):
        part = x[:, j * wb:(j + 1) * wb]
        ms = _group_meansq(part, bd)
        outs.append(part * lax.rsqrt(ms + EPS) * gain[:, j * wb:(j + 1) * wb])
    return outs[0] if len(outs) == 1 else jnp.concatenate(outs, axis=1)


def _block_diag_mean(width, group):
    r = np.arange(width)[:, None] // group
    c = np.arange(width)[None, :] // group
    return jnp.asarray((r == c).astype(np.float32) / group, dtype=BF16)


def _lam_value(lq1, lk1, lq2, lk2, lam_init):
    a = jnp.sum(lq1 * lk1, axis=-1, keepdims=True)
    b = jnp.sum(lq2 * lk2, axis=-1, keepdims=True)
    return jnp.exp(a) - jnp.exp(b) + lam_init


def _cparams(sem):
    return pltpu.CompilerParams(dimension_semantics=sem, vmem_limit_bytes=VMEM_LIMIT_BYTES)


def _full(shape):
    nd = len(shape)
    return pl.BlockSpec(shape, lambda *_: (0,) * nd)


def _inproj_a_kernel(x_ref, g_ref, w_ref, lbl_ref, qg_ref, kg_ref, bd_ref,
                     qa_ref, lf_ref, kk_ref, ia_ref, gsa_ref, qb_ref, kb_ref, vb_ref, gsb_ref, *, e, wa, wb, wkv):
    hb = _rms_rows(x_ref[...], g_ref[...]).astype(BF16)
    bd = bd_ref[...]
    o = [0]

    def seg(width):
        z = _dot(hb, w_ref[:, o[0]:o[0] + width])
        o[0] += width
        return z

    qa = seg(wa)
    qa_ref[...] = _silu(qa) * (DK_A ** -0.5)
    lbl = lbl_ref[...]
    ex = jnp.exp(lbl - jnp.max(lbl, axis=0, keepdims=True))
    sm = ex / jnp.sum(ex, axis=0, keepdims=True)
    lb = jnp.sum(sm[0:e + 1, :], axis=0, keepdims=True)
    f = lb + (1.0 - lb) * jax.nn.sigmoid(seg(wa))
    lf_ref[...] = jnp.log(f)
    kk_ref[...] = 1.0 - f
    ia_ref[...] = seg(wa)
    gsa_ref[...] = _silu(seg(wa))
    qb_ref[...] = _group_rms(seg(wb), qg_ref[...], bd) * (D_B ** -0.5 * LOG2E)
    kb_ref[...] = _group_rms(seg(wkv), kg_ref[...], bd)
    vb_ref[...] = seg(wkv)
    gsb_ref[...] = _silu(seg(wb))


def _inproj_a(x, g, w_bf16, lb_logits, q_gain_t, k_gain_t, bd, e, tm):
    T, D = x.shape
    wa, wb, wkv = lb_logits.shape[1], q_gain_t.shape[1], k_gain_t.shape[1]
    n_in = w_bf16.shape[1]
    assert n_in == 4 * wa + 2 * wb + 2 * wkv and T % tm == 0
    widths = (wa, wa, wa, wa, wa, wb, wkv, wkv, wb)
    row = lambda w: pl.BlockSpec((tm, w), lambda i: (i, 0))
    return pl.pallas_call(
        functools.partial(_inproj_a_kernel, e=e, wa=wa, wb=wb, wkv=wkv),
        out_shape=[jax.ShapeDtypeStruct((T, w), F32) for w in widths],
        grid=(T // tm,),
        in_specs=[row(D), _full(g.shape), _full(w_bf16.shape), _full(lb_logits.shape),
                  _full(q_gain_t.shape), _full(k_gain_t.shape), _full(bd.shape)],
        out_specs=[row(w) for w in widths],
        compiler_params=_cparams(("parallel",)),
        name="inproj_a",
    )(x, g, w_bf16, lb_logits, q_gain_t, k_gain_t, bd)


def _gla_intra(q, k, b, bx, v_bf, C, c):
    n = C // c
    lane = lax.broadcasted_iota(jnp.int32, (c, C), 1)
    trow = lax.broadcasted_iota(jnp.int32, (c, LANES), 0)
    ones = jnp.ones((LANES, LANES), BF16)
    rows = []
    for i in range(n):
        sl = slice(i * c, (i + 1) * c)
        qi, ki, bi = q[sl], k[sl], b[sl]
        if i > 0:
            b0 = bx[i * c:i * c + 1, :]
            ks = k * jnp.exp(jnp.minimum(b0 - b, 0.0))
            qh = qi * jnp.exp(bi - b0)
            a_i = jnp.where(lane < i * c, _dot_nt(qh.astype(BF16), ks.astype(BF16)), 0.0)
        else:
            a_i = jnp.zeros((c, C), F32)
        pieces = []
        for s in range(c):
            d = jnp.minimum(bi - bi[s:s + 1, :], 0.0)
            pieces.append(jnp.where(trow >= s, qi * ki[s:s + 1, :] * jnp.exp(d), 0.0))
        r = _dot(jnp.concatenate(pieces, axis=0).astype(BF16), ones)
        for s in range(c):
            a_i = jnp.where(lane == i * c + s, r[s * c:(s + 1) * c, :C], a_i)
        rows.append(a_i)
    a = rows[0] if n == 1 else jnp.concatenate(rows, axis=0)
    return _dot(a.astype(BF16), v_bf)


def _gla_prompt_kernel(q_ref, k_ref, v_ref, lf_ref, o_ref, s_ref, st_scr, *, C, c):
    j = pl.program_id(1)

    @pl.when(j == 0)
    def _():
        st_scr[...] = jnp.zeros_like(st_scr)

    q, k, v, lf = q_ref[...], k_ref[...], v_ref[...], lf_ref[...]
    rr = lax.broadcasted_iota(jnp.int32, (C, C), 0)
    cc = lax.broadcasted_iota(jnp.int32, (C, C), 1)
    tri = (cc <= rr).astype(BF16)
    l1, l2, l3 = _split3(lf)
    b = _dot(tri, l1) + _dot(tri, l2) + _dot(tri, l3)
    bx = b - lf
    st = st_scr[...]
    v_bf = v.astype(BF16)
    o = _dot_nt((q * jnp.exp(b)).astype(BF16), st.astype(BF16))
    o_ref[...] = o + _gla_intra(q, k, b, bx, v_bf, C, c)
    b_end = b[C - 1:C, :]
    kt = k * jnp.exp(b_end - b)
    st_new = st * jnp.exp(b_end) + _dot_tn(v_bf, kt.astype(BF16))
    st_scr[...] = st_new

    @pl.when(j == pl.num_programs(1) - 1)
    def _():
        s_ref[0] = st_new.T


def _gla_prompt(q, k, v, lf, C):
    T, W = q.shape
    H = W // DK_A
    assert T % C == 0 and C % GLA_SUB == 0
    blk = pl.BlockSpec((C, DK_A), lambda h, j: (j, h))
    return pl.pallas_call(
        functools.partial(_gla_prompt_kernel, C=C, c=GLA_SUB),
        out_shape=[jax.ShapeDtypeStruct((T, W), F32), jax.ShapeDtypeStruct((H, DK_A, DV_A), F32)],
        grid=(H, T // C),
        in_specs=[blk, blk, blk, blk],
        out_specs=[blk, pl.BlockSpec((1, DK_A, DV_A), lambda h, j: (h, 0, 0))],
        scratch_shapes=[pltpu.VMEM((DV_A, DK_A), F32)],
        compiler_params=_cparams(("parallel", "arbitrary")),
        name="gla_prompt",
    )(q, k, v, lf)


def _gla_sample_kernel(q_ref, k_ref, v_ref, lf_ref, s0_ref, o_ref, s_ref, *, T, H):
    pad = LANES - T
    row = lax.broadcasted_iota(jnp.int32, (T, LANES), 0)
    zpad = jnp.zeros((pad, LANES), F32)
    for h in range(H):
        sl = slice(h * DK_A, (h + 1) * DK_A)
        q, k, v, lf = q_ref[:, sl], k_ref[:, sl], v_ref[:, sl], lf_ref[:, sl]
        b = jnp.zeros((T, LANES), F32)
        for t in range(T):
            bt = jnp.sum(jnp.where(row <= t, lf, 0.0), axis=0, keepdims=True)
            b = jnp.where(row == t, bt, b)
        st = s0_ref[0, h].T
        v_bf = jnp.concatenate([v, zpad], axis=0).astype(BF16)
        qd = jnp.concatenate([q * jnp.exp(b), zpad], axis=0).astype(BF16)
        o = _dot_nt(qd, st.astype(BF16))[:T]
        ones = jnp.ones((LANES, LANES), BF16)
        lane = lax.broadcasted_iota(jnp.int32, (T, LANES), 1)
        pieces = []
        for s in range(T):
            d = jnp.minimum(b - b[s:s + 1, :], 0.0)
            pieces.append(jnp.where(row >= s, q * k[s:s + 1, :] * jnp.exp(d), 0.0))
        r = _dot(jnp.concatenate(pieces, axis=0).astype(BF16), ones)
        a = jnp.zeros((T, LANES), F32)
        for s in range(T):
            a = jnp.where(lane == s, r[s * T:(s + 1) * T, :], a)
        a_bf = jnp.concatenate([a, zpad], axis=0).astype(BF16)
        o_ref[:, sl] = o + _dot(a_bf, v_bf)[:T]
        b_end = b[T - 1:T, :]
        kt = jnp.concatenate([k * jnp.exp(b_end - b), zpad], axis=0).astype(BF16)
        st_new = st * jnp.exp(b_end) + _dot_tn(v_bf, kt)
        s_ref[0, h] = st_new.T


def _gla_sample(q, k, v, lf, s0, T):
    BT, W = q.shape
    B, H = s0.shape[0], s0.shape[1]
    assert BT == B * T and W == H * DK_A and T % 8 == 0 and T <= GLA_SUB
    blk = pl.BlockSpec((T, W), lambda i: (i, 0))
    sblk = pl.BlockSpec((1, H, DK_A, DV_A), lambda i: (i, 0, 0, 0))
    return pl.pallas_call(
        functools.partial(_gla_sample_kernel, T=T, H=H),
        out_shape=[jax.ShapeDtypeStruct((BT, W), F32), jax.ShapeDtypeStruct(s0.shape, F32)],
        grid=(B,),
        in_specs=[blk, blk, blk, blk, sblk],
        out_specs=[blk, sblk],
        compiler_params=_cparams(("parallel",)),
        name="gla_sample",
    )(q, k, v, lf, s0)


def _alibi_slope(n_heads, h):
    return 2.0 ** (-8.0 * (h + 1) / n_heads)


def _online_update(mi, s, v_bf, m_scr, l_scr, acc_scr):
    reps = s.shape[1] // LANES
    m_prev = m_scr[mi]
    m_new = jnp.maximum(m_prev, jnp.max(s, axis=1, keepdims=True))
    alpha = jnp.exp2(m_prev - m_new)
    p = jnp.exp2(s - jnp.tile(m_new, (1, reps)))
    l_scr[mi] = alpha * l_scr[mi] + jnp.sum(p, axis=1, keepdims=True)
    acc_scr[mi] = alpha * acc_scr[mi] + _dot(p.astype(BF16), v_bf)
    m_scr[mi] = m_new


def _diff_prompt_kernel(qi_ref, kj_ref, q_ref, k_ref, v_ref, lq1_ref, lk1_ref, lq2_ref, lk2_ref,
                        o_ref, m_scr, l_scr, acc_scr, *, B, n_heads, lam_init):
    n = pl.program_id(0)
    p = pl.program_id(1)
    i = qi_ref[p]
    j = kj_ref[p]

    @pl.when(j == 0)
    def _():
        m_scr[...] = jnp.full_like(m_scr, NEG_BIG)
        l_scr[...] = jnp.zeros_like(l_scr)
        acc_scr[...] = jnp.zeros_like(acc_scr)

    def tile(masked):
        k_bf = k_ref[...].astype(BF16)
        v_bf = v_ref[...].astype(BF16)
        lane = lax.broadcasted_iota(jnp.int32, (B, LANES), 1)
        col = lax.broadcasted_iota(jnp.int32, (1, B), 1).astype(F32)
        rel = col - ((i - j) * B).astype(F32)
        if masked:
            keep = (lax.broadcasted_iota(jnp.int32, (B, B), 1) <= lax.broadcasted_iota(jnp.int32, (B, B), 0))
        for g in range(G_B):
            qh = q_ref[:, g * LANES:(g + 1) * LANES]
            slope = jnp.where(n == 0, _alibi_slope(n_heads, g), _alibi_slope(n_heads, G_B + g)) * LOG2E
            bias = slope * rel
            for c in range(2):
                half = (lane < D_B) if c == 0 else (lane >= D_B)
                s = _dot_nt(jnp.where(half, qh, 0.0).astype(BF16), k_bf) + bias
                if masked:
                    s = jnp.where(keep, s, NEG_BIG)
                _online_update(g * 2 + c, s, v_bf, m_scr, l_scr, acc_scr)

    @pl.when(j < i)
    def _():
        tile(False)

    @pl.when(j == i)
    def _():
        tile(True)
        lam = _lam_value(lq1_ref[...], lk1_ref[...], lq2_ref[...], lk2_ref[...], lam_init)
        for g in range(G_B):
            o_ref[:, g * LANES:(g + 1) * LANES] = (acc_scr[2 * g] / l_scr[2 * g]
                                                   - lam * (acc_scr[2 * g + 1] / l_scr[2 * g + 1]))


def _diff_prompt(qb, kb, vb, lq1, lk1, lq2, lk2, lam_init, B):
    T = qb.shape[0]
    n_heads = qb.shape[1] // LANES
    nb = T // B
    assert T % B == 0 and n_heads == KVH_B * G_B
    qi = np.concatenate([np.full(i + 1, i, np.int32) for i in range(nb)])
    kj = np.concatenate([np.arange(i + 1, dtype=np.int32) for i in range(nb)])
    small = pl.BlockSpec(lq1.shape, lambda n, p, qi, kj: (0, 0))
    grid_spec = pltpu.PrefetchScalarGridSpec(
        num_scalar_prefetch=2,
        grid=(KVH_B, len(qi)),
        in_specs=[pl.BlockSpec((B, G_B * LANES), lambda n, p, qi, kj: (qi[p], n)),
                  pl.BlockSpec((B, LANES), lambda n, p, qi, kj: (kj[p], n)),
                  pl.BlockSpec((B, LANES), lambda n, p, qi, kj: (kj[p], n)),
                  small, small, small, small],
        out_specs=pl.BlockSpec((B, G_B * LANES), lambda n, p, qi, kj: (qi[p], n)),
        scratch_shapes=[pltpu.VMEM((2 * G_B, B, LANES), F32)] * 3,
    )
    return pl.pallas_call(
        functools.partial(_diff_prompt_kernel, B=B, n_heads=n_heads, lam_init=lam_init),
        out_shape=jax.ShapeDtypeStruct(qb.shape, F32),
        grid_spec=grid_spec,
        compiler_params=_cparams(("parallel", "arbitrary")),
        name="diff_prompt",
    )(jnp.asarray(qi), jnp.asarray(kj), qb, kb, vb, lq1, lk1, lq2, lk2)


def _diff_sample_kernel(pt_ref, q_ref, kn_ref, vn_ref, lq1_ref, lk1_ref, lq2_ref, lk2_ref, ck_ref, cv_ref,
                        o_ref, kbuf, vbuf, ksem, vsem, m_scr, l_scr, acc_scr,
                        *, P, page, n_chunks, n_seq, T, n_heads, lam_init):
    b = pl.program_id(0)
    c = pl.program_id(1)
    step = b * n_chunks + c
    slot = lax.rem(step, 2)
    past = n_chunks * P * page
    R = G_B * 2 * T

    def k_copy(pg, sl, p):
        return pltpu.make_async_copy(ck_ref.at[pg], kbuf.at[sl, p], ksem.at[sl])

    def v_copy(pg, sl, p):
        return pltpu.make_async_copy(cv_ref.at[pg], vbuf.at[sl, p], vsem.at[sl])

    def issue(bb, cc, sl):
        for p in range(P):
            pg = pt_ref[bb, cc * P + p]
            k_copy(pg, sl, p).start()
            v_copy(pg, sl, p).start()

    @pl.when(step == 0)
    def _():
        issue(0, 0, 0)

    nxt = step + 1

    @pl.when(nxt < n_seq * n_chunks)
    def _():
        issue(nxt // n_chunks, lax.rem(nxt, n_chunks), 1 - slot)

    @pl.when(c == 0)
    def _():
        m_scr[...] = jnp.full_like(m_scr, NEG_BIG)
        l_scr[...] = jnp.zeros_like(l_scr)
        acc_scr[...] = jnp.zeros_like(acc_scr)

    for p in range(P):
        k_copy(0, slot, p).wait()
        v_copy(0, slot, p).wait()

    lane = lax.broadcasted_iota(jnp.int32, (T, LANES), 1)
    rrow = lax.broadcasted_iota(jnp.int32, (R, 1), 0)
    col = lax.broadcasted_iota(jnp.int32, (1, P * page), 1)
    rel = (col + (c * (P * page) - past)).astype(F32)

    def lhs_and_slopes(n):
        blocks = []
        for g in range(G_B):
            qh = q_ref[:, (n * G_B + g) * LANES:(n * G_B + g + 1) * LANES]
            for cs in range(2):
                half = (lane < D_B) if cs == 0 else (lane >= D_B)
                blocks.append(jnp.where(half, qh, 0.0))
        lhs = jnp.concatenate(blocks, axis=0).astype(BF16)
        slopes = jnp.where(rrow < 2 * T, _alibi_slope(n_heads, n * G_B), _alibi_slope(n_heads, n * G_B + 1)) * LOG2E
        return lhs, slopes

    for n in range(KVH_B):
        lhs, slopes = lhs_and_slopes(n)
        kt = jnp.concatenate([kbuf[slot, p, n * LANES:(n + 1) * LANES, :] for p in range(P)], axis=1).astype(BF16)
        vv = jnp.concatenate([vbuf[slot, p, pl.ds(n, page, stride=KVH_B), :] for p in range(P)], axis=0).astype(BF16)
        s = _dot(lhs, kt) + slopes * rel
        _online_update(n, s, vv, m_scr, l_scr, acc_scr)

    @pl.when(c == n_chunks - 1)
    def _():
        lam = _lam_value(lq1_ref[...], lk1_ref[...], lq2_ref[...], lk2_ref[...], lam_init)
        zpad = jnp.zeros((LANES - T, LANES), F32)
        ncol = lax.broadcasted_iota(jnp.int32, (R, LANES), 1)
        trow = lax.rem(lax.broadcasted_iota(jnp.int32, (R, LANES), 0), T)
        keep = ncol <= trow
        for n in range(KVH_B):
            lhs, slopes = lhs_and_slopes(n)
            kn = jnp.concatenate([kn_ref[:, n * LANES:(n + 1) * LANES], zpad], axis=0).astype(BF16)
            vn = jnp.concatenate([vn_ref[:, n * LANES:(n + 1) * LANES], zpad], axis=0).astype(BF16)
            s = _dot_nt(lhs, kn) + slopes * ncol.astype(F32)
            s = jnp.where(keep, s, NEG_BIG)
            _online_update(n, s, vn, m_scr, l_scr, acc_scr)
            res = acc_scr[n] / l_scr[n]
            for g in range(G_B):
                r0 = g * 2 * T
                o_ref[:, (n * G_B + g) * LANES:(n * G_B + g + 1) * LANES] = (
                    res[r0:r0 + T] - lam * res[r0 + T:r0 + 2 * T])


def _diff_sample(qb, kb_new, vb_new, page_table, kt_pages, v_pages, lq1, lk1, lq2, lk2, lam_init, T, P):
    n_seq, n_pages = page_table.shape
    page = kt_pages.shape[2]
    n_heads = qb.shape[1] // LANES
    assert n_pages % P == 0 and qb.shape[0] == n_seq * T and page == LANES
    n_chunks = n_pages // P
    R = G_B * 2 * T
    small = pl.BlockSpec(lq1.shape, lambda b, c, pt: (0, 0))
    rows = lambda w: pl.BlockSpec((T, w), lambda b, c, pt: (b, 0))
    grid_spec = pltpu.PrefetchScalarGridSpec(
        num_scalar_prefetch=1,
        grid=(n_seq, n_chunks),
        in_specs=[rows(qb.shape[1]), rows(kb_new.shape[1]), rows(vb_new.shape[1]), small, small, small, small,
                  pl.BlockSpec(memory_space=pl.ANY), pl.BlockSpec(memory_space=pl.ANY)],
        out_specs=rows(qb.shape[1]),
        scratch_shapes=[pltpu.VMEM((2, P) + kt_pages.shape[1:], F32),
                        pltpu.VMEM((2, P) + v_pages.shape[1:], F32),
                        pltpu.SemaphoreType.DMA((2,)), pltpu.SemaphoreType.DMA((2,)),
                        pltpu.VMEM((KVH_B, R, LANES), F32), pltpu.VMEM((KVH_B, R, LANES), F32),
                        pltpu.VMEM((KVH_B, R, LANES), F32)],
    )
    return pl.pallas_call(
        functools.partial(_diff_sample_kernel, P=P, page=page, n_chunks=n_chunks, n_seq=n_seq, T=T,
                          n_heads=n_heads, lam_init=lam_init),
        out_shape=jax.ShapeDtypeStruct(qb.shape, F32),
        grid_spec=grid_spec,
        compiler_params=_cparams(("arbitrary", "arbitrary")),
        name="diff_sample",
    )(page_table, qb, kb_new, vb_new, lq1, lk1, lq2, lk2, kt_pages, v_pages)


def _outproj_a_kernel(x_ref, oa_ref, gsa_ref, ob_ref, gsb_ref, og_ref, sg_ref, w_ref, y_ref, *, lam_init):
    def branch(o_ref, gate_ref, gain, scale):
        parts = []
        for h in range(o_ref.shape[1] // LANES):
            sl = slice(h * LANES, (h + 1) * LANES)
            parts.append(_rms_rows(o_ref[:, sl], gain) * scale * gate_ref[:, sl])
        return jnp.concatenate(parts, axis=1).astype(BF16)

    ya = branch(oa_ref, gsa_ref, og_ref[...], 1.0)
    yb = branch(ob_ref, gsb_ref, sg_ref[...], 1.0 - lam_init)
    wa = oa_ref.shape[1]
    y_ref[...] = x_ref[...] + _dot(ya, w_ref[0:wa, :]) + _dot(yb, w_ref[wa:, :])


def _outproj_a(x, oa, gsa, ob, gsb, o_gain, subln, w_bf16, lam_init, tm):
    T, D = x.shape
    row = lambda w: pl.BlockSpec((tm, w), lambda i: (i, 0))
    return pl.pallas_call(
        functools.partial(_outproj_a_kernel, lam_init=lam_init),
        out_shape=jax.ShapeDtypeStruct((T, D), F32),
        grid=(T // tm,),
        in_specs=[row(D), row(oa.shape[1]), row(oa.shape[1]), row(ob.shape[1]), row(ob.shape[1]),
                  _full(o_gain.shape), _full(subln.shape), _full(w_bf16.shape)],
        out_specs=row(D),
        compiler_params=_cparams(("parallel",)),
        name="outproj_a",
    )(x, oa, gsa, ob, gsb, o_gain, subln, w_bf16)


def _inproj_c_kernel(x_ref, g_ref, w_ref, qg_ref, kg_ref, bd_ref, q_ref, k_ref, v_ref, gs_ref, *, wq, wkv):
    hb = _rms_rows(x_ref[...], g_ref[...]).astype(BF16)
    bd = bd_ref[...]
    q_ref[...] = _group_rms(_dot(hb, w_ref[:, 0:wq]), qg_ref[...], bd) * (HD_C ** -0.5 * LOG2E)
    k_ref[...] = _group_rms(_dot(hb, w_ref[:, wq:wq + wkv]), kg_ref[...], bd)
    v_ref[...] = _dot(hb, w_ref[:, wq + wkv:wq + 2 * wkv])
    gs_ref[...] = _silu(_dot(hb, w_ref[:, wq + 2 * wkv:]))


def _inproj_c(x, g, w_bf16, q_gain_t, k_gain_t, bd, tm):
    T, D = x.shape
    wq, wkv = q_gain_t.shape[1], k_gain_t.shape[1]
    widths = (wq, wkv, wkv, wq)
    row = lambda w: pl.BlockSpec((tm, w), lambda i: (i, 0))
    return pl.pallas_call(
        functools.partial(_inproj_c_kernel, wq=wq, wkv=wkv),
        out_shape=[jax.ShapeDtypeStruct((T, w), F32) for w in widths],
        grid=(T // tm,),
        in_specs=[row(D), _full(g.shape), _full(w_bf16.shape), _full(q_gain_t.shape), _full(k_gain_t.shape),
                  _full(bd.shape)],
        out_specs=[row(w) for w in widths],
        compiler_params=_cparams(("parallel",)),
        name="inproj_c",
    )(x, g, w_bf16, q_gain_t, k_gain_t, bd)


def _swa_softmax_pv(s, sink, v_list):
    m = sink
    for piece in s:
        m = jnp.maximum(m, jnp.max(piece, axis=1, keepdims=True))
    den = jnp.exp2(sink - m)
    out = None
    for piece, pv in zip(s, v_list):
        p = jnp.exp2(piece - m)
        den = den + jnp.sum(p, axis=1, keepdims=True)
        term = pv(p.astype(BF16))
        out = term if out is None else out + term
    return out / den


def _swa_prompt_kernel(q_ref, kc_ref, kp_ref, vc_ref, vp_ref, snk_ref, o_ref, *, n_heads):
    i = pl.program_id(0)
    W = q_ref.shape[0]
    n_pairs = n_heads // KVH_C
    lane = lax.broadcasted_iota(jnp.int32, (W, LANES), 1)
    r = lax.broadcasted_iota(jnp.int32, (W, W), 0)
    c = lax.broadcasted_iota(jnp.int32, (W, W), 1)
    keep_cur = c <= r
    keep_prev = jnp.logical_and(c >= r, i > 0)
    dist_cur = (r - c).astype(F32)
    dist_prev = (r - c + W).astype(F32)
    kc, kp = kc_ref[...].astype(BF16), kp_ref[...].astype(BF16)
    vc, vp = vc_ref[...].astype(BF16), vp_ref[...].astype(BF16)
    for m in range(n_pairs):
        qp = q_ref[:, m * LANES:(m + 1) * LANES]
        outs = []
        for n in range(KVH_C):
            h = n * n_pairs + m
            slope = _alibi_slope(n_heads, h) * LOG2E
            half = (lane < HD_C) if n == 0 else (lane >= HD_C)
            qm = jnp.where(half, qp, 0.0).astype(BF16)
            s_cur = jnp.where(keep_cur, _dot_nt(qm, kc) - slope * dist_cur, NEG_BIG)
            s_prev = jnp.where(keep_prev, _dot_nt(qm, kp) - slope * dist_prev, NEG_BIG)
            sink = snk_ref[:, h:h + 1] * LOG2E
            outs.append(_swa_softmax_pv([s_prev, s_cur], sink, [lambda p: _dot(p, vp), lambda p: _dot(p, vc)]))
        o_ref[:, m * LANES:(m + 1) * LANES] = jnp.where(lane < HD_C, outs[0], outs[1])


def _swa_prompt(q, k, v, sinks):
    T, Wq = q.shape
    n_heads = Wq // HD_C
    nb = T // WINDOW
    cur = lambda w: pl.BlockSpec((WINDOW, w), lambda i: (i, 0))
    prev = lambda w: pl.BlockSpec((WINDOW, w), lambda i: (jnp.maximum(i - 1, 0), 0))
    kw = k.shape[1]
    return pl.pallas_call(
        functools.partial(_swa_prompt_kernel, n_heads=n_heads),
        out_shape=jax.ShapeDtypeStruct(q.shape, F32),
        grid=(nb,),
        in_specs=[cur(Wq), cur(kw), prev(kw), cur(kw), prev(kw), _full(sinks.shape)],
        out_specs=cur(Wq),
        compiler_params=_cparams(("parallel",)),
        name="swa_prompt",
    )(q, k, k, v, v, sinks)


def _swa_sample_kernel(q_ref, kn_ref, vn_ref, ckt_ref, cvt_ref, snk_ref, o_ref, *, T, n_heads):
    n_pairs = n_heads // KVH_C
    R = n_heads * T
    lane = lax.broadcasted_iota(jnp.int32, (T, LANES), 1)
    blocks = []
    for n in range(KVH_C):
        half = (lane < HD_C) if n == 0 else (lane >= HD_C)
        for m in range(n_pairs):
            blocks.append(jnp.where(half, q_ref[:, m * LANES:(m + 1) * LANES], 0.0))
    lhs = jnp.concatenate(blocks, axis=0).astype(BF16)
    rr = lax.broadcasted_iota(jnp.int32, (R, WINDOW), 0)
    cc = lax.broadcasted_iota(jnp.int32, (R, WINDOW), 1)
    t = lax.rem(rr, T)
    hrow = lax.broadcasted_iota(jnp.int32, (R, 1), 0) // T
    slope = jnp.exp2(-8.0 * (hrow + 1).astype(F32) / n_heads) * LOG2E
    sink = jnp.zeros((R, 1), F32)
    for h in range(n_heads):
        sink = jnp.where(hrow == h, snk_ref[:, h:h + 1], sink)
    sink = sink * LOG2E
    zpad = jnp.zeros((WINDOW - T, LANES), F32)
    kn = jnp.concatenate([kn_ref[...], zpad], axis=0).astype(BF16)
    vn = jnp.concatenate([vn_ref[...], zpad], axis=0).astype(BF16)
    ckt = ckt_ref[0].astype(BF16)
    cvt = cvt_ref[0].astype(BF16)
    s_c = jnp.where(cc >= t, _dot(lhs, ckt) - slope * (t + WINDOW - cc).astype(F32), NEG_BIG)
    s_n = jnp.where(cc <= t, _dot_nt(lhs, kn) - slope * (t - cc).astype(F32), NEG_BIG)
    res = _swa_softmax_pv([s_c, s_n], sink, [lambda p: _dot_nt(p, cvt), lambda p: _dot(p, vn)])
    half_rows = n_pairs * T
    for m in range(n_pairs):
        lo = res[m * T:(m + 1) * T]
        hi = res[half_rows + m * T:half_rows + (m + 1) * T]
        o_ref[:, m * LANES:(m + 1) * LANES] = jnp.where(lane < HD_C, lo, hi)


def _swa_sample(q, k_new, v_new, ckt, cvt, sinks, T):
    BT, Wq = q.shape
    B = ckt.shape[0]
    n_heads = Wq // HD_C
    rows = lambda w: pl.BlockSpec((T, w), lambda b: (b, 0))
    cache = pl.BlockSpec((1,) + ckt.shape[1:], lambda b: (b, 0, 0))
    return pl.pallas_call(
        functools.partial(_swa_sample_kernel, T=T, n_heads=n_heads),
        out_shape=jax.ShapeDtypeStruct(q.shape, F32),
        grid=(B,),
        in_specs=[rows(Wq), rows(k_new.shape[1]), rows(v_new.shape[1]), cache, cache, _full(sinks.shape)],
        out_specs=rows(Wq),
        compiler_params=_cparams(("parallel",)),
        name="swa_sample",
    )(q, k_new, v_new, ckt, cvt, sinks)


def _outproj_c_kernel(x_ref, o_ref, gs_ref, w_ref, y_ref):
    y_ref[...] = x_ref[...] + _dot((o_ref[...] * gs_ref[...]).astype(BF16), w_ref[...])


def _outproj_c(x, o, gs, w_bf16, tm):
    T, D = x.shape
    row = lambda w: pl.BlockSpec((tm, w), lambda i: (i, 0))
    return pl.pallas_call(
        _outproj_c_kernel,
        out_shape=jax.ShapeDtypeStruct((T, D), F32),
        grid=(T // tm,),
        in_specs=[row(D), row(o.shape[1]), row(o.shape[1]), _full(w_bf16.shape)],
        out_specs=row(D),
        compiler_params=_cparams(("parallel",)),
        name="outproj_c",
    )(x, o, gs, w_bf16)


TOKEN_TILE = 256
GLA_CHUNK = 128
DIFF_BLOCK = 512
DIFF_PAGES_PER_STEP = 16


def _pair_perm(n_heads):
    n_pairs = n_heads // KVH_C
    cols = []
    for m in range(n_pairs):
        for n in range(KVH_C):
            h = n * n_pairs + m
            cols.extend(range(h * HD_C, (h + 1) * HD_C))
    return np.asarray(cols, np.int32)


def kernel(x_prompt, x_sample, state_hgrn, cache_k_diff, cache_v_diff, cache_k_swa, cache_v_swa, page_table, norm_a, w_in_a, w_out_a, lb_logits, hgrn_out_gain, diff_q_gain, diff_k_gain, diff_subln_gain, lam_q1, lam_k1, lam_q2, lam_k2, norm_c, w_in_c, w_out_c, swa_q_gain, swa_k_gain, sinks):
    batch, seq, d_model = x_prompt.shape
    n_seq, t_dec, _ = x_sample.shape
    assert batch == 1
    n_even, n_odd = norm_a.shape[0], norm_c.shape[0]
    depth = n_even + n_odd
    h_a = state_hgrn.shape[2]
    n_pool, page = cache_k_diff.shape[1], cache_k_diff.shape[2]
    w_b = w_out_a.shape[1] - h_a * DV_A
    h_b = w_b // (2 * D_B)
    h_c = sinks.shape[1]

    xp = x_prompt.reshape(seq, d_model)
    xs = x_sample.reshape(n_seq * t_dec, d_model)
    bd64 = _block_diag_mean(2 * LANES, D_B)
    bd64_c = _block_diag_mean(LANES, HD_C)
    perm = _pair_perm(h_c)

    hs_p, hs_s, kdp, vdp, kds, vds, ksp, vsp, kss, vss = ([] for _ in range(10))
    for l in range(depth):
        if l % 2 == 0:
            e = l // 2
            lam_init = 0.8 - 0.6 * math.exp(-0.3 * l)
            w_in = w_in_a[e].astype(BF16)
            w_out = w_out_a[e].astype(BF16)
            g = norm_a[e:e + 1]
            qg = jnp.tile(diff_q_gain[e:e + 1], (1, h_b * 2))
            kg = jnp.tile(diff_k_gain[e:e + 1], (1, KVH_B * 2))
            lam_args = (lam_q1[e:e + 1], lam_k1[e:e + 1], lam_q2[e:e + 1], lam_k2[e:e + 1])
            kt_pages = jnp.transpose(cache_k_diff[e], (0, 2, 3, 4, 1)).reshape(n_pool, KVH_B * 2 * D_B, page)
            v_pages = cache_v_diff[e].reshape(n_pool, page * KVH_B, 2 * D_B)

            pa = _inproj_a(xp, g, w_in, lb_logits, qg, kg, bd64, e, TOKEN_TILE)
            sa = _inproj_a(xs, g, w_in, lb_logits, qg, kg, bd64, e, TOKEN_TILE)
            qa_p, lf_p, kk_p, ia_p, gsa_p, qb_p, kb_p, vb_p, gsb_p = pa
            qa_s, lf_s, kk_s, ia_s, gsa_s, qb_s, kb_s, vb_s, gsb_s = sa

            oa_p, st_p = _gla_prompt(qa_p, kk_p, ia_p, lf_p, GLA_CHUNK)
            oa_s, st_s = _gla_sample(qa_s, kk_s, ia_s, lf_s, state_hgrn[e], t_dec)
            ob_p = _diff_prompt(qb_p, kb_p, vb_p, *lam_args, lam_init, DIFF_BLOCK)
            ob_s = _diff_sample(qb_s, kb_s, vb_s, page_table, kt_pages, v_pages, *lam_args, lam_init, t_dec,
                                DIFF_PAGES_PER_STEP)

            og, sg = hgrn_out_gain[e:e + 1], diff_subln_gain[e:e + 1]
            xp = _outproj_a(xp, oa_p, gsa_p, ob_p, gsb_p, og, sg, w_out, lam_init, TOKEN_TILE)
            xs = _outproj_a(xs, oa_s, gsa_s, ob_s, gsb_s, og, sg, w_out, lam_init, TOKEN_TILE)

            hs_p.append(st_p[None])
            hs_s.append(st_s)
            kdp.append(kb_p.reshape(batch, seq, KVH_B, 2, D_B))
            vdp.append(vb_p.reshape(batch, seq, KVH_B, 2 * D_B))
            kds.append(kb_s.reshape(n_seq, t_dec, KVH_B, 2, D_B))
            vds.append(vb_s.reshape(n_seq, t_dec, KVH_B, 2 * D_B))
        else:
            o = l // 2
            wq = h_c * HD_C
            wkv = KVH_C * HD_C
            w_full = w_in_c[o]
            w_in = jnp.concatenate([w_full[:, :wq][:, perm], w_full[:, wq:wq + 2 * wkv],
                                    w_full[:, wq + 2 * wkv:][:, perm]], axis=1).astype(BF16)
            w_out = w_out_c[o][perm, :].astype(BF16)
            g = norm_c[o:o + 1]
            qg = jnp.tile(swa_q_gain[o:o + 1], (1, h_c))
            kg = jnp.tile(swa_k_gain[o:o + 1], (1, KVH_C))
            snk = sinks[o:o + 1]
            ckt = jnp.transpose(cache_k_swa[o], (0, 2, 3, 1)).reshape(n_seq, wkv, WINDOW)
            cvt = jnp.transpose(cache_v_swa[o], (0, 2, 3, 1)).reshape(n_seq, wkv, WINDOW)

            q_p, k_p, v_p, gs_p = _inproj_c(xp, g, w_in, qg, kg, bd64_c, TOKEN_TILE)
            q_s, k_s, v_s, gs_s = _inproj_c(xs, g, w_in, qg, kg, bd64_c, TOKEN_TILE)
            o_p = _swa_prompt(q_p, k_p, v_p, snk)
            o_s = _swa_sample(q_s, k_s, v_s, ckt, cvt, snk, t_dec)
            xp = _outproj_c(xp, o_p, gs_p, w_out, TOKEN_TILE)
            xs = _outproj_c(xs, o_s, gs_s, w_out, TOKEN_TILE)

            ksp.append(k_p[-WINDOW:].reshape(batch, WINDOW, KVH_C, HD_C))
            vsp.append(v_p[-WINDOW:].reshape(batch, WINDOW, KVH_C, HD_C))
            k_new = k_s.reshape(n_seq, t_dec, KVH_C, HD_C)
            v_new = v_s.reshape(n_seq, t_dec, KVH_C, HD_C)
            kss.append(jnp.concatenate([cache_k_swa[o], k_new], axis=1)[:, -WINDOW:])
            vss.append(jnp.concatenate([cache_v_swa[o], v_new], axis=1)[:, -WINDOW:])
    return (xp.reshape(batch, seq, d_model), xs.reshape(n_seq, t_dec, d_model),
            jnp.stack(hs_p), jnp.stack(hs_s), jnp.stack(kdp), jnp.stack(vdp), jnp.stack(kds), jnp.stack(vds),
            jnp.stack(ksp), jnp.stack(vsp), jnp.stack(kss), jnp.stack(vss))
```

```python
import functools
import math

import jax
import jax.numpy as jnp
import ml_dtypes
import numpy as np
from jax import lax
from jax.experimental import pallas as pl
from jax.experimental.pallas import tpu as pltpu

F32 = jnp.float32
BF16 = jnp.bfloat16

EPS = 1e-6
LOG2E = 1.4426950408889634
NEG_BIG = -1e30
LANES = 128
VMEM_LIMIT_BYTES = 56 * 1024 * 1024

DK_A = 128
DV_A = 128
D_B = 64
KVH_B = 2
G_B = 2
HD_C = 64
KVH_C = 2
WINDOW = 128
GLA_SUB = 16
FAST_LOGIT_BOUND = 60.0


def _dot(a, b):
    return jnp.dot(a, b, preferred_element_type=F32)


def _dot_nt(a, b):
    return lax.dot_general(a, b, (((1,), (1,)), ((), ())), preferred_element_type=F32)


def _dot_tn(a, b):
    return lax.dot_general(a, b, (((0,), (0,)), ((), ())), preferred_element_type=F32)


def _silu(x):
    return x * jax.nn.sigmoid(x)


def _rms_rows(x, gain):
    return x * lax.rsqrt(jnp.mean(x * x, axis=-1, keepdims=True) + EPS) * gain


def _split3(x):
    x1 = x.astype(BF16)
    r1 = x - x1.astype(F32)
    x2 = r1.astype(BF16)
    x3 = (r1 - x2.astype(F32)).astype(BF16)
    return x1, x2, x3


def _group_meansq(x, bd):
    s = x * x
    hi = s.astype(BF16)
    lo = (s - hi.astype(F32)).astype(BF16)
    return _dot(hi, bd) + _dot(lo, bd)


def _group_rms(x, gain, bd):
    wb = bd.shape[0]
    outs = []
    for j in range(x.shape[1] // wb):
        part = x[:, j * wb:(j + 1) * wb]
        ms = _group_meansq(part, bd)
        outs.append(part * lax.rsqrt(ms + EPS) * gain[:, j * wb:(j + 1) * wb])
    return outs[0] if len(outs) == 1 else jnp.concatenate(outs, axis=1)


def _block_diag_mean(width, group):
    r = np.arange(width)[:, None] // group
    c = np.arange(width)[None, :] // group
    return jnp.asarray((r == c).astype(np.float32) / group, dtype=BF16)


def _lam_value(lq1, lk1, lq2, lk2, lam_init):
    a = jnp.sum(lq1 * lk1, axis=-1, keepdims=True)
    b = jnp.sum(lq2 * lk2, axis=-1, keepdims=True)
    return jnp.exp(a) - jnp.exp(b) + lam_init


def _split3_const(x):
    x = np.float32(x)
    a = np.float32(ml_dtypes.bfloat16(x))
    b = np.float32(ml_dtypes.bfloat16(np.float32(x - a)))
    c = np.float32(ml_dtypes.bfloat16(np.float32(np.float32(x - a) - b)))
    return float(a), float(b), float(c)


def _lane_tile_sum(p):
    out = p[:, 0:LANES]
    for t in range(1, p.shape[1] // LANES):
        out = out + p[:, t * LANES:(t + 1) * LANES]
    return out


def _cparams(sem):
    return pltpu.CompilerParams(dimension_semantics=sem, vmem_limit_bytes=VMEM_LIMIT_BYTES)


def _full(shape):
    nd = len(shape)
    return pl.BlockSpec(shape, lambda *_: (0,) * nd)


def _inproj_a_kernel(x_ref, g_ref, w_ref, lbl_ref, qg_ref, kg_ref, bd_ref,
                     qa_ref, lf_ref, kk_ref, ia_ref, gsa_ref, qb_ref, kb_ref, vb_ref, gsb_ref, *, e, wa, wb, wkv):
    hb = _rms_rows(x_ref[...], g_ref[...]).astype(BF16)
    bd = bd_ref[...]
    o = [0]

    def seg(width):
        z = _dot(hb, w_ref[:, o[0]:o[0] + width])
        o[0] += width
        return z

    qa = seg(wa)
    qa_ref[...] = _silu(qa) * (DK_A ** -0.5)
    lbl = lbl_ref[...]
    ex = jnp.exp(lbl - jnp.max(lbl, axis=0, keepdims=True))
    sm = ex / jnp.sum(ex, axis=0, keepdims=True)
    lb = jnp.sum(sm[0:e + 1, :], axis=0, keepdims=True)
    f = lb + (1.0 - lb) * jax.nn.sigmoid(seg(wa))
    lf_ref[...] = jnp.log(f)
    kk_ref[...] = 1.0 - f
    ia_ref[...] = seg(wa)
    gsa_ref[...] = _silu(seg(wa))
    qb_ref[...] = _group_rms(seg(wb), qg_ref[...], bd) * (D_B ** -0.5 * LOG2E)
    kb_ref[...] = _group_rms(seg(wkv), kg_ref[...], bd)
    vb_ref[...] = seg(wkv)
    gsb_ref[...] = _silu(seg(wb))


def _inproj_a(x, g, w_bf16, lb_logits, q_gain_t, k_gain_t, bd, e, tm):
    T, D = x.shape
    wa, wb, wkv = lb_logits.shape[1], q_gain_t.shape[1], k_gain_t.shape[1]
    n_in = w_bf16.shape[1]
    assert n_in == 4 * wa + 2 * wb + 2 * wkv and T % tm == 0
    widths = (wa, wa, wa, wa, wa, wb, wkv, wkv, wb)
    row = lambda w: pl.BlockSpec((tm, w), lambda i: (i, 0))
    return pl.pallas_call(
        functools.partial(_inproj_a_kernel, e=e, wa=wa, wb=wb, wkv=wkv),
        out_shape=[jax.ShapeDtypeStruct((T, w), F32) for w in widths],
        grid=(T // tm,),
        in_specs=[row(D), _full(g.shape), _full(w_bf16.shape), _full(lb_logits.shape),
                  _full(q_gain_t.shape), _full(k_gain_t.shape), _full(bd.shape)],
        out_specs=[row(w) for w in widths],
        compiler_params=_cparams(("parallel",)),
        name="inproj_a",
    )(x, g, w_bf16, lb_logits, q_gain_t, k_gain_t, bd)


def _gla_intra(q, k, b, bx, v_bf, C, c):
    n = C // c
    lane = lax.broadcasted_iota(jnp.int32, (c, C), 1)
    trow = lax.broadcasted_iota(jnp.int32, (c, LANES), 0)
    ones = jnp.ones((LANES, LANES), BF16)
    rows = []
    for i in range(n):
        sl = slice(i * c, (i + 1) * c)
        qi, ki, bi = q[sl], k[sl], b[sl]
        if i > 0:
            b0 = bx[i * c:i * c + 1, :]
            ks = k * jnp.exp(jnp.minimum(b0 - b, 0.0))
            qh = qi * jnp.exp(bi - b0)
            a_i = jnp.where(lane < i * c, _dot_nt(qh.astype(BF16), ks.astype(BF16)), 0.0)
        else:
            a_i = jnp.zeros((c, C), F32)
        pieces = []
        for s in range(c):
            d = jnp.minimum(bi - bi[s:s + 1, :], 0.0)
            pieces.append(jnp.where(trow >= s, qi * ki[s:s + 1, :] * jnp.exp(d), 0.0))
        r = _dot(jnp.concatenate(pieces, axis=0).astype(BF16), ones)
        for s in range(c):
            a_i = jnp.where(lane == i * c + s, r[s * c:(s + 1) * c, :C], a_i)
        rows.append(a_i)
    a = rows[0] if n == 1 else jnp.concatenate(rows, axis=0)
    return _dot(a.astype(BF16), v_bf)


def _gla_prompt_kernel(q_ref, k_ref, v_ref, lf_ref, o_ref, s_ref, st_scr, *, C, c):
    j = pl.program_id(1)

    @pl.when(j == 0)
    def _():
        st_scr[...] = jnp.zeros_like(st_scr)

    q, k, v, lf = q_ref[...], k_ref[...], v_ref[...], lf_ref[...]
    rr = lax.broadcasted_iota(jnp.int32, (C, C), 0)
    cc = lax.broadcasted_iota(jnp.int32, (C, C), 1)
    tri = (cc <= rr).astype(BF16)
    l1, l2, l3 = _split3(lf)
    b = _dot(tri, l1) + _dot(tri, l2) + _dot(tri, l3)
    bx = b - lf
    st = st_scr[...]
    v_bf = v.astype(BF16)
    o = _dot_nt((q * jnp.exp(b)).astype(BF16), st.astype(BF16))
    o_ref[...] = o + _gla_intra(q, k, b, bx, v_bf, C, c)
    b_end = b[C - 1:C, :]
    kt = k * jnp.exp(b_end - b)
    st_new = st * jnp.exp(b_end) + _dot_tn(v_bf, kt.astype(BF16))
    st_scr[...] = st_new

    @pl.when(j == pl.num_programs(1) - 1)
    def _():
        s_ref[0] = st_new.T


def _gla_prompt(q, k, v, lf, C):
    T, W = q.shape
    H = W // DK_A
    assert T % C == 0 and C % GLA_SUB == 0
    blk = pl.BlockSpec((C, DK_A), lambda h, j: (j, h))
    return pl.pallas_call(
        functools.partial(_gla_prompt_kernel, C=C, c=GLA_SUB),
        out_shape=[jax.ShapeDtypeStruct((T, W), F32), jax.ShapeDtypeStruct((H, DK_A, DV_A), F32)],
        grid=(H, T // C),
        in_specs=[blk, blk, blk, blk],
        out_specs=[blk, pl.BlockSpec((1, DK_A, DV_A), lambda h, j: (h, 0, 0))],
        scratch_shapes=[pltpu.VMEM((DV_A, DK_A), F32)],
        compiler_params=_cparams(("parallel", "arbitrary")),
        name="gla_prompt",
    )(q, k, v, lf)


def _gla_sample_kernel(q_ref, k_ref, v_ref, lf_ref, s0_ref, o_ref, s_ref, *, T, H):
    pad = LANES - T
    row = lax.broadcasted_iota(jnp.int32, (T, LANES), 0)
    zpad = jnp.zeros((pad, LANES), F32)
    for h in range(H):
        sl = slice(h * DK_A, (h + 1) * DK_A)
        q, k, v, lf = q_ref[:, sl], k_ref[:, sl], v_ref[:, sl], lf_ref[:, sl]
        b = jnp.zeros((T, LANES), F32)
        for t in range(T):
            bt = jnp.sum(jnp.where(row <= t, lf, 0.0), axis=0, keepdims=True)
            b = jnp.where(row == t, bt, b)
        st = s0_ref[0, h].T
        v_bf = jnp.concatenate([v, zpad], axis=0).astype(BF16)
        qd = jnp.concatenate([q * jnp.exp(b), zpad], axis=0).astype(BF16)
        o = _dot_nt(qd, st.astype(BF16))[:T]
        ones = jnp.ones((LANES, LANES), BF16)
        lane = lax.broadcasted_iota(jnp.int32, (T, LANES), 1)
        pieces = []
        for s in range(T):
            d = jnp.minimum(b - b[s:s + 1, :], 0.0)
            pieces.append(jnp.where(row >= s, q * k[s:s + 1, :] * jnp.exp(d), 0.0))
        r = _dot(jnp.concatenate(pieces, axis=0).astype(BF16), ones)
        a = jnp.zeros((T, LANES), F32)
        for s in range(T):
            a = jnp.where(lane == s, r[s * T:(s + 1) * T, :], a)
        a_bf = jnp.concatenate([a, zpad], axis=0).astype(BF16)
        o_ref[:, sl] = o + _dot(a_bf, v_bf)[:T]
        b_end = b[T - 1:T, :]
        kt = jnp.concatenate([k * jnp.exp(b_end - b), zpad], axis=0).astype(BF16)
        st_new = st * jnp.exp(b_end) + _dot_tn(v_bf, kt)
        s_ref[0, h] = st_new.T


def _gla_sample(q, k, v, lf, s0, T):
    BT, W = q.shape
    B, H = s0.shape[0], s0.shape[1]
    assert BT == B * T and W == H * DK_A and T % 8 == 0 and T <= GLA_SUB
    blk = pl.BlockSpec((T, W), lambda i: (i, 0))
    sblk = pl.BlockSpec((1, H, DK_A, DV_A), lambda i: (i, 0, 0, 0))
    return pl.pallas_call(
        functools.partial(_gla_sample_kernel, T=T, H=H),
        out_shape=[jax.ShapeDtypeStruct((BT, W), F32), jax.ShapeDtypeStruct(s0.shape, F32)],
        grid=(B,),
        in_specs=[blk, blk, blk, blk, sblk],
        out_specs=[blk, sblk],
        compiler_params=_cparams(("parallel",)),
        name="gla_sample",
    )(q, k, v, lf, s0)


def _alibi_slope(n_heads, h):
    return 2.0 ** (-8.0 * (h + 1) / n_heads)


def _online_update(mi, s, v_bf, m_scr, l_scr, acc_scr):
    reps = s.shape[1] // LANES
    m_prev = m_scr[mi]
    m_new = jnp.maximum(m_prev, jnp.max(s, axis=1, keepdims=True))
    alpha = jnp.exp2(m_prev - m_new)
    p = jnp.exp2(s - jnp.tile(m_new, (1, reps)))
    l_scr[mi] = alpha * l_scr[mi] + jnp.sum(p, axis=1, keepdims=True)
    acc_scr[mi] = alpha * acc_scr[mi] + _dot(p.astype(BF16), v_bf)
    m_scr[mi] = m_new


def _diff_prompt_kernel(qi_ref, kj_ref, flag_ref, q_ref, k_ref, v_ref, kg_ref, base_ref,
                        lq1_ref, lk1_ref, lq2_ref, lk2_ref, o_ref, qx_scr, m_scr, l_scr, acc_scr,
                        *, B, n_heads, lam_init):
    n = pl.program_id(0)
    p = pl.program_id(1)
    i = qi_ref[p]
    j = kj_ref[p]
    fast = flag_ref[0] != 0
    lane = lax.broadcasted_iota(jnp.int32, (B, LANES), 1)

    def head_slope(g):
        return jnp.where(n == 0, _alibi_slope(n_heads, g), _alibi_slope(n_heads, G_B + g)) * LOG2E

    def q_map(g, c):
        qh = q_ref[:, g * LANES:(g + 1) * LANES]
        return jnp.where((lane < D_B) if c == 0 else (lane >= D_B), qh, 0.0)

    def causal_keep():
        return lax.broadcasted_iota(jnp.int32, (B, B), 1) <= lax.broadcasted_iota(jnp.int32, (B, B), 0)

    def finish(l_of):
        lam = _lam_value(lq1_ref[...], lk1_ref[...], lq2_ref[...], lk2_ref[...], lam_init)
        for g in range(G_B):
            o_ref[:, g * LANES:(g + 1) * LANES] = (acc_scr[2 * g] / l_of(2 * g)
                                                   - lam * (acc_scr[2 * g + 1] / l_of(2 * g + 1)))

    @pl.when(jnp.logical_and(fast, j == 0))
    def _():
        kmax = math.sqrt(D_B) * jnp.max(jnp.abs(kg_ref[...]), axis=-1, keepdims=True)
        rloc = lax.broadcasted_iota(jnp.int32, (B, 1), 0).astype(F32)
        lane1 = lax.broadcasted_iota(jnp.int32, (1, LANES), 1)
        for g in range(G_B):
            slope = head_slope(g)
            c_n0 = _split3_const(_alibi_slope(n_heads, g) * LOG2E)
            c_n1 = _split3_const(_alibi_slope(n_heads, G_B + g) * LOG2E)
            crow = jnp.zeros((1, LANES), F32)
            for t in range(3):
                st = jnp.where(n == 0, c_n0[t], c_n1[t])
                crow = jnp.where(lane1 == t, st * LANES, crow)
                crow = jnp.where(lane1 == 3 + t, st, crow)
            for c in range(2):
                qm = q_map(g, c)
                qn = jnp.sqrt(jnp.sum(qm * qm, axis=1, keepdims=True))
                m1, m2, m3 = _split3(slope * rloc + qn * kmax)
                feat = jnp.where(lane == 6, -m1.astype(F32),
                                 jnp.where(lane == 7, -m2.astype(F32), jnp.where(lane == 8, -m3.astype(F32), crow)))
                qx_scr[2 * g + c] = jnp.concatenate([qm.astype(BF16), feat.astype(BF16)], axis=1)
        l_scr[...] = jnp.zeros_like(l_scr)
        acc_scr[...] = jnp.zeros_like(acc_scr)

    def tile_fast(masked):
        a = ((i - j) * (B // LANES)).astype(F32)
        ind = (lax.broadcasted_iota(jnp.int32, (1, LANES), 1) < 3).astype(F32)
        featk = (base_ref[...] - a * ind).astype(BF16)
        kx = jnp.concatenate([k_ref[...].astype(BF16), featk], axis=1)
        v_bf = v_ref[...].astype(BF16)
        if masked:
            keep = causal_keep()
        for mi in range(2 * G_B):
            s = _dot_nt(qx_scr[mi], kx)
            if masked:
                s = jnp.where(keep, s, NEG_BIG)
            pr = jnp.exp2(s)
            l_scr[mi] = l_scr[mi] + _lane_tile_sum(pr)
            acc_scr[mi] = acc_scr[mi] + _dot(pr.astype(BF16), v_bf)

    @pl.when(jnp.logical_and(fast, j < i))
    def _():
        tile_fast(False)

    @pl.when(jnp.logical_and(fast, j == i))
    def _():
        tile_fast(True)
        finish(lambda mi: jnp.sum(l_scr[mi], axis=1, keepdims=True))

    slow = jnp.logical_not(fast)

    @pl.when(jnp.logical_and(slow, j == 0))
    def _():
        m_scr[...] = jnp.full_like(m_scr, NEG_BIG)
        l_scr[...] = jnp.zeros_like(l_scr)
        acc_scr[...] = jnp.zeros_like(acc_scr)

    def tile_slow(masked):
        k_bf = k_ref[...].astype(BF16)
        v_bf = v_ref[...].astype(BF16)
        col = lax.broadcasted_iota(jnp.int32, (1, B), 1).astype(F32)
        rel = col - ((i - j) * B).astype(F32)
        if masked:
            keep = causal_keep()
        for g in range(G_B):
            bias = head_slope(g) * rel
            for c in range(2):
                s = _dot_nt(q_map(g, c).astype(BF16), k_bf) + bias
                if masked:
                    s = jnp.where(keep, s, NEG_BIG)
                _online_update(g * 2 + c, s, v_bf, m_scr, l_scr, acc_scr)

    @pl.when(jnp.logical_and(slow, j < i))
    def _():
        tile_slow(False)

    @pl.when(jnp.logical_and(slow, j == i))
    def _():
        tile_slow(True)
        finish(lambda mi: l_scr[mi])


def _position_features(B):
    c = np.arange(B)
    f = np.zeros((B, LANES), np.float32)
    f[:, 0:3] = (c >> 7)[:, None]
    f[:, 3:6] = (c & 127)[:, None]
    f[:, 6:9] = 1.0
    return jnp.asarray(f)


def _diff_prompt(qb, kb, vb, k_gain, fast_flag, lq1, lk1, lq2, lk2, lam_init, B):
    T = qb.shape[0]
    n_heads = qb.shape[1] // LANES
    nb = T // B
    assert T % B == 0 and n_heads == KVH_B * G_B and B % LANES == 0 and (nb * B) // LANES <= 256
    qi = np.concatenate([np.full(i + 1, i, np.int32) for i in range(nb)])
    kj = np.concatenate([np.arange(i + 1, dtype=np.int32) for i in range(nb)])
    base = _position_features(B)
    const = lambda shape: pl.BlockSpec(shape, lambda n, p, qi, kj, fl: (0,) * len(shape))
    grid_spec = pltpu.PrefetchScalarGridSpec(
        num_scalar_prefetch=3,
        grid=(KVH_B, len(qi)),
        in_specs=[pl.BlockSpec((B, G_B * LANES), lambda n, p, qi, kj, fl: (qi[p], n)),
                  pl.BlockSpec((B, LANES), lambda n, p, qi, kj, fl: (kj[p], n)),
                  pl.BlockSpec((B, LANES), lambda n, p, qi, kj, fl: (kj[p], n)),
                  const(k_gain.shape), const(base.shape),
                  const(lq1.shape), const(lk1.shape), const(lq2.shape), const(lk2.shape)],
        out_specs=pl.BlockSpec((B, G_B * LANES), lambda n, p, qi, kj, fl: (qi[p], n)),
        scratch_shapes=[pltpu.VMEM((2 * G_B, B, 2 * LANES), BF16)] + [pltpu.VMEM((2 * G_B, B, LANES), F32)] * 3,
    )
    return pl.pallas_call(
        functools.partial(_diff_prompt_kernel, B=B, n_heads=n_heads, lam_init=lam_init),
        out_shape=jax.ShapeDtypeStruct(qb.shape, F32),
        grid_spec=grid_spec,
        compiler_params=_cparams(("parallel", "arbitrary")),
        name="diff_prompt",
    )(jnp.asarray(qi), jnp.asarray(kj), fast_flag, qb, kb, vb, k_gain, base, lq1, lk1, lq2, lk2)


def _diff_sample_kernel(pt_ref, flag_ref, q_ref, kn_ref, vn_ref, kg_ref, lq1_ref, lk1_ref, lq2_ref, lk2_ref,
                        ck_ref, cv_ref, o_ref, kbuf, vbuf, ksem, vsem, m_scr, l_scr, acc_scr,
                        *, P, page, n_chunks, n_seq, T, n_heads, lam_init):
    b = pl.program_id(0)
    c = pl.program_id(1)
    step = b * n_chunks + c
    slot = lax.rem(step, 2)
    past = n_chunks * P * page
    R = G_B * 2 * T
    fast = flag_ref[0] != 0

    def k_copy(pg, sl, p):
        return pltpu.make_async_copy(ck_ref.at[pg], kbuf.at[sl, p], ksem.at[sl])

    def v_copy(pg, sl, p):
        return pltpu.make_async_copy(cv_ref.at[pg], vbuf.at[sl, p], vsem.at[sl])

    def issue(bb, cc, sl):
        for p in range(P):
            pg = pt_ref[bb, cc * P + p]
            k_copy(pg, sl, p).start()
            v_copy(pg, sl, p).start()

    @pl.when(step == 0)
    def _():
        issue(0, 0, 0)

    nxt = step + 1

    @pl.when(nxt < n_seq * n_chunks)
    def _():
        issue(nxt // n_chunks, lax.rem(nxt, n_chunks), 1 - slot)

    for p in range(P):
        k_copy(0, slot, p).wait()
        v_copy(0, slot, p).wait()

    lane = lax.broadcasted_iota(jnp.int32, (T, LANES), 1)
    rrow = lax.broadcasted_iota(jnp.int32, (R, 1), 0)
    trow1 = lax.rem(rrow, T).astype(F32)
    col = lax.broadcasted_iota(jnp.int32, (1, P * page), 1)
    rel = (col + (c * (P * page) - past)).astype(F32)

    def lhs_and_slopes(n):
        blocks = []
        for g in range(G_B):
            qh = q_ref[:, (n * G_B + g) * LANES:(n * G_B + g + 1) * LANES]
            for cs in range(2):
                half = (lane < D_B) if cs == 0 else (lane >= D_B)
                blocks.append(jnp.where(half, qh, 0.0))
        lhs = jnp.concatenate(blocks, axis=0)
        slopes = jnp.where(rrow < 2 * T, _alibi_slope(n_heads, n * G_B), _alibi_slope(n_heads, n * G_B + 1)) * LOG2E
        return lhs, slopes

    def cached_kv(n):
        kt = jnp.concatenate([kbuf[slot, p, n * LANES:(n + 1) * LANES, :] for p in range(P)], axis=1).astype(BF16)
        vv = jnp.concatenate([vbuf[slot, p, pl.ds(n, page, stride=KVH_B), :] for p in range(P)], axis=0).astype(BF16)
        return kt, vv

    def new_kv(n):
        zpad = jnp.zeros((LANES - T, LANES), F32)
        kn = jnp.concatenate([kn_ref[:, n * LANES:(n + 1) * LANES], zpad], axis=0).astype(BF16)
        vn = jnp.concatenate([vn_ref[:, n * LANES:(n + 1) * LANES], zpad], axis=0).astype(BF16)
        ncol = lax.broadcasted_iota(jnp.int32, (R, LANES), 1)
        keep = ncol <= lax.rem(lax.broadcasted_iota(jnp.int32, (R, LANES), 0), T)
        return kn, vn, ncol.astype(F32), keep

    def write_out(n, res, lam):
        for g in range(G_B):
            r0 = g * 2 * T
            o_ref[:, (n * G_B + g) * LANES:(n * G_B + g + 1) * LANES] = res[r0:r0 + T] - lam * res[r0 + T:r0 + 2 * T]

    def run(use_bound):
        @pl.when(c == 0)
        def _():
            if not use_bound:
                m_scr[...] = jnp.full_like(m_scr, NEG_BIG)
            l_scr[...] = jnp.zeros_like(l_scr)
            acc_scr[...] = jnp.zeros_like(acc_scr)

        kmax = math.sqrt(D_B) * jnp.max(jnp.abs(kg_ref[...]), axis=-1, keepdims=True)

        def row_shift(lhs, slopes):
            return slopes * trow1 + jnp.sqrt(jnp.sum(lhs * lhs, axis=1, keepdims=True)) * kmax

        def accumulate(n, s, vv):
            if use_bound:
                pr = jnp.exp2(s)
                l_scr[n] = l_scr[n] + _lane_tile_sum(pr)
                acc_scr[n] = acc_scr[n] + _dot(pr.astype(BF16), vv)
            else:
                _online_update(n, s, vv, m_scr, l_scr, acc_scr)

        for n in range(KVH_B):
            lhs, slopes = lhs_and_slopes(n)
            kt, vv = cached_kv(n)
            s = _dot(lhs.astype(BF16), kt) + slopes * rel
            if use_bound:
                s = s - row_shift(lhs, slopes)
            accumulate(n, s, vv)

        @pl.when(c == n_chunks - 1)
        def _():
            lam = _lam_value(lq1_ref[...], lk1_ref[...], lq2_ref[...], lk2_ref[...], lam_init)
            for n in range(KVH_B):
                lhs, slopes = lhs_and_slopes(n)
                kn, vn, ncol, keep = new_kv(n)
                s = _dot_nt(lhs.astype(BF16), kn) + slopes * ncol
                if use_bound:
                    s = s - row_shift(lhs, slopes)
                accumulate(n, jnp.where(keep, s, NEG_BIG), vn)
                den = jnp.sum(l_scr[n], axis=1, keepdims=True) if use_bound else l_scr[n]
                write_out(n, acc_scr[n] / den, lam)

    @pl.when(fast)
    def _():
        run(True)

    @pl.when(jnp.logical_not(fast))
    def _():
        run(False)


def _diff_sample(qb, kb_new, vb_new, page_table, kt_pages, v_pages, k_gain, fast_flag, lq1, lk1, lq2, lk2, lam_init,
                 T, P):
    n_seq, n_pages = page_table.shape
    page = kt_pages.shape[2]
    n_heads = qb.shape[1] // LANES
    assert n_pages % P == 0 and qb.shape[0] == n_seq * T and page == LANES
    n_chunks = n_pages // P
    R = G_B * 2 * T
    small = pl.BlockSpec(lq1.shape, lambda b, c, pt, fl: (0, 0))
    rows = lambda w: pl.BlockSpec((T, w), lambda b, c, pt, fl: (b, 0))
    grid_spec = pltpu.PrefetchScalarGridSpec(
        num_scalar_prefetch=2,
        grid=(n_seq, n_chunks),
        in_specs=[rows(qb.shape[1]), rows(kb_new.shape[1]), rows(vb_new.shape[1]), small, small, small, small, small,
                  pl.BlockSpec(memory_space=pl.ANY), pl.BlockSpec(memory_space=pl.ANY)],
        out_specs=rows(qb.shape[1]),
        scratch_shapes=[pltpu.VMEM((2, P) + kt_pages.shape[1:], F32),
                        pltpu.VMEM((2, P) + v_pages.shape[1:], F32),
                        pltpu.SemaphoreType.DMA((2,)), pltpu.SemaphoreType.DMA((2,)),
                        pltpu.VMEM((KVH_B, R, LANES), F32), pltpu.VMEM((KVH_B, R, LANES), F32),
                        pltpu.VMEM((KVH_B, R, LANES), F32)],
    )
    return pl.pallas_call(
        functools.partial(_diff_sample_kernel, P=P, page=page, n_chunks=n_chunks, n_seq=n_seq, T=T,
                          n_heads=n_heads, lam_init=lam_init),
        out_shape=jax.ShapeDtypeStruct(qb.shape, F32),
        grid_spec=grid_spec,
        compiler_params=_cparams(("arbitrary", "arbitrary")),
        name="diff_sample",
    )(page_table, fast_flag, qb, kb_new, vb_new, k_gain, lq1, lk1, lq2, lk2, kt_pages, v_pages)


def _outproj_a_kernel(x_ref, oa_ref, gsa_ref, ob_ref, gsb_ref, og_ref, sg_ref, w_ref, y_ref, *, lam_init):
    def branch(o_ref, gate_ref, gain, scale):
        parts = []
        for h in range(o_ref.shape[1] // LANES):
            sl = slice(h * LANES, (h + 1) * LANES)
            parts.append(_rms_rows(o_ref[:, sl], gain) * scale * gate_ref[:, sl])
        return jnp.concatenate(parts, axis=1).astype(BF16)

    ya = branch(oa_ref, gsa_ref, og_ref[...], 1.0)
    yb = branch(ob_ref, gsb_ref, sg_ref[...], 1.0 - lam_init)
    wa = oa_ref.shape[1]
    y_ref[...] = x_ref[...] + _dot(ya, w_ref[0:wa, :]) + _dot(yb, w_ref[wa:, :])


def _outproj_a(x, oa, gsa, ob, gsb, o_gain, subln, w_bf16, lam_init, tm):
    T, D = x.shape
    row = lambda w: pl.BlockSpec((tm, w), lambda i: (i, 0))
    return pl.pallas_call(
        functools.partial(_outproj_a_kernel, lam_init=lam_init),
        out_shape=jax.ShapeDtypeStruct((T, D), F32),
        grid=(T // tm,),
        in_specs=[row(D), row(oa.shape[1]), row(oa.shape[1]), row(ob.shape[1]), row(ob.shape[1]),
                  _full(o_gain.shape), _full(subln.shape), _full(w_bf16.shape)],
        out_specs=row(D),
        compiler_params=_cparams(("parallel",)),
        name="outproj_a",
    )(x, oa, gsa, ob, gsb, o_gain, subln, w_bf16)


def _inproj_c_kernel(x_ref, g_ref, w_ref, qg_ref, kg_ref, bd_ref, q_ref, k_ref, v_ref, gs_ref, *, wq, wkv):
    hb = _rms_rows(x_ref[...], g_ref[...]).astype(BF16)
    bd = bd_ref[...]
    q_ref[...] = _group_rms(_dot(hb, w_ref[:, 0:wq]), qg_ref[...], bd) * (HD_C ** -0.5 * LOG2E)
    k_ref[...] = _group_rms(_dot(hb, w_ref[:, wq:wq + wkv]), kg_ref[...], bd)
    v_ref[...] = _dot(hb, w_ref[:, wq + wkv:wq + 2 * wkv])
    gs_ref[...] = _silu(_dot(hb, w_ref[:, wq + 2 * wkv:]))


def _inproj_c(x, g, w_bf16, q_gain_t, k_gain_t, bd, tm):
    T, D = x.shape
    wq, wkv = q_gain_t.shape[1], k_gain_t.shape[1]
    widths = (wq, wkv, wkv, wq)
    row = lambda w: pl.BlockSpec((tm, w), lambda i: (i, 0))
    return pl.pallas_call(
        functools.partial(_inproj_c_kernel, wq=wq, wkv=wkv),
        out_shape=[jax.ShapeDtypeStruct((T, w), F32) for w in widths],
        grid=(T // tm,),
        in_specs=[row(D), _full(g.shape), _full(w_bf16.shape), _full(q_gain_t.shape), _full(k_gain_t.shape),
                  _full(bd.shape)],
        out_specs=[row(w) for w in widths],
        compiler_params=_cparams(("parallel",)),
        name="inproj_c",
    )(x, g, w_bf16, q_gain_t, k_gain_t, bd)


def _swa_softmax_pv(s, sink, v_list):
    m = sink
    for piece in s:
        m = jnp.maximum(m, jnp.max(piece, axis=1, keepdims=True))
    den = jnp.exp2(sink - m)
    out = None
    for piece, pv in zip(s, v_list):
        p = jnp.exp2(piece - m)
        den = den + jnp.sum(p, axis=1, keepdims=True)
        term = pv(p.astype(BF16))
        out = term if out is None else out + term
    return out / den


def _swa_prompt_kernel(q_ref, kc_ref, kp_ref, vc_ref, vp_ref, snk_ref, o_ref, *, n_heads):
    i = pl.program_id(0)
    W = q_ref.shape[0]
    n_pairs = n_heads // KVH_C
    lane = lax.broadcasted_iota(jnp.int32, (W, LANES), 1)
    blocks, slopes, sinks = [], [], []
    for n in range(KVH_C):
        half = (lane < HD_C) if n == 0 else (lane >= HD_C)
        for m in range(n_pairs):
            h = n * n_pairs + m
            blocks.append(jnp.where(half, q_ref[:, m * LANES:(m + 1) * LANES], 0.0))
            slopes.append(jnp.full((W, 1), _alibi_slope(n_heads, h) * LOG2E, F32))
            sinks.append(jnp.broadcast_to(snk_ref[:, h:h + 1] * LOG2E, (W, 1)))
    lhs = jnp.concatenate(blocks, axis=0).astype(BF16)
    slope = jnp.concatenate(slopes, axis=0)
    sink = jnp.concatenate(sinks, axis=0)
    keys = jnp.concatenate([kp_ref[...], kc_ref[...]], axis=0).astype(BF16)
    vals = jnp.concatenate([vp_ref[...], vc_ref[...]], axis=0).astype(BF16)
    r = lax.broadcasted_iota(jnp.int32, (W, 2 * W), 0)
    c = lax.broadcasted_iota(jnp.int32, (W, 2 * W), 1)
    dist = r + W - c
    valid = jnp.logical_and(jnp.logical_and(dist >= 0, dist <= W), jnp.logical_or(c >= W, i > 0))
    neg_dist = jnp.tile(jnp.where(valid, -dist.astype(F32), NEG_BIG), (n_heads, 1))
    s = _dot_nt(lhs, keys) + slope * neg_dist
    m_row = jnp.maximum(jnp.max(s, axis=1, keepdims=True), sink)
    pr = jnp.exp2(s - m_row)
    den = jnp.sum(pr, axis=1, keepdims=True) + jnp.exp2(sink - m_row)
    res = _dot(pr.astype(BF16), vals) / den
    half_rows = n_pairs * W
    for m in range(n_pairs):
        lo = res[m * W:(m + 1) * W]
        hi = res[half_rows + m * W:half_rows + (m + 1) * W]
        o_ref[:, m * LANES:(m + 1) * LANES] = jnp.where(lane < HD_C, lo, hi)


def _swa_prompt(q, k, v, sinks):
    T, Wq = q.shape
    n_heads = Wq // HD_C
    nb = T // WINDOW
    cur = lambda w: pl.BlockSpec((WINDOW, w), lambda i: (i, 0))
    prev = lambda w: pl.BlockSpec((WINDOW, w), lambda i: (jnp.maximum(i - 1, 0), 0))
    kw = k.shape[1]
    return pl.pallas_call(
        functools.partial(_swa_prompt_kernel, n_heads=n_heads),
        out_shape=jax.ShapeDtypeStruct(q.shape, F32),
        grid=(nb,),
        in_specs=[cur(Wq), cur(kw), prev(kw), cur(kw), prev(kw), _full(sinks.shape)],
        out_specs=cur(Wq),
        compiler_params=_cparams(("parallel",)),
        name="swa_prompt",
    )(q, k, k, v, v, sinks)


def _swa_sample_kernel(q_ref, kn_ref, vn_ref, ckt_ref, cvt_ref, snk_ref, o_ref, *, T, n_heads, n_sub):
    n_pairs = n_heads // KVH_C
    R = n_heads * T
    lane = lax.broadcasted_iota(jnp.int32, (T, LANES), 1)
    rr = lax.broadcasted_iota(jnp.int32, (R, WINDOW), 0)
    cc = lax.broadcasted_iota(jnp.int32, (R, WINDOW), 1)
    t = lax.rem(rr, T)
    hrow = lax.broadcasted_iota(jnp.int32, (R, 1), 0) // T
    slope = jnp.exp2(-8.0 * (hrow + 1).astype(F32) / n_heads) * LOG2E
    sink = jnp.zeros((R, 1), F32)
    for h in range(n_heads):
        sink = jnp.where(hrow == h, snk_ref[:, h:h + 1], sink)
    sink = sink * LOG2E
    bias_c = jnp.where(cc >= t, -slope * (t + WINDOW - cc).astype(F32), NEG_BIG)
    bias_n = jnp.where(cc <= t, -slope * (t - cc).astype(F32), NEG_BIG)
    zpad = jnp.zeros((WINDOW - T, LANES), F32)
    half_rows = n_pairs * T
    for sq in range(n_sub):
        rows = slice(sq * T, (sq + 1) * T)
        blocks = []
        for n in range(KVH_C):
            half = (lane < HD_C) if n == 0 else (lane >= HD_C)
            for m in range(n_pairs):
                blocks.append(jnp.where(half, q_ref[rows, m * LANES:(m + 1) * LANES], 0.0))
        lhs = jnp.concatenate(blocks, axis=0).astype(BF16)
        kn = jnp.concatenate([kn_ref[rows, :], zpad], axis=0).astype(BF16)
        vn = jnp.concatenate([vn_ref[rows, :], zpad], axis=0).astype(BF16)
        ckt = ckt_ref[sq].astype(BF16)
        cvt = cvt_ref[sq].astype(BF16)
        s_c = _dot(lhs, ckt) + bias_c
        s_n = _dot_nt(lhs, kn) + bias_n
        res = _swa_softmax_pv([s_c, s_n], sink, [lambda p: _dot_nt(p, cvt), lambda p: _dot(p, vn)])
        for m in range(n_pairs):
            lo = res[m * T:(m + 1) * T]
            hi = res[half_rows + m * T:half_rows + (m + 1) * T]
            o_ref[rows, m * LANES:(m + 1) * LANES] = jnp.where(lane < HD_C, lo, hi)


def _swa_sample(q, k_new, v_new, ckt, cvt, sinks, T, n_sub):
    BT, Wq = q.shape
    B = ckt.shape[0]
    n_heads = Wq // HD_C
    assert B % n_sub == 0
    rows = lambda w: pl.BlockSpec((n_sub * T, w), lambda b: (b, 0))
    cache = pl.BlockSpec((n_sub,) + ckt.shape[1:], lambda b: (b, 0, 0))
    return pl.pallas_call(
        functools.partial(_swa_sample_kernel, T=T, n_heads=n_heads, n_sub=n_sub),
        out_shape=jax.ShapeDtypeStruct(q.shape, F32),
        grid=(B // n_sub,),
        in_specs=[rows(Wq), rows(k_new.shape[1]), rows(v_new.shape[1]), cache, cache, _full(sinks.shape)],
        out_specs=rows(Wq),
        compiler_params=_cparams(("parallel",)),
        name="swa_sample",
    )(q, k_new, v_new, ckt, cvt, sinks)


def _outproj_c_kernel(x_ref, o_ref, gs_ref, w_ref, y_ref):
    y_ref[...] = x_ref[...] + _dot((o_ref[...] * gs_ref[...]).astype(BF16), w_ref[...])


def _outproj_c(x, o, gs, w_bf16, tm):
    T, D = x.shape
    row = lambda w: pl.BlockSpec((tm, w), lambda i: (i, 0))
    return pl.pallas_call(
        _outproj_c_kernel,
        out_shape=jax.ShapeDtypeStruct((T, D), F32),
        grid=(T // tm,),
        in_specs=[row(D), row(o.shape[1]), row(o.shape[1]), _full(w_bf16.shape)],
        out_specs=row(D),
        compiler_params=_cparams(("parallel",)),
        name="outproj_c",
    )(x, o, gs, w_bf16)


TOKEN_TILE = 256
GLA_CHUNK = 128
DIFF_BLOCK = 512
DIFF_PAGES_PER_STEP = 32
SWA_SEQS_PER_STEP = 8


def _pair_perm(n_heads):
    n_pairs = n_heads // KVH_C
    cols = []
    for m in range(n_pairs):
        for n in range(KVH_C):
            h = n * n_pairs + m
            cols.extend(range(h * HD_C, (h + 1) * HD_C))
    return np.asarray(cols, np.int32)


def kernel(x_prompt, x_sample, state_hgrn, cache_k_diff, cache_v_diff, cache_k_swa, cache_v_swa, page_table, norm_a, w_in_a, w_out_a, lb_logits, hgrn_out_gain, diff_q_gain, diff_k_gain, diff_subln_gain, lam_q1, lam_k1, lam_q2, lam_k2, norm_c, w_in_c, w_out_c, swa_q_gain, swa_k_gain, sinks):
    batch, seq, d_model = x_prompt.shape
    n_seq, t_dec, _ = x_sample.shape
    assert batch == 1
    n_even, n_odd = norm_a.shape[0], norm_c.shape[0]
    depth = n_even + n_odd
    h_a = state_hgrn.shape[2]
    n_pool, page = cache_k_diff.shape[1], cache_k_diff.shape[2]
    w_b = w_out_a.shape[1] - h_a * DV_A
    h_b = w_b // (2 * D_B)
    h_c = sinks.shape[1]

    xp = x_prompt.reshape(seq, d_model)
    xs = x_sample.reshape(n_seq * t_dec, d_model)
    bd64 = _block_diag_mean(2 * LANES, D_B)
    bd64_c = _block_diag_mean(LANES, HD_C)
    perm = _pair_perm(h_c)

    hs_p, hs_s, kdp, vdp, kds, vds, ksp, vsp, kss, vss = ([] for _ in range(10))
    for l in range(depth):
        if l % 2 == 0:
            e = l // 2
            lam_init = 0.8 - 0.6 * math.exp(-0.3 * l)
            w_in = w_in_a[e].astype(BF16)
            w_out = w_out_a[e].astype(BF16)
            g = norm_a[e:e + 1]
            qg = jnp.tile(diff_q_gain[e:e + 1], (1, h_b * 2))
            kg = jnp.tile(diff_k_gain[e:e + 1], (1, KVH_B * 2))
            lam_args = (lam_q1[e:e + 1], lam_k1[e:e + 1], lam_q2[e:e + 1], lam_k2[e:e + 1])
            kt_pages = jnp.transpose(cache_k_diff[e], (0, 2, 3, 4, 1)).reshape(n_pool, KVH_B * 2 * D_B, page)
            v_pages = cache_v_diff[e].reshape(n_pool, page * KVH_B, 2 * D_B)

            pa = _inproj_a(xp, g, w_in, lb_logits, qg, kg, bd64, e, TOKEN_TILE)
            sa = _inproj_a(xs, g, w_in, lb_logits, qg, kg, bd64, e, TOKEN_TILE)
            qa_p, lf_p, kk_p, ia_p, gsa_p, qb_p, kb_p, vb_p, gsb_p = pa
            qa_s, lf_s, kk_s, ia_s, gsa_s, qb_s, kb_s, vb_s, gsb_s = sa

            oa_p, st_p = _gla_prompt(qa_p, kk_p, ia_p, lf_p, GLA_CHUNK)
            oa_s, st_s = _gla_sample(qa_s, kk_s, ia_s, lf_s, state_hgrn[e], t_dec)
            logit_bound = (LOG2E * math.sqrt(D_B)) * jnp.max(jnp.abs(diff_q_gain[e])) * jnp.max(jnp.abs(diff_k_gain[e]))
            fast_flag = (logit_bound <= FAST_LOGIT_BOUND).astype(jnp.int32).reshape(1)
            k_gain = diff_k_gain[e:e + 1]
            ob_p = _diff_prompt(qb_p, kb_p, vb_p, k_gain, fast_flag, *lam_args, lam_init, DIFF_BLOCK)
            ob_s = _diff_sample(qb_s, kb_s, vb_s, page_table, kt_pages, v_pages, k_gain, fast_flag, *lam_args,
                                lam_init, t_dec, DIFF_PAGES_PER_STEP)

            og, sg = hgrn_out_gain[e:e + 1], diff_subln_gain[e:e + 1]
            xp = _outproj_a(xp, oa_p, gsa_p, ob_p, gsb_p, og, sg, w_out, lam_init, TOKEN_TILE)
            xs = _outproj_a(xs, oa_s, gsa_s, ob_s, gsb_s, og, sg, w_out, lam_init, TOKEN_TILE)

            hs_p.append(st_p[None])
            hs_s.append(st_s)
            kdp.append(kb_p.reshape(batch, seq, KVH_B, 2, D_B))
            vdp.append(vb_p.reshape(batch, seq, KVH_B, 2 * D_B))
            kds.append(kb_s.reshape(n_seq, t_dec, KVH_B, 2, D_B))
            vds.append(vb_s.reshape(n_seq, t_dec, KVH_B, 2 * D_B))
        else:
            o = l // 2
            wq = h_c * HD_C
            wkv = KVH_C * HD_C
            w_full = w_in_c[o]
            w_in = jnp.concatenate([w_full[:, :wq][:, perm], w_full[:, wq:wq + 2 * wkv],
                                    w_full[:, wq + 2 * wkv:][:, perm]], axis=1).astype(BF16)
            w_out = w_out_c[o][perm, :].astype(BF16)
            g = norm_c[o:o + 1]
            qg = jnp.tile(swa_q_gain[o:o + 1], (1, h_c))
            kg = jnp.tile(swa_k_gain[o:o + 1], (1, KVH_C))
            snk = sinks[o:o + 1]
            ckt = jnp.transpose(cache_k_swa[o], (0, 2, 3, 1)).reshape(n_seq, wkv, WINDOW)
            cvt = jnp.transpose(cache_v_swa[o], (0, 2, 3, 1)).reshape(n_seq, wkv, WINDOW)

            q_p, k_p, v_p, gs_p = _inproj_c(xp, g, w_in, qg, kg, bd64_c, TOKEN_TILE)
            q_s, k_s, v_s, gs_s = _inproj_c(xs, g, w_in, qg, kg, bd64_c, TOKEN_TILE)
            o_p = _swa_prompt(q_p, k_p, v_p, snk)
            o_s = _swa_sample(q_s, k_s, v_s, ckt, cvt, snk, t_dec, SWA_SEQS_PER_STEP)
            xp = _outproj_c(xp, o_p, gs_p, w_out, TOKEN_TILE)
            xs = _outproj_c(xs, o_s, gs_s, w_out, TOKEN_TILE)

            ksp.append(k_p[-WINDOW:].reshape(batch, WINDOW, KVH_C, HD_C))
            vsp.append(v_p[-WINDOW:].reshape(batch, WINDOW, KVH_C, HD_C))
            k_new = k_s.reshape(n_seq, t_dec, KVH_C, HD_C)
            v_new = v_s.reshape(n_seq, t_dec, KVH_C, HD_C)
            kss.append(jnp.concatenate([cache_k_swa[o], k_new], axis=1)[:, -WINDOW:])
            vss.append(jnp.concatenate([cache_v_swa[o], v_new], axis=1)[:, -WINDOW:])
    return (xp.reshape(batch, seq, d_model), xs.reshape(n_seq, t_dec, d_model),
            jnp.stack(hs_p), jnp.stack(hs_s), jnp.stack(kdp), jnp.stack(vdp), jnp.stack(kds), jnp.stack(vds),
            jnp.stack(ksp), jnp.stack(vsp), jnp.stack(kss), jnp.stack(vss))
```

```python
import functools
import math

import jax
import jax.numpy as jnp
import ml_dtypes
import numpy as np
from jax import lax
from jax.experimental import pallas as pl
from jax.experimental.pallas import tpu as pltpu

F32 = jnp.float32
BF16 = jnp.bfloat16

EPS = 1e-6
LOG2E = 1.4426950408889634
NEG_BIG = -1e30
LANES = 128
VMEM_LIMIT_BYTES = 56 * 1024 * 1024

DK_A = 128
DV_A = 128
D_B = 64
KVH_B = 2
G_B = 2
HD_C = 64
KVH_C = 2
WINDOW = 128
GLA_SUB = 16
FAST_LOGIT_BOUND = 60.0
GLA_FAST_DECAY = 80.0


def _dot(a, b):
    return jnp.dot(a, b, preferred_element_type=F32)


def _dot_nt(a, b):
    return lax.dot_general(a, b, (((1,), (1,)), ((), ())), preferred_element_type=F32)


def _dot_tn(a, b):
    return lax.dot_general(a, b, (((0,), (0,)), ((), ())), preferred_element_type=F32)


def _silu(x):
    return x * jax.nn.sigmoid(x)


def _rms_rows(x, gain):
    return x * lax.rsqrt(jnp.mean(x * x, axis=-1, keepdims=True) + EPS) * gain


def _split3(x):
    x1 = x.astype(BF16)
    r1 = x - x1.astype(F32)
    x2 = r1.astype(BF16)
    x3 = (r1 - x2.astype(F32)).astype(BF16)
    return x1, x2, x3


def _group_meansq(x, bd):
    s = x * x
    hi = s.astype(BF16)
    lo = (s - hi.astype(F32)).astype(BF16)
    return _dot(hi, bd) + _dot(lo, bd)


def _group_rms(x, gain, bd):
    wb = bd.shape[0]
    outs = []
    for j in range(x.shape[1] // wb):
        part = x[:, j * wb:(j + 1) * wb]
        ms = _group_meansq(part, bd)
        outs.append(part * lax.rsqrt(ms + EPS) * gain[:, j * wb:(j + 1) * wb])
    return outs[0] if len(outs) == 1 else jnp.concatenate(outs, axis=1)


def _block_diag_mean(width, group):
    r = np.arange(width)[:, None] // group
    c = np.arange(width)[None, :] // group
    return jnp.asarray((r == c).astype(np.float32) / group, dtype=BF16)


def _lam_value(lq1, lk1, lq2, lk2, lam_init):
    a = jnp.sum(lq1 * lk1, axis=-1, keepdims=True)
    b = jnp.sum(lq2 * lk2, axis=-1, keepdims=True)
    return jnp.exp(a) - jnp.exp(b) + lam_init


def _split3_const(x):
    x = np.float32(x)
    a = np.float32(ml_dtypes.bfloat16(x))
    b = np.float32(ml_dtypes.bfloat16(np.float32(x - a)))
    c = np.float32(ml_dtypes.bfloat16(np.float32(np.float32(x - a) - b)))
    return float(a), float(b), float(c)


def _lane_tile_sum(p):
    out = p[:, 0:LANES]
    for t in range(1, p.shape[1] // LANES):
        out = out + p[:, t * LANES:(t + 1) * LANES]
    return out


def _cparams(sem):
    return pltpu.CompilerParams(dimension_semantics=sem, vmem_limit_bytes=VMEM_LIMIT_BYTES)


def _full(shape):
    nd = len(shape)
    return pl.BlockSpec(shape, lambda *_: (0,) * nd)


def _inproj_a_kernel(x_ref, g_ref, w_ref, lbl_ref, qg_ref, kg_ref, bd_ref,
                     qa_ref, lf_ref, kk_ref, ia_ref, gsa_ref, qb_ref, kb_ref, vb_ref, gsb_ref, *, e, wa, wb, wkv):
    hb = _rms_rows(x_ref[...], g_ref[...]).astype(BF16)
    bd = bd_ref[...]
    o = [0]

    def seg(width):
        z = _dot(hb, w_ref[:, o[0]:o[0] + width])
        o[0] += width
        return z

    qa = seg(wa)
    qa_ref[...] = _silu(qa) * (DK_A ** -0.5)
    lbl = lbl_ref[...]
    ex = jnp.exp(lbl - jnp.max(lbl, axis=0, keepdims=True))
    sm = ex / jnp.sum(ex, axis=0, keepdims=True)
    lb = jnp.sum(sm[0:e + 1, :], axis=0, keepdims=True)
    f = lb + (1.0 - lb) * jax.nn.sigmoid(seg(wa))
    lf_ref[...] = jnp.log(f)
    kk_ref[...] = 1.0 - f
    ia_ref[...] = seg(wa)
    gsa_ref[...] = _silu(seg(wa))
    qb_ref[...] = _group_rms(seg(wb), qg_ref[...], bd) * (D_B ** -0.5 * LOG2E)
    kb_ref[...] = _group_rms(seg(wkv), kg_ref[...], bd)
    vb_ref[...] = seg(wkv)
    gsb_ref[...] = _silu(seg(wb))


def _inproj_a(x, g, w_bf16, lb_logits, q_gain_t, k_gain_t, bd, e, tm):
    T, D = x.shape
    wa, wb, wkv = lb_logits.shape[1], q_gain_t.shape[1], k_gain_t.shape[1]
    n_in = w_bf16.shape[1]
    assert n_in == 4 * wa + 2 * wb + 2 * wkv and T % tm == 0
    widths = (wa, wa, wa, wa, wa, wb, wkv, wkv, wb)
    row = lambda w: pl.BlockSpec((tm, w), lambda i: (i, 0))
    return pl.pallas_call(
        functools.partial(_inproj_a_kernel, e=e, wa=wa, wb=wb, wkv=wkv),
        out_shape=[jax.ShapeDtypeStruct((T, w), F32) for w in widths],
        grid=(T // tm,),
        in_specs=[row(D), _full(g.shape), _full(w_bf16.shape), _full(lb_logits.shape),
                  _full(q_gain_t.shape), _full(k_gain_t.shape), _full(bd.shape)],
        out_specs=[row(w) for w in widths],
        compiler_params=_cparams(("parallel",)),
        name="inproj_a",
    )(x, g, w_bf16, lb_logits, q_gain_t, k_gain_t, bd)


def _gla_intra(q, k, b, bx, v_bf, C, c):
    n = C // c
    lane = lax.broadcasted_iota(jnp.int32, (c, C), 1)
    trow = lax.broadcasted_iota(jnp.int32, (c, LANES), 0)
    ones = jnp.ones((LANES, LANES), BF16)
    rows = []
    for i in range(n):
        sl = slice(i * c, (i + 1) * c)
        qi, ki, bi = q[sl], k[sl], b[sl]
        if i > 0:
            b0 = bx[i * c:i * c + 1, :]
            ks = k * jnp.exp(jnp.minimum(b0 - b, 0.0))
            qh = qi * jnp.exp(bi - b0)
            a_i = jnp.where(lane < i * c, _dot_nt(qh.astype(BF16), ks.astype(BF16)), 0.0)
        else:
            a_i = jnp.zeros((c, C), F32)
        pieces = []
        for s in range(c):
            d = jnp.minimum(bi - bi[s:s + 1, :], 0.0)
            pieces.append(jnp.where(trow >= s, qi * ki[s:s + 1, :] * jnp.exp(d), 0.0))
        r = _dot(jnp.concatenate(pieces, axis=0).astype(BF16), ones)
        for s in range(c):
            a_i = jnp.where(lane == i * c + s, r[s * c:(s + 1) * c, :C], a_i)
        rows.append(a_i)
    a = rows[0] if n == 1 else jnp.concatenate(rows, axis=0)
    return _dot(a.astype(BF16), v_bf)


def _gla_intra_bounded(q, k, b, bx, v_bf, C, c):
    lane = lax.broadcasted_iota(jnp.int32, (c, C), 1)
    trow = lax.broadcasted_iota(jnp.int32, (c, C), 0)
    rows = []
    for i in range(C // c):
        sl = slice(i * c, (i + 1) * c)
        b0 = bx[i * c:i * c + 1, :]
        ks = k * jnp.exp(jnp.minimum(b0 - b, GLA_FAST_DECAY))
        qh = q[sl] * jnp.exp(b[sl] - b0)
        rows.append(jnp.where(lane <= trow + i * c, _dot_nt(qh.astype(BF16), ks.astype(BF16)), 0.0))
    return _dot(jnp.concatenate(rows, axis=0).astype(BF16), v_bf)


def _gla_prompt_kernel(flag_ref, q_ref, k_ref, v_ref, lf_ref, o_ref, s_ref, st_scr, *, C, c, hps):
    j = pl.program_id(1)

    @pl.when(j == 0)
    def _():
        st_scr[...] = jnp.zeros_like(st_scr)

    def chunk(intra):
        rr = lax.broadcasted_iota(jnp.int32, (C, C), 0)
        cc = lax.broadcasted_iota(jnp.int32, (C, C), 1)
        tri = (cc <= rr).astype(BF16)
        for h in range(hps):
            sl = slice(h * DK_A, (h + 1) * DK_A)
            q, k, v, lf = q_ref[:, sl], k_ref[:, sl], v_ref[:, sl], lf_ref[:, sl]
            l1, l2, l3 = _split3(lf)
            b = _dot(tri, l1) + _dot(tri, l2) + _dot(tri, l3)
            st = st_scr[h]
            v_bf = v.astype(BF16)
            o = _dot_nt((q * jnp.exp(b)).astype(BF16), st.astype(BF16))
            o_ref[:, sl] = o + intra(q, k, b, b - lf, v_bf, C, c)
            b_end = b[C - 1:C, :]
            kt = k * jnp.exp(b_end - b)
            st_scr[h] = st * jnp.exp(b_end) + _dot_tn(v_bf, kt.astype(BF16))

    @pl.when(flag_ref[0] != 0)
    def _():
        chunk(_gla_intra_bounded)

    @pl.when(flag_ref[0] == 0)
    def _():
        chunk(_gla_intra)

    @pl.when(j == pl.num_programs(1) - 1)
    def _():
        for h in range(hps):
            s_ref[h] = st_scr[h].T


def _gla_prompt(q, k, v, lf, bounded_flag, C, hps):
    T, W = q.shape
    H = W // DK_A
    assert T % C == 0 and C % GLA_SUB == 0 and H % hps == 0
    blk = pl.BlockSpec((C, hps * DK_A), lambda h, j, fl: (j, h))
    grid_spec = pltpu.PrefetchScalarGridSpec(
        num_scalar_prefetch=1,
        grid=(H // hps, T // C),
        in_specs=[blk, blk, blk, blk],
        out_specs=[blk, pl.BlockSpec((hps, DK_A, DV_A), lambda h, j, fl: (h, 0, 0))],
        scratch_shapes=[pltpu.VMEM((hps, DV_A, DK_A), F32)],
    )
    return pl.pallas_call(
        functools.partial(_gla_prompt_kernel, C=C, c=GLA_SUB, hps=hps),
        out_shape=[jax.ShapeDtypeStruct((T, W), F32), jax.ShapeDtypeStruct((H, DK_A, DV_A), F32)],
        grid_spec=grid_spec,
        compiler_params=_cparams(("parallel", "arbitrary")),
        name="gla_prompt",
    )(bounded_flag, q, k, v, lf)


def _gla_sample_kernel(q_ref, k_ref, v_ref, lf_ref, s0_ref, o_ref, s_ref, *, T, H, n_sub):
    pad = LANES - T
    row = lax.broadcasted_iota(jnp.int32, (T, LANES), 0)
    zpad = jnp.zeros((pad, LANES), F32)
    for sq, h in [(sq, h) for sq in range(n_sub) for h in range(H)]:
        sl = slice(h * DK_A, (h + 1) * DK_A)
        rs = slice(sq * T, (sq + 1) * T)
        q, k, v, lf = q_ref[rs, sl], k_ref[rs, sl], v_ref[rs, sl], lf_ref[rs, sl]
        b = jnp.zeros((T, LANES), F32)
        for t in range(T):
            bt = jnp.sum(jnp.where(row <= t, lf, 0.0), axis=0, keepdims=True)
            b = jnp.where(row == t, bt, b)
        st = s0_ref[sq, h].T
        v_bf = jnp.concatenate([v, zpad], axis=0).astype(BF16)
        qd = jnp.concatenate([q * jnp.exp(b), zpad], axis=0).astype(BF16)
        o = _dot_nt(qd, st.astype(BF16))[:T]
        ones = jnp.ones((LANES, LANES), BF16)
        lane = lax.broadcasted_iota(jnp.int32, (T, LANES), 1)
        pieces = []
        for s in range(T):
            d = jnp.minimum(b - b[s:s + 1, :], 0.0)
            pieces.append(jnp.where(row >= s, q * k[s:s + 1, :] * jnp.exp(d), 0.0))
        r = _dot(jnp.concatenate(pieces, axis=0).astype(BF16), ones)
        a = jnp.zeros((T, LANES), F32)
        for s in range(T):
            a = jnp.where(lane == s, r[s * T:(s + 1) * T, :], a)
        a_bf = jnp.concatenate([a, zpad], axis=0).astype(BF16)
        o_ref[rs, sl] = o + _dot(a_bf, v_bf)[:T]
        b_end = b[T - 1:T, :]
        kt = jnp.concatenate([k * jnp.exp(b_end - b), zpad], axis=0).astype(BF16)
        st_new = st * jnp.exp(b_end) + _dot_tn(v_bf, kt)
        s_ref[sq, h] = st_new.T


def _gla_sample(q, k, v, lf, s0, T, n_sub):
    BT, W = q.shape
    B, H = s0.shape[0], s0.shape[1]
    assert BT == B * T and W == H * DK_A and T % 8 == 0 and T <= GLA_SUB and B % n_sub == 0
    blk = pl.BlockSpec((n_sub * T, W), lambda i: (i, 0))
    sblk = pl.BlockSpec((n_sub, H, DK_A, DV_A), lambda i: (i, 0, 0, 0))
    return pl.pallas_call(
        functools.partial(_gla_sample_kernel, T=T, H=H, n_sub=n_sub),
        out_shape=[jax.ShapeDtypeStruct((BT, W), F32), jax.ShapeDtypeStruct(s0.shape, F32)],
        grid=(B // n_sub,),
        in_specs=[blk, blk, blk, blk, sblk],
        out_specs=[blk, sblk],
        compiler_params=_cparams(("parallel",)),
        name="gla_sample",
    )(q, k, v, lf, s0)


def _alibi_slope(n_heads, h):
    return 2.0 ** (-8.0 * (h + 1) / n_heads)


def _online_update(mi, s, v_bf, m_scr, l_scr, acc_scr):
    reps = s.shape[1] // LANES
    m_prev = m_scr[mi]
    m_new = jnp.maximum(m_prev, jnp.max(s, axis=1, keepdims=True))
    alpha = jnp.exp2(m_prev - m_new)
    p = jnp.exp2(s - jnp.tile(m_new, (1, reps)))
    l_scr[mi] = alpha * l_scr[mi] + jnp.sum(p, axis=1, keepdims=True)
    acc_scr[mi] = alpha * acc_scr[mi] + _dot(p.astype(BF16), v_bf)
    m_scr[mi] = m_new


def _diff_prompt_kernel(qi_ref, kj_ref, flag_ref, q_ref, k_ref, v_ref, kg_ref, base_ref,
                        lq1_ref, lk1_ref, lq2_ref, lk2_ref, o_ref, qx_scr, m_scr, l_scr, acc_scr,
                        *, B, n_heads, lam_init):
    n = pl.program_id(0)
    p = pl.program_id(1)
    i = qi_ref[p]
    j = kj_ref[p]
    fast = flag_ref[0] != 0
    lane = lax.broadcasted_iota(jnp.int32, (B, LANES), 1)

    def head_slope(g):
        return jnp.where(n == 0, _alibi_slope(n_heads, g), _alibi_slope(n_heads, G_B + g)) * LOG2E

    def q_map(g, c):
        qh = q_ref[:, g * LANES:(g + 1) * LANES]
        return jnp.where((lane < D_B) if c == 0 else (lane >= D_B), qh, 0.0)

    def causal_keep():
        return lax.broadcasted_iota(jnp.int32, (B, B), 1) <= lax.broadcasted_iota(jnp.int32, (B, B), 0)

    def finish(l_of):
        lam = _lam_value(lq1_ref[...], lk1_ref[...], lq2_ref[...], lk2_ref[...], lam_init)
        for g in range(G_B):
            o_ref[:, g * LANES:(g + 1) * LANES] = (acc_scr[2 * g] / l_of(2 * g)
                                                   - lam * (acc_scr[2 * g + 1] / l_of(2 * g + 1)))

    @pl.when(jnp.logical_and(fast, j == 0))
    def _():
        kmax = math.sqrt(D_B) * jnp.max(jnp.abs(kg_ref[...]), axis=-1, keepdims=True)
        rloc = lax.broadcasted_iota(jnp.int32, (B, 1), 0).astype(F32)
        lane1 = lax.broadcasted_iota(jnp.int32, (1, LANES), 1)
        for g in range(G_B):
            slope = head_slope(g)
            c_n0 = _split3_const(_alibi_slope(n_heads, g) * LOG2E)
            c_n1 = _split3_const(_alibi_slope(n_heads, G_B + g) * LOG2E)
            crow = jnp.zeros((1, LANES), F32)
            for t in range(3):
                st = jnp.where(n == 0, c_n0[t], c_n1[t])
                crow = jnp.where(lane1 == t, st * LANES, crow)
                crow = jnp.where(lane1 == 3 + t, st, crow)
            for c in range(2):
                qm = q_map(g, c)
                qn = jnp.sqrt(jnp.sum(qm * qm, axis=1, keepdims=True))
                m1, m2, m3 = _split3(slope * rloc + qn * kmax)
                feat = jnp.where(lane == 6, -m1.astype(F32),
                                 jnp.where(lane == 7, -m2.astype(F32), jnp.where(lane == 8, -m3.astype(F32), crow)))
                qx_scr[2 * g + c] = jnp.concatenate([qm.astype(BF16), feat.astype(BF16)], axis=1)
        l_scr[...] = jnp.zeros_like(l_scr)
        acc_scr[...] = jnp.zeros_like(acc_scr)

    def tile_fast(masked):
        a = ((i - j) * (B // LANES)).astype(F32)
        ind = (lax.broadcasted_iota(jnp.int32, (1, LANES), 1) < 3).astype(F32)
        featk = (base_ref[...] - a * ind).astype(BF16)
        kx = jnp.concatenate([k_ref[...].astype(BF16), featk], axis=1)
        v_bf = v_ref[...].astype(BF16)
        if masked:
            keep = causal_keep()
        for mi in range(2 * G_B):
            s = _dot_nt(qx_scr[mi], kx)
            if masked:
                s = jnp.where(keep, s, NEG_BIG)
            pr = jnp.exp2(s)
            l_scr[mi] = l_scr[mi] + _lane_tile_sum(pr)
            acc_scr[mi] = acc_scr[mi] + _dot(pr.astype(BF16), v_bf)

    @pl.when(jnp.logical_and(fast, j < i))
    def _():
        tile_fast(False)

    @pl.when(jnp.logical_and(fast, j == i))
    def _():
        tile_fast(True)
        finish(lambda mi: jnp.sum(l_scr[mi], axis=1, keepdims=True))

    slow = jnp.logical_not(fast)

    @pl.when(jnp.logical_and(slow, j == 0))
    def _():
        m_scr[...] = jnp.full_like(m_scr, NEG_BIG)
        l_scr[...] = jnp.zeros_like(l_scr)
        acc_scr[...] = jnp.zeros_like(acc_scr)

    def tile_slow(masked):
        k_bf = k_ref[...].astype(BF16)
        v_bf = v_ref[...].astype(BF16)
        col = lax.broadcasted_iota(jnp.int32, (1, B), 1).astype(F32)
        rel = col - ((i - j) * B).astype(F32)
        if masked:
            keep = causal_keep()
        for g in range(G_B):
            bias = head_slope(g) * rel
            for c in range(2):
                s = _dot_nt(q_map(g, c).astype(BF16), k_bf) + bias
                if masked:
                    s = jnp.where(keep, s, NEG_BIG)
                _online_update(g * 2 + c, s, v_bf, m_scr, l_scr, acc_scr)

    @pl.when(jnp.logical_and(slow, j < i))
    def _():
        tile_slow(False)

    @pl.when(jnp.logical_and(slow, j == i))
    def _():
        tile_slow(True)
        finish(lambda mi: l_scr[mi])


def _position_features(B):
    c = np.arange(B)
    f = np.zeros((B, LANES), np.float32)
    f[:, 0:3] = (c >> 7)[:, None]
    f[:, 3:6] = (c & 127)[:, None]
    f[:, 6:9] = 1.0
    return jnp.asarray(f)


def _diff_prompt(qb, kb, vb, k_gain, fast_flag, lq1, lk1, lq2, lk2, lam_init, B):
    T = qb.shape[0]
    n_heads = qb.shape[1] // LANES
    nb = T // B
    assert T % B == 0 and n_heads == KVH_B * G_B and B % LANES == 0 and (nb * B) // LANES <= 256
    qi = np.concatenate([np.full(i + 1, i, np.int32) for i in range(nb)])
    kj = np.concatenate([np.arange(i + 1, dtype=np.int32) for i in range(nb)])
    base = _position_features(B)
    const = lambda shape: pl.BlockSpec(shape, lambda n, p, qi, kj, fl: (0,) * len(shape))
    grid_spec = pltpu.PrefetchScalarGridSpec(
        num_scalar_prefetch=3,
        grid=(KVH_B, len(qi)),
        in_specs=[pl.BlockSpec((B, G_B * LANES), lambda n, p, qi, kj, fl: (qi[p], n)),
                  pl.BlockSpec((B, LANES), lambda n, p, qi, kj, fl: (kj[p], n)),
                  pl.BlockSpec((B, LANES), lambda n, p, qi, kj, fl: (kj[p], n)),
                  const(k_gain.shape), const(base.shape),
                  const(lq1.shape), const(lk1.shape), const(lq2.shape), const(lk2.shape)],
        out_specs=pl.BlockSpec((B, G_B * LANES), lambda n, p, qi, kj, fl: (qi[p], n)),
        scratch_shapes=[pltpu.VMEM((2 * G_B, B, 2 * LANES), BF16)] + [pltpu.VMEM((2 * G_B, B, LANES), F32)] * 3,
    )
    return pl.pallas_call(
        functools.partial(_diff_prompt_kernel, B=B, n_heads=n_heads, lam_init=lam_init),
        out_shape=jax.ShapeDtypeStruct(qb.shape, F32),
        grid_spec=grid_spec,
        compiler_params=_cparams(("parallel", "arbitrary")),
        name="diff_prompt",
    )(jnp.asarray(qi), jnp.asarray(kj), fast_flag, qb, kb, vb, k_gain, base, lq1, lk1, lq2, lk2)


def _diff_sample_kernel(pt_ref, flag_ref, q_ref, kn_ref, vn_ref, kg_ref, lq1_ref, lk1_ref, lq2_ref, lk2_ref,
                        ck_ref, cv_ref, o_ref, kbuf, vbuf, ksem, vsem, m_scr, l_scr, acc_scr,
                        *, P, page, n_chunks, n_seq, T, n_heads, lam_init):
    b = pl.program_id(0)
    c = pl.program_id(1)
    step = b * n_chunks + c
    slot = lax.rem(step, 2)
    past = n_chunks * P * page
    R = G_B * 2 * T
    fast = flag_ref[0] != 0

    def k_copy(pg, sl, p):
        return pltpu.make_async_copy(ck_ref.at[pg], kbuf.at[sl, p], ksem.at[sl])

    def v_copy(pg, sl, p):
        return pltpu.make_async_copy(cv_ref.at[pg], vbuf.at[sl, p], vsem.at[sl])

    def issue(bb, cc, sl):
        for p in range(P):
            pg = pt_ref[bb, cc * P + p]
            k_copy(pg, sl, p).start()
            v_copy(pg, sl, p).start()

    @pl.when(step == 0)
    def _():
        issue(0, 0, 0)

    nxt = step + 1

    @pl.when(nxt < n_seq * n_chunks)
    def _():
        issue(nxt // n_chunks, lax.rem(nxt, n_chunks), 1 - slot)

    for p in range(P):
        k_copy(0, slot, p).wait()
        v_copy(0, slot, p).wait()

    lane = lax.broadcasted_iota(jnp.int32, (T, LANES), 1)
    rrow = lax.broadcasted_iota(jnp.int32, (R, 1), 0)
    trow1 = lax.rem(rrow, T).astype(F32)
    col = lax.broadcasted_iota(jnp.int32, (1, P * page), 1)
    rel = (col + (c * (P * page) - past)).astype(F32)

    def lhs_and_slopes(n):
        blocks = []
        for g in range(G_B):
            qh = q_ref[:, (n * G_B + g) * LANES:(n * G_B + g + 1) * LANES]
            for cs in range(2):
                half = (lane < D_B) if cs == 0 else (lane >= D_B)
                blocks.append(jnp.where(half, qh, 0.0))
        lhs = jnp.concatenate(blocks, axis=0)
        slopes = jnp.where(rrow < 2 * T, _alibi_slope(n_heads, n * G_B), _alibi_slope(n_heads, n * G_B + 1)) * LOG2E
        return lhs, slopes

    def cached_kv(n):
        kt = jnp.concatenate([kbuf[slot, p, n * LANES:(n + 1) * LANES, :] for p in range(P)], axis=1).astype(BF16)
        vv = jnp.concatenate([vbuf[slot, p, pl.ds(n, page, stride=KVH_B), :] for p in range(P)], axis=0).astype(BF16)
        return kt, vv

    def new_kv(n):
        zpad = jnp.zeros((LANES - T, LANES), F32)
        kn = jnp.concatenate([kn_ref[:, n * LANES:(n + 1) * LANES], zpad], axis=0).astype(BF16)
        vn = jnp.concatenate([vn_ref[:, n * LANES:(n + 1) * LANES], zpad], axis=0).astype(BF16)
        ncol = lax.broadcasted_iota(jnp.int32, (R, LANES), 1)
        keep = ncol <= lax.rem(lax.broadcasted_iota(jnp.int32, (R, LANES), 0), T)
        return kn, vn, ncol.astype(F32), keep

    def write_out(n, res, lam):
        for g in range(G_B):
            r0 = g * 2 * T
            o_ref[:, (n * G_B + g) * LANES:(n * G_B + g + 1) * LANES] = res[r0:r0 + T] - lam * res[r0 + T:r0 + 2 * T]

    def run(use_bound):
        @pl.when(c == 0)
        def _():
            if not use_bound:
                m_scr[...] = jnp.full_like(m_scr, NEG_BIG)
            l_scr[...] = jnp.zeros_like(l_scr)
            acc_scr[...] = jnp.zeros_like(acc_scr)

        kmax = math.sqrt(D_B) * jnp.max(jnp.abs(kg_ref[...]), axis=-1, keepdims=True)

        def row_shift(lhs, slopes):
            return slopes * trow1 + jnp.sqrt(jnp.sum(lhs * lhs, axis=1, keepdims=True)) * kmax

        def accumulate(n, s, vv):
            if use_bound:
                pr = jnp.exp2(s)
                l_scr[n] = l_scr[n] + _lane_tile_sum(pr)
                acc_scr[n] = acc_scr[n] + _dot(pr.astype(BF16), vv)
            else:
                _online_update(n, s, vv, m_scr, l_scr, acc_scr)

        for n in range(KVH_B):
            lhs, slopes = lhs_and_slopes(n)
            kt, vv = cached_kv(n)
            s = _dot(lhs.astype(BF16), kt) + slopes * rel
            if use_bound:
                s = s - row_shift(lhs, slopes)
            accumulate(n, s, vv)

        @pl.when(c == n_chunks - 1)
        def _():
            lam = _lam_value(lq1_ref[...], lk1_ref[...], lq2_ref[...], lk2_ref[...], lam_init)
            for n in range(KVH_B):
                lhs, slopes = lhs_and_slopes(n)
                kn, vn, ncol, keep = new_kv(n)
                s = _dot_nt(lhs.astype(BF16), kn) + slopes * ncol
                if use_bound:
                    s = s - row_shift(lhs, slopes)
                accumulate(n, jnp.where(keep, s, NEG_BIG), vn)
                den = jnp.sum(l_scr[n], axis=1, keepdims=True) if use_bound else l_scr[n]
                write_out(n, acc_scr[n] / den, lam)

    @pl.when(fast)
    def _():
        run(True)

    @pl.when(jnp.logical_not(fast))
    def _():
        run(False)


def _diff_sample(qb, kb_new, vb_new, page_table, kt_pages, v_pages, k_gain, fast_flag, lq1, lk1, lq2, lk2, lam_init,
                 T, P):
    n_seq, n_pages = page_table.shape
    page = kt_pages.shape[2]
    n_heads = qb.shape[1] // LANES
    assert n_pages % P == 0 and qb.shape[0] == n_seq * T and page == LANES
    n_chunks = n_pages // P
    R = G_B * 2 * T
    small = pl.BlockSpec(lq1.shape, lambda b, c, pt, fl: (0, 0))
    rows = lambda w: pl.BlockSpec((T, w), lambda b, c, pt, fl: (b, 0))
    grid_spec = pltpu.PrefetchScalarGridSpec(
        num_scalar_prefetch=2,
        grid=(n_seq, n_chunks),
        in_specs=[rows(qb.shape[1]), rows(kb_new.shape[1]), rows(vb_new.shape[1]), small, small, small, small, small,
                  pl.BlockSpec(memory_space=pl.ANY), pl.BlockSpec(memory_space=pl.ANY)],
        out_specs=rows(qb.shape[1]),
        scratch_shapes=[pltpu.VMEM((2, P) + kt_pages.shape[1:], F32),
                        pltpu.VMEM((2, P) + v_pages.shape[1:], F32),
                        pltpu.SemaphoreType.DMA((2,)), pltpu.SemaphoreType.DMA((2,)),
                        pltpu.VMEM((KVH_B, R, LANES), F32), pltpu.VMEM((KVH_B, R, LANES), F32),
                        pltpu.VMEM((KVH_B, R, LANES), F32)],
    )
    return pl.pallas_call(
        functools.partial(_diff_sample_kernel, P=P, page=page, n_chunks=n_chunks, n_seq=n_seq, T=T,
                          n_heads=n_heads, lam_init=lam_init),
        out_shape=jax.ShapeDtypeStruct(qb.shape, F32),
        grid_spec=grid_spec,
        compiler_params=_cparams(("arbitrary", "arbitrary")),
        name="diff_sample",
    )(page_table, fast_flag, qb, kb_new, vb_new, k_gain, lq1, lk1, lq2, lk2, kt_pages, v_pages)


def _outproj_a_kernel(x_ref, oa_ref, gsa_ref, ob_ref, gsb_ref, og_ref, sg_ref, w_ref, y_ref, *, lam_init):
    def branch(o_ref, gate_ref, gain, scale):
        parts = []
        for h in range(o_ref.shape[1] // LANES):
            sl = slice(h * LANES, (h + 1) * LANES)
            parts.append(_rms_rows(o_ref[:, sl], gain) * scale * gate_ref[:, sl])
        return jnp.concatenate(parts, axis=1).astype(BF16)

    ya = branch(oa_ref, gsa_ref, og_ref[...], 1.0)
    yb = branch(ob_ref, gsb_ref, sg_ref[...], 1.0 - lam_init)
    wa = oa_ref.shape[1]
    y_ref[...] = x_ref[...] + _dot(ya, w_ref[0:wa, :]) + _dot(yb, w_ref[wa:, :])


def _outproj_a(x, oa, gsa, ob, gsb, o_gain, subln, w_bf16, lam_init, tm):
    T, D = x.shape
    row = lambda w: pl.BlockSpec((tm, w), lambda i: (i, 0))
    return pl.pallas_call(
        functools.partial(_outproj_a_kernel, lam_init=lam_init),
        out_shape=jax.ShapeDtypeStruct((T, D), F32),
        grid=(T // tm,),
        in_specs=[row(D), row(oa.shape[1]), row(oa.shape[1]), row(ob.shape[1]), row(ob.shape[1]),
                  _full(o_gain.shape), _full(subln.shape), _full(w_bf16.shape)],
        out_specs=row(D),
        compiler_params=_cparams(("parallel",)),
        name="outproj_a",
    )(x, oa, gsa, ob, gsb, o_gain, subln, w_bf16)


def _inproj_c_kernel(x_ref, g_ref, w_ref, qg_ref, kg_ref, bd_ref, q_ref, k_ref, v_ref, gs_ref, *, wq, wkv):
    hb = _rms_rows(x_ref[...], g_ref[...]).astype(BF16)
    bd = bd_ref[...]
    q_ref[...] = _group_rms(_dot(hb, w_ref[:, 0:wq]), qg_ref[...], bd) * (HD_C ** -0.5 * LOG2E)
    k_ref[...] = _group_rms(_dot(hb, w_ref[:, wq:wq + wkv]), kg_ref[...], bd)
    v_ref[...] = _dot(hb, w_ref[:, wq + wkv:wq + 2 * wkv])
    gs_ref[...] = _silu(_dot(hb, w_ref[:, wq + 2 * wkv:]))


def _inproj_c(x, g, w_bf16, q_gain_t, k_gain_t, bd, tm):
    T, D = x.shape
    wq, wkv = q_gain_t.shape[1], k_gain_t.shape[1]
    widths = (wq, wkv, wkv, wq)
    row = lambda w: pl.BlockSpec((tm, w), lambda i: (i, 0))
    return pl.pallas_call(
        functools.partial(_inproj_c_kernel, wq=wq, wkv=wkv),
        out_shape=[jax.ShapeDtypeStruct((T, w), F32) for w in widths],
        grid=(T // tm,),
        in_specs=[row(D), _full(g.shape), _full(w_bf16.shape), _full(q_gain_t.shape), _full(k_gain_t.shape),
                  _full(bd.shape)],
        out_specs=[row(w) for w in widths],
        compiler_params=_cparams(("parallel",)),
        name="inproj_c",
    )(x, g, w_bf16, q_gain_t, k_gain_t, bd)


def _swa_softmax_pv(s, sink, v_list):
    m = sink
    for piece in s:
        m = jnp.maximum(m, jnp.max(piece, axis=1, keepdims=True))
    den = jnp.exp2(sink - m)
    out = None
    for piece, pv in zip(s, v_list):
        p = jnp.exp2(piece - m)
        den = den + jnp.sum(p, axis=1, keepdims=True)
        term = pv(p.astype(BF16))
        out = term if out is None else out + term
    return out / den


def _swa_prompt_kernel(flag_ref, q_ref, kc_ref, kp_ref, vc_ref, vp_ref, snk_ref, qg_ref, kg_ref, o_ref, *, n_heads):
    i = pl.program_id(0)
    W = q_ref.shape[0]
    n_pairs = n_heads // KVH_C
    half_rows = n_pairs * W
    lane = lax.broadcasted_iota(jnp.int32, (W, LANES), 1)
    blocks = []
    for n in range(KVH_C):
        half = (lane < HD_C) if n == 0 else (lane >= HD_C)
        for m in range(n_pairs):
            blocks.append(jnp.where(half, q_ref[:, m * LANES:(m + 1) * LANES], 0.0))
    lhs = jnp.concatenate(blocks, axis=0).astype(BF16)
    keys = jnp.concatenate([kp_ref[...], kc_ref[...]], axis=0).astype(BF16)
    vals = jnp.concatenate([vp_ref[...], vc_ref[...]], axis=0).astype(BF16)
    r = lax.broadcasted_iota(jnp.int32, (W, 2 * W), 0)
    c = lax.broadcasted_iota(jnp.int32, (W, 2 * W), 1)
    dist = r + W - c
    valid = jnp.logical_and(jnp.logical_and(dist >= 0, dist <= W), jnp.logical_or(c >= W, i > 0))
    neg_dist = jnp.where(valid, -dist.astype(F32), NEG_BIG)

    def write(res):
        for m in range(n_pairs):
            lo = res[m * W:(m + 1) * W]
            hi = res[half_rows + m * W:half_rows + (m + 1) * W]
            o_ref[:, m * LANES:(m + 1) * LANES] = jnp.where(lane < HD_C, lo, hi)

    @pl.when(flag_ref[0] != 0)
    def _():
        bound = ((LOG2E * math.sqrt(HD_C)) * jnp.max(jnp.abs(qg_ref[...]), axis=-1, keepdims=True)
                 * jnp.max(jnp.abs(kg_ref[...]), axis=-1, keepdims=True))
        biases, sink_terms = [], []
        for h in range(n_heads):
            sink_h = snk_ref[:, h:h + 1] * LOG2E
            shift = jnp.maximum(bound, sink_h)
            biases.append((_alibi_slope(n_heads, h) * LOG2E) * neg_dist - shift)
            sink_terms.append(jnp.broadcast_to(jnp.exp2(sink_h - shift), (W, LANES)))
        pr = jnp.exp2(_dot_nt(lhs, keys) + jnp.concatenate(biases, axis=0))
        vals1 = jnp.concatenate([vals, jnp.ones_like(vals)], axis=1)
        res = _dot(pr.astype(BF16), vals1)
        den = res[:, LANES:] + jnp.concatenate(sink_terms, axis=0)
        write(res[:, :LANES] / den)

    @pl.when(flag_ref[0] == 0)
    def _():
        slopes, sinks = [], []
        for h in range(n_heads):
            slopes.append(jnp.full((W, 1), _alibi_slope(n_heads, h) * LOG2E, F32))
            sinks.append(jnp.broadcast_to(snk_ref[:, h:h + 1] * LOG2E, (W, 1)))
        slope = jnp.concatenate(slopes, axis=0)
        sink = jnp.concatenate(sinks, axis=0)
        s = _dot_nt(lhs, keys) + slope * jnp.tile(neg_dist, (n_heads, 1))
        m_row = jnp.maximum(jnp.max(s, axis=1, keepdims=True), sink)
        pr = jnp.exp2(s - m_row)
        den = jnp.sum(pr, axis=1, keepdims=True) + jnp.exp2(sink - m_row)
        write(_dot(pr.astype(BF16), vals) / den)


def _swa_prompt(q, k, v, sinks, q_gain, k_gain, fast_flag):
    T, Wq = q.shape
    n_heads = Wq // HD_C
    nb = T // WINDOW
    cur = lambda w: pl.BlockSpec((WINDOW, w), lambda i, fl: (i, 0))
    prev = lambda w: pl.BlockSpec((WINDOW, w), lambda i, fl: (jnp.maximum(i - 1, 0), 0))
    const = lambda a: pl.BlockSpec(a.shape, lambda i, fl: (0,) * a.ndim)
    kw = k.shape[1]
    grid_spec = pltpu.PrefetchScalarGridSpec(
        num_scalar_prefetch=1,
        grid=(nb,),
        in_specs=[cur(Wq), cur(kw), prev(kw), cur(kw), prev(kw), const(sinks), const(q_gain), const(k_gain)],
        out_specs=cur(Wq),
    )
    return pl.pallas_call(
        functools.partial(_swa_prompt_kernel, n_heads=n_heads),
        out_shape=jax.ShapeDtypeStruct(q.shape, F32),
        grid_spec=grid_spec,
        compiler_params=_cparams(("parallel",)),
        name="swa_prompt",
    )(fast_flag, q, k, k, v, v, sinks, q_gain, k_gain)


def _swa_sample_kernel(q_ref, kn_ref, vn_ref, ckt_ref, cvt_ref, snk_ref, o_ref, *, T, n_heads, n_sub):
    n_pairs = n_heads // KVH_C
    R = n_heads * T
    lane = lax.broadcasted_iota(jnp.int32, (T, LANES), 1)
    rr = lax.broadcasted_iota(jnp.int32, (R, WINDOW), 0)
    cc = lax.broadcasted_iota(jnp.int32, (R, WINDOW), 1)
    t = lax.rem(rr, T)
    hrow = lax.broadcasted_iota(jnp.int32, (R, 1), 0) // T
    slope = jnp.exp2(-8.0 * (hrow + 1).astype(F32) / n_heads) * LOG2E
    sink = jnp.zeros((R, 1), F32)
    for h in range(n_heads):
        sink = jnp.where(hrow == h, snk_ref[:, h:h + 1], sink)
    sink = sink * LOG2E
    bias_c = jnp.where(cc >= t, -slope * (t + WINDOW - cc).astype(F32), NEG_BIG)
    bias_n = jnp.where(cc <= t, -slope * (t - cc).astype(F32), NEG_BIG)
    zpad = jnp.zeros((WINDOW - T, LANES), F32)
    half_rows = n_pairs * T
    for sq in range(n_sub):
        rows = slice(sq * T, (sq + 1) * T)
        blocks = []
        for n in range(KVH_C):
            half = (lane < HD_C) if n == 0 else (lane >= HD_C)
            for m in range(n_pairs):
                blocks.append(jnp.where(half, q_ref[rows, m * LANES:(m + 1) * LANES], 0.0))
        lhs = jnp.concatenate(blocks, axis=0).astype(BF16)
        kn = jnp.concatenate([kn_ref[rows, :], zpad], axis=0).astype(BF16)
        vn = jnp.concatenate([vn_ref[rows, :], zpad], axis=0).astype(BF16)
        ckt = ckt_ref[sq].astype(BF16)
        cvt = cvt_ref[sq].astype(BF16)
        s_c = _dot(lhs, ckt) + bias_c
        s_n = _dot_nt(lhs, kn) + bias_n
        res = _swa_softmax_pv([s_c, s_n], sink, [lambda p: _dot_nt(p, cvt), lambda p: _dot(p, vn)])
        for m in range(n_pairs):
            lo = res[m * T:(m + 1) * T]
            hi = res[half_rows + m * T:half_rows + (m + 1) * T]
            o_ref[rows, m * LANES:(m + 1) * LANES] = jnp.where(lane < HD_C, lo, hi)


def _swa_sample(q, k_new, v_new, ckt, cvt, sinks, T, n_sub):
    BT, Wq = q.shape
    B = ckt.shape[0]
    n_heads = Wq // HD_C
    assert B % n_sub == 0
    rows = lambda w: pl.BlockSpec((n_sub * T, w), lambda b: (b, 0))
    cache = pl.BlockSpec((n_sub,) + ckt.shape[1:], lambda b: (b, 0, 0))
    return pl.pallas_call(
        functools.partial(_swa_sample_kernel, T=T, n_heads=n_heads, n_sub=n_sub),
        out_shape=jax.ShapeDtypeStruct(q.shape, F32),
        grid=(B // n_sub,),
        in_specs=[rows(Wq), rows(k_new.shape[1]), rows(v_new.shape[1]), cache, cache, _full(sinks.shape)],
        out_specs=rows(Wq),
        compiler_params=_cparams(("parallel",)),
        name="swa_sample",
    )(q, k_new, v_new, ckt, cvt, sinks)


def _outproj_c_kernel(x_ref, o_ref, gs_ref, w_ref, y_ref):
    y_ref[...] = x_ref[...] + _dot((o_ref[...] * gs_ref[...]).astype(BF16), w_ref[...])


def _outproj_c(x, o, gs, w_bf16, tm):
    T, D = x.shape
    row = lambda w: pl.BlockSpec((tm, w), lambda i: (i, 0))
    return pl.pallas_call(
        _outproj_c_kernel,
        out_shape=jax.ShapeDtypeStruct((T, D), F32),
        grid=(T // tm,),
        in_specs=[row(D), row(o.shape[1]), row(o.shape[1]), _full(w_bf16.shape)],
        out_specs=row(D),
        compiler_params=_cparams(("parallel",)),
        name="outproj_c",
    )(x, o, gs, w_bf16)


TOKEN_TILE = 256
GLA_CHUNK = 128
DIFF_BLOCK = 512
DIFF_PAGES_PER_STEP = 32
SWA_SEQS_PER_STEP = 8
GLA_SEQS_PER_STEP = 4
GLA_HEADS_PER_STEP = 4


def _pair_perm(n_heads):
    n_pairs = n_heads // KVH_C
    cols = []
    for m in range(n_pairs):
        for n in range(KVH_C):
            h = n * n_pairs + m
            cols.extend(range(h * HD_C, (h + 1) * HD_C))
    return np.asarray(cols, np.int32)


def kernel(x_prompt, x_sample, state_hgrn, cache_k_diff, cache_v_diff, cache_k_swa, cache_v_swa, page_table, norm_a, w_in_a, w_out_a, lb_logits, hgrn_out_gain, diff_q_gain, diff_k_gain, diff_subln_gain, lam_q1, lam_k1, lam_q2, lam_k2, norm_c, w_in_c, w_out_c, swa_q_gain, swa_k_gain, sinks):
    batch, seq, d_model = x_prompt.shape
    n_seq, t_dec, _ = x_sample.shape
    assert batch == 1
    n_even, n_odd = norm_a.shape[0], norm_c.shape[0]
    depth = n_even + n_odd
    h_a = state_hgrn.shape[2]
    n_pool, page = cache_k_diff.shape[1], cache_k_diff.shape[2]
    w_b = w_out_a.shape[1] - h_a * DV_A
    h_b = w_b // (2 * D_B)
    h_c = sinks.shape[1]

    xp = x_prompt.reshape(seq, d_model)
    xs = x_sample.reshape(n_seq * t_dec, d_model)
    bd64 = _block_diag_mean(2 * LANES, D_B)
    bd64_c = _block_diag_mean(LANES, HD_C)
    perm = _pair_perm(h_c)

    hs_p, hs_s, kdp, vdp, kds, vds, ksp, vsp, kss, vss = ([] for _ in range(10))
    for l in range(depth):
        if l % 2 == 0:
            e = l // 2
            lam_init = 0.8 - 0.6 * math.exp(-0.3 * l)
            w_in = w_in_a[e].astype(BF16)
            w_out = w_out_a[e].astype(BF16)
            g = norm_a[e:e + 1]
            qg = jnp.tile(diff_q_gain[e:e + 1], (1, h_b * 2))
            kg = jnp.tile(diff_k_gain[e:e + 1], (1, KVH_B * 2))
            lam_args = (lam_q1[e:e + 1], lam_k1[e:e + 1], lam_q2[e:e + 1], lam_k2[e:e + 1])
            kt_pages = jnp.transpose(cache_k_diff[e], (0, 2, 3, 4, 1)).reshape(n_pool, KVH_B * 2 * D_B, page)
            v_pages = cache_v_diff[e].reshape(n_pool, page * KVH_B, 2 * D_B)

            pa = _inproj_a(xp, g, w_in, lb_logits, qg, kg, bd64, e, TOKEN_TILE)
            sa = _inproj_a(xs, g, w_in, lb_logits, qg, kg, bd64, e, TOKEN_TILE)
            qa_p, lf_p, kk_p, ia_p, gsa_p, qb_p, kb_p, vb_p, gsb_p = pa
            qa_s, lf_s, kk_s, ia_s, gsa_s, qb_s, kb_s, vb_s, gsb_s = sa

            lb_min = jnp.min(jnp.cumsum(jax.nn.softmax(lb_logits.astype(F32), axis=0), axis=0)[e])
            gla_flag = (-GLA_SUB * jnp.log(lb_min) <= GLA_FAST_DECAY).astype(jnp.int32).reshape(1)
            oa_p, st_p = _gla_prompt(qa_p, kk_p, ia_p, lf_p, gla_flag, GLA_CHUNK, GLA_HEADS_PER_STEP)
            oa_s, st_s = _gla_sample(qa_s, kk_s, ia_s, lf_s, state_hgrn[e], t_dec, GLA_SEQS_PER_STEP)
            logit_bound = (LOG2E * math.sqrt(D_B)) * jnp.max(jnp.abs(diff_q_gain[e])) * jnp.max(jnp.abs(diff_k_gain[e]))
            fast_flag = (logit_bound <= FAST_LOGIT_BOUND).astype(jnp.int32).reshape(1)
            k_gain = diff_k_gain[e:e + 1]
            ob_p = _diff_prompt(qb_p, kb_p, vb_p, k_gain, fast_flag, *lam_args, lam_init, DIFF_BLOCK)
            ob_s = _diff_sample(qb_s, kb_s, vb_s, page_table, kt_pages, v_pages, k_gain, fast_flag, *lam_args,
                                lam_init, t_dec, DIFF_PAGES_PER_STEP)

            og, sg = hgrn_out_gain[e:e + 1], diff_subln_gain[e:e + 1]
            xp = _outproj_a(xp, oa_p, gsa_p, ob_p, gsb_p, og, sg, w_out, lam_init, TOKEN_TILE)
            xs = _outproj_a(xs, oa_s, gsa_s, ob_s, gsb_s, og, sg, w_out, lam_init, TOKEN_TILE)

            hs_p.append(st_p[None])
            hs_s.append(st_s)
            kdp.append(kb_p.reshape(batch, seq, KVH_B, 2, D_B))
            vdp.append(vb_p.reshape(batch, seq, KVH_B, 2 * D_B))
            kds.append(kb_s.reshape(n_seq, t_dec, KVH_B, 2, D_B))
            vds.append(vb_s.reshape(n_seq, t_dec, KVH_B, 2 * D_B))
        else:
            o = l // 2
            wq = h_c * HD_C
            wkv = KVH_C * HD_C
            w_full = w_in_c[o]
            w_in = jnp.concatenate([w_full[:, :wq][:, perm], w_full[:, wq:wq + 2 * wkv],
                                    w_full[:, wq + 2 * wkv:][:, perm]], axis=1).astype(BF16)
            w_out = w_out_c[o][perm, :].astype(BF16)
            g = norm_c[o:o + 1]
            qg = jnp.tile(swa_q_gain[o:o + 1], (1, h_c))
            kg = jnp.tile(swa_k_gain[o:o + 1], (1, KVH_C))
            snk = sinks[o:o + 1]
            ckt = jnp.transpose(cache_k_swa[o], (0, 2, 3, 1)).reshape(n_seq, wkv, WINDOW)
            cvt = jnp.transpose(cache_v_swa[o], (0, 2, 3, 1)).reshape(n_seq, wkv, WINDOW)

            q_p, k_p, v_p, gs_p = _inproj_c(xp, g, w_in, qg, kg, bd64_c, TOKEN_TILE)
            q_s, k_s, v_s, gs_s = _inproj_c(xs, g, w_in, qg, kg, bd64_c, TOKEN_TILE)
            swa_bound = ((LOG2E * math.sqrt(HD_C)) * jnp.max(jnp.abs(swa_q_gain[o]))
                         * jnp.max(jnp.abs(swa_k_gain[o])))
            swa_flag = (swa_bound <= FAST_LOGIT_BOUND).astype(jnp.int32).reshape(1)
            o_p = _swa_prompt(q_p, k_p, v_p, snk, swa_q_gain[o:o + 1], swa_k_gain[o:o + 1], swa_flag)
            o_s = _swa_sample(q_s, k_s, v_s, ckt, cvt, snk, t_dec, SWA_SEQS_PER_STEP)
            xp = _outproj_c(xp, o_p, gs_p, w_out, TOKEN_TILE)
            xs = _outproj_c(xs, o_s, gs_s, w_out, TOKEN_TILE)

            ksp.append(k_p[-WINDOW:].reshape(batch, WINDOW, KVH_C, HD_C))
            vsp.append(v_p[-WINDOW:].reshape(batch, WINDOW, KVH_C, HD_C))
            k_new = k_s.reshape(n_seq, t_dec, KVH_C, HD_C)
            v_new = v_s.reshape(n_seq, t_dec, KVH_C, HD_C)
            kss.append(jnp.concatenate([cache_k_swa[o], k_new], axis=1)[:, -WINDOW:])
            vss.append(jnp.concatenate([cache_v_swa[o], v_new], axis=1)[:, -WINDOW:])
    return (xp.reshape(batch, seq, d_model), xs.reshape(n_seq, t_dec, d_model),
            jnp.stack(hs_p), jnp.stack(hs_s), jnp.stack(kdp), jnp.stack(vdp), jnp.stack(kds), jnp.stack(vds),
            jnp.stack(ksp), jnp.stack(vsp), jnp.stack(kss), jnp.stack(vss))
```

```python
import functools
import math

import jax
import jax.numpy as jnp
import ml_dtypes
import numpy as np
from jax import lax
from jax.experimental import pallas as pl
from jax.experimental.pallas import tpu as pltpu

F32 = jnp.float32
BF16 = jnp.bfloat16

EPS = 1e-6
LOG2E = 1.4426950408889634
NEG_BIG = -1e30
LANES = 128
VMEM_LIMIT_BYTES = 56 * 1024 * 1024

DK_A = 128
DV_A = 128
D_B = 64
KVH_B = 2
G_B = 2
HD_C = 64
KVH_C = 2
WINDOW = 128
GLA_SUB = 16
FAST_LOGIT_BOUND = 60.0
GLA_FAST_DECAY = 80.0
ONES_ROWS = 16


def _dot(a, b):
    return jnp.dot(a, b, preferred_element_type=F32)


def _dot_nt(a, b):
    return lax.dot_general(a, b, (((1,), (1,)), ((), ())), preferred_element_type=F32)


def _dot_tn(a, b):
    return lax.dot_general(a, b, (((0,), (0,)), ((), ())), preferred_element_type=F32)


def _silu(x):
    return x * jax.nn.sigmoid(x)


def _rms_rows(x, gain):
    return x * lax.rsqrt(jnp.mean(x * x, axis=-1, keepdims=True) + EPS) * gain


def _split3(x):
    x1 = x.astype(BF16)
    r1 = x - x1.astype(F32)
    x2 = r1.astype(BF16)
    x3 = (r1 - x2.astype(F32)).astype(BF16)
    return x1, x2, x3


def _group_meansq(x, bd):
    s = x * x
    hi = s.astype(BF16)
    lo = (s - hi.astype(F32)).astype(BF16)
    return _dot(hi, bd) + _dot(lo, bd)


def _group_rms(x, gain, bd):
    wb = bd.shape[0]
    outs = []
    for j in range(x.shape[1] // wb):
        part = x[:, j * wb:(j + 1) * wb]
        ms = _group_meansq(part, bd)
        outs.append(part * lax.rsqrt(ms + EPS) * gain[:, j * wb:(j + 1) * wb])
    return outs[0] if len(outs) == 1 else jnp.concatenate(outs, axis=1)


def _block_diag_mean(width, group):
    r = np.arange(width)[:, None] // group
    c = np.arange(width)[None, :] // group
    return jnp.asarray((r == c).astype(np.float32) / group, dtype=BF16)


def _lam_value(lq1, lk1, lq2, lk2, lam_init):
    a = jnp.sum(lq1 * lk1, axis=-1, keepdims=True)
    b = jnp.sum(lq2 * lk2, axis=-1, keepdims=True)
    return jnp.exp(a) - jnp.exp(b) + lam_init


def _split3_const(x):
    x = np.float32(x)
    a = np.float32(ml_dtypes.bfloat16(x))
    b = np.float32(ml_dtypes.bfloat16(np.float32(x - a)))
    c = np.float32(ml_dtypes.bfloat16(np.float32(np.float32(x - a) - b)))
    return float(a), float(b), float(c)


def _lane_tile_sum(p):
    out = p[:, 0:LANES]
    for t in range(1, p.shape[1] // LANES):
        out = out + p[:, t * LANES:(t + 1) * LANES]
    return out


def _cparams(sem):
    return pltpu.CompilerParams(dimension_semantics=sem, vmem_limit_bytes=VMEM_LIMIT_BYTES)


def _full(shape):
    nd = len(shape)
    return pl.BlockSpec(shape, lambda *_: (0,) * nd)


def _inproj_a_kernel(x_ref, g_ref, w_ref, lbl_ref, qg_ref, kg_ref, bd_ref,
                     qa_ref, lf_ref, kk_ref, ia_ref, gsa_ref, qb_ref, kb_ref, vb_ref, gsb_ref, *, e, wa, wb, wkv):
    hb = _rms_rows(x_ref[...], g_ref[...]).astype(BF16)
    bd = bd_ref[...]
    o = [0]

    def seg(width):
        z = _dot(hb, w_ref[:, o[0]:o[0] + width])
        o[0] += width
        return z

    qa = seg(wa)
    qa_ref[...] = _silu(qa) * (DK_A ** -0.5)
    lbl = lbl_ref[...]
    ex = jnp.exp(lbl - jnp.max(lbl, axis=0, keepdims=True))
    sm = ex / jnp.sum(ex, axis=0, keepdims=True)
    lb = jnp.sum(sm[0:e + 1, :], axis=0, keepdims=True)
    f = lb + (1.0 - lb) * jax.nn.sigmoid(seg(wa))
    lf_ref[...] = jnp.log(f)
    kk_ref[...] = 1.0 - f
    ia_ref[...] = seg(wa)
    gsa_ref[...] = _silu(seg(wa))
    qb_ref[...] = _group_rms(seg(wb), qg_ref[...], bd) * (D_B ** -0.5 * LOG2E)
    kb_ref[...] = _group_rms(seg(wkv), kg_ref[...], bd)
    vb_ref[...] = seg(wkv)
    gsb_ref[...] = _silu(seg(wb))


def _inproj_a(x, g, w_bf16, lb_logits, q_gain_t, k_gain_t, bd, e, tm):
    T, D = x.shape
    wa, wb, wkv = lb_logits.shape[1], q_gain_t.shape[1], k_gain_t.shape[1]
    n_in = w_bf16.shape[1]
    assert n_in == 4 * wa + 2 * wb + 2 * wkv and T % tm == 0
    widths = (wa, wa, wa, wa, wa, wb, wkv, wkv, wb)
    row = lambda w: pl.BlockSpec((tm, w), lambda i: (i, 0))
    return pl.pallas_call(
        functools.partial(_inproj_a_kernel, e=e, wa=wa, wb=wb, wkv=wkv),
        out_shape=[jax.ShapeDtypeStruct((T, w), F32) for w in widths],
        grid=(T // tm,),
        in_specs=[row(D), _full(g.shape), _full(w_bf16.shape), _full(lb_logits.shape),
                  _full(q_gain_t.shape), _full(k_gain_t.shape), _full(bd.shape)],
        out_specs=[row(w) for w in widths],
        compiler_params=_cparams(("parallel",)),
        name="inproj_a",
    )(x, g, w_bf16, lb_logits, q_gain_t, k_gain_t, bd)


def _gla_intra(q, k, b, bx, v_bf, C, c):
    n = C // c
    lane = lax.broadcasted_iota(jnp.int32, (c, C), 1)
    trow = lax.broadcasted_iota(jnp.int32, (c, LANES), 0)
    ones = jnp.ones((LANES, LANES), BF16)
    rows = []
    for i in range(n):
        sl = slice(i * c, (i + 1) * c)
        qi, ki, bi = q[sl], k[sl], b[sl]
        if i > 0:
            b0 = bx[i * c:i * c + 1, :]
            ks = k * jnp.exp(jnp.minimum(b0 - b, 0.0))
            qh = qi * jnp.exp(bi - b0)
            a_i = jnp.where(lane < i * c, _dot_nt(qh.astype(BF16), ks.astype(BF16)), 0.0)
        else:
            a_i = jnp.zeros((c, C), F32)
        pieces = []
        for s in range(c):
            d = jnp.minimum(bi - bi[s:s + 1, :], 0.0)
            pieces.append(jnp.where(trow >= s, qi * ki[s:s + 1, :] * jnp.exp(d), 0.0))
        r = _dot(jnp.concatenate(pieces, axis=0).astype(BF16), ones)
        for s in range(c):
            a_i = jnp.where(lane == i * c + s, r[s * c:(s + 1) * c, :C], a_i)
        rows.append(a_i)
    a = rows[0] if n == 1 else jnp.concatenate(rows, axis=0)
    return _dot(a.astype(BF16), v_bf)


def _gla_intra_bounded(q, k, b, bx, v_bf, C, c):
    lane = lax.broadcasted_iota(jnp.int32, (c, C), 1)
    trow = lax.broadcasted_iota(jnp.int32, (c, C), 0)
    rows = []
    for i in range(C // c):
        sl = slice(i * c, (i + 1) * c)
        b0 = bx[i * c:i * c + 1, :]
        ks = k * jnp.exp(jnp.minimum(b0 - b, GLA_FAST_DECAY))
        qh = q[sl] * jnp.exp(b[sl] - b0)
        rows.append(jnp.where(lane <= trow + i * c, _dot_nt(qh.astype(BF16), ks.astype(BF16)), 0.0))
    return _dot(jnp.concatenate(rows, axis=0).astype(BF16), v_bf)


def _gla_prompt_kernel(flag_ref, q_ref, k_ref, v_ref, lf_ref, o_ref, s_ref, st_scr, *, C, c, hps):
    j = pl.program_id(1)

    @pl.when(j == 0)
    def _():
        st_scr[...] = jnp.zeros_like(st_scr)

    def chunk(intra):
        rr = lax.broadcasted_iota(jnp.int32, (C, C), 0)
        cc = lax.broadcasted_iota(jnp.int32, (C, C), 1)
        tri = (cc <= rr).astype(BF16)
        for h in range(hps):
            sl = slice(h * DK_A, (h + 1) * DK_A)
            q, k, v, lf = q_ref[:, sl], k_ref[:, sl], v_ref[:, sl], lf_ref[:, sl]
            l1, l2, l3 = _split3(lf)
            b = _dot(tri, l1) + _dot(tri, l2) + _dot(tri, l3)
            st = st_scr[h]
            v_bf = v.astype(BF16)
            o = _dot_nt((q * jnp.exp(b)).astype(BF16), st.astype(BF16))
            o_ref[:, sl] = o + intra(q, k, b, b - lf, v_bf, C, c)
            b_end = b[C - 1:C, :]
            kt = k * jnp.exp(b_end - b)
            st_scr[h] = st * jnp.exp(b_end) + _dot_tn(v_bf, kt.astype(BF16))

    @pl.when(flag_ref[0] != 0)
    def _():
        chunk(_gla_intra_bounded)

    @pl.when(flag_ref[0] == 0)
    def _():
        chunk(_gla_intra)

    @pl.when(j == pl.num_programs(1) - 1)
    def _():
        for h in range(hps):
            s_ref[h] = st_scr[h].T


def _gla_prompt(q, k, v, lf, bounded_flag, C, hps):
    T, W = q.shape
    H = W // DK_A
    assert T % C == 0 and C % GLA_SUB == 0 and H % hps == 0
    blk = pl.BlockSpec((C, hps * DK_A), lambda h, j, fl: (j, h))
    grid_spec = pltpu.PrefetchScalarGridSpec(
        num_scalar_prefetch=1,
        grid=(H // hps, T // C),
        in_specs=[blk, blk, blk, blk],
        out_specs=[blk, pl.BlockSpec((hps, DK_A, DV_A), lambda h, j, fl: (h, 0, 0))],
        scratch_shapes=[pltpu.VMEM((hps, DV_A, DK_A), F32)],
    )
    return pl.pallas_call(
        functools.partial(_gla_prompt_kernel, C=C, c=GLA_SUB, hps=hps),
        out_shape=[jax.ShapeDtypeStruct((T, W), F32), jax.ShapeDtypeStruct((H, DK_A, DV_A), F32)],
        grid_spec=grid_spec,
        compiler_params=_cparams(("parallel", "arbitrary")),
        name="gla_prompt",
    )(bounded_flag, q, k, v, lf)


def _gla_sample_kernel(q_ref, k_ref, v_ref, lf_ref, s0_ref, o_ref, s_ref, *, T, H, n_sub):
    pad = LANES - T
    row = lax.broadcasted_iota(jnp.int32, (T, LANES), 0)
    zpad = jnp.zeros((pad, LANES), F32)
    for sq, h in [(sq, h) for sq in range(n_sub) for h in range(H)]:
        sl = slice(h * DK_A, (h + 1) * DK_A)
        rs = slice(sq * T, (sq + 1) * T)
        q, k, v, lf = q_ref[rs, sl], k_ref[rs, sl], v_ref[rs, sl], lf_ref[rs, sl]
        b = jnp.zeros((T, LANES), F32)
        for t in range(T):
            bt = jnp.sum(jnp.where(row <= t, lf, 0.0), axis=0, keepdims=True)
            b = jnp.where(row == t, bt, b)
        st = s0_ref[sq, h].T
        v_bf = jnp.concatenate([v, zpad], axis=0).astype(BF16)
        qd = jnp.concatenate([q * jnp.exp(b), zpad], axis=0).astype(BF16)
        o = _dot_nt(qd, st.astype(BF16))[:T]
        ones = jnp.ones((LANES, LANES), BF16)
        lane = lax.broadcasted_iota(jnp.int32, (T, LANES), 1)
        pieces = []
        for s in range(T):
            d = jnp.minimum(b - b[s:s + 1, :], 0.0)
            pieces.append(jnp.where(row >= s, q * k[s:s + 1, :] * jnp.exp(d), 0.0))
        r = _dot(jnp.concatenate(pieces, axis=0).astype(BF16), ones)
        a = jnp.zeros((T, LANES), F32)
        for s in range(T):
            a = jnp.where(lane == s, r[s * T:(s + 1) * T, :], a)
        a_bf = jnp.concatenate([a, zpad], axis=0).astype(BF16)
        o_ref[rs, sl] = o + _dot(a_bf, v_bf)[:T]
        b_end = b[T - 1:T, :]
        kt = jnp.concatenate([k * jnp.exp(b_end - b), zpad], axis=0).astype(BF16)
        st_new = st * jnp.exp(b_end) + _dot_tn(v_bf, kt)
        s_ref[sq, h] = st_new.T


def _gla_sample(q, k, v, lf, s0, T, n_sub):
    BT, W = q.shape
    B, H = s0.shape[0], s0.shape[1]
    assert BT == B * T and W == H * DK_A and T % 8 == 0 and T <= GLA_SUB and B % n_sub == 0
    blk = pl.BlockSpec((n_sub * T, W), lambda i: (i, 0))
    sblk = pl.BlockSpec((n_sub, H, DK_A, DV_A), lambda i: (i, 0, 0, 0))
    return pl.pallas_call(
        functools.partial(_gla_sample_kernel, T=T, H=H, n_sub=n_sub),
        out_shape=[jax.ShapeDtypeStruct((BT, W), F32), jax.ShapeDtypeStruct(s0.shape, F32)],
        grid=(B // n_sub,),
        in_specs=[blk, blk, blk, blk, sblk],
        out_specs=[blk, sblk],
        compiler_params=_cparams(("parallel",)),
        name="gla_sample",
    )(q, k, v, lf, s0)


def _alibi_slope(n_heads, h):
    return 2.0 ** (-8.0 * (h + 1) / n_heads)


def _online_update(mi, s, v_bf, m_scr, l_scr, acc_scr):
    reps = s.shape[1] // LANES
    m_prev = m_scr[mi]
    m_new = jnp.maximum(m_prev, jnp.max(s, axis=1, keepdims=True))
    alpha = jnp.exp2(m_prev - m_new)
    p = jnp.exp2(s - jnp.tile(m_new, (1, reps)))
    l_scr[mi] = alpha * l_scr[mi] + jnp.sum(p, axis=1, keepdims=True)
    acc_scr[mi] = alpha * acc_scr[mi] + _dot(p.astype(BF16), v_bf)
    m_scr[mi] = m_new


def _diff_prompt_kernel(qi_ref, kj_ref, flag_ref, q_ref, k_ref, v_ref, kg_ref, base_ref,
                        lq1_ref, lk1_ref, lq2_ref, lk2_ref, o_ref, qxt_scr, acct_scr, m_scr, l_scr, acc_scr,
                        *, B, n_heads, lam_init):
    n = pl.program_id(0)
    p = pl.program_id(1)
    i = qi_ref[p]
    j = kj_ref[p]
    fast = flag_ref[0] != 0
    lane = lax.broadcasted_iota(jnp.int32, (B, LANES), 1)

    def head_slope(g):
        return jnp.where(n == 0, _alibi_slope(n_heads, g), _alibi_slope(n_heads, G_B + g)) * LOG2E

    def q_map(g, c):
        qh = q_ref[:, g * LANES:(g + 1) * LANES]
        return jnp.where((lane < D_B) if c == 0 else (lane >= D_B), qh, 0.0)

    def causal_keep():
        return lax.broadcasted_iota(jnp.int32, (B, B), 1) <= lax.broadcasted_iota(jnp.int32, (B, B), 0)

    def finish(l_of):
        lam = _lam_value(lq1_ref[...], lk1_ref[...], lq2_ref[...], lk2_ref[...], lam_init)
        for g in range(G_B):
            o_ref[:, g * LANES:(g + 1) * LANES] = (acc_scr[2 * g] / l_of(2 * g)
                                                   - lam * (acc_scr[2 * g + 1] / l_of(2 * g + 1)))

    @pl.when(jnp.logical_and(fast, j == 0))
    def _():
        kmax = math.sqrt(D_B) * jnp.max(jnp.abs(kg_ref[...]), axis=-1, keepdims=True)
        rloc = lax.broadcasted_iota(jnp.int32, (B, 1), 0).astype(F32)
        lane1 = lax.broadcasted_iota(jnp.int32, (1, LANES), 1)
        for g in range(G_B):
            slope = head_slope(g)
            c_n0 = _split3_const(_alibi_slope(n_heads, g) * LOG2E)
            c_n1 = _split3_const(_alibi_slope(n_heads, G_B + g) * LOG2E)
            crow = jnp.zeros((1, LANES), F32)
            for t in range(3):
                st = jnp.where(n == 0, c_n0[t], c_n1[t])
                crow = jnp.where(lane1 == t, st * LANES, crow)
                crow = jnp.where(lane1 == 3 + t, st, crow)
            for c in range(2):
                qm = q_map(g, c)
                qn = jnp.sqrt(jnp.sum(qm * qm, axis=1, keepdims=True))
                m1, m2, m3 = _split3(slope * rloc + qn * kmax)
                feat = jnp.where(lane == 6, -m1.astype(F32),
                                 jnp.where(lane == 7, -m2.astype(F32), jnp.where(lane == 8, -m3.astype(F32), crow)))
                qxt_scr[2 * g + c] = jnp.concatenate([qm.T, feat.T], axis=0).astype(BF16)
        acct_scr[...] = jnp.zeros_like(acct_scr)

    def tile_fast(masked):
        a = ((i - j) * (B // LANES)).astype(F32)
        ind = (lax.broadcasted_iota(jnp.int32, (1, LANES), 1) < 3).astype(F32)
        featk = (base_ref[...] - a * ind).astype(BF16)
        kx = jnp.concatenate([k_ref[...].astype(BF16), featk], axis=1)
        v1t = jnp.concatenate([v_ref[...].T.astype(BF16), jnp.ones((ONES_ROWS, B), BF16)], axis=0)
        if masked:
            keep = lax.broadcasted_iota(jnp.int32, (B, B), 0) <= lax.broadcasted_iota(jnp.int32, (B, B), 1)
        for mi in range(2 * G_B):
            st = _dot(kx, qxt_scr[mi])
            if masked:
                st = jnp.where(keep, st, NEG_BIG)
            acct_scr[mi] = acct_scr[mi] + _dot(v1t, jnp.exp2(st).astype(BF16))

    @pl.when(jnp.logical_and(fast, j < i))
    def _():
        tile_fast(False)

    @pl.when(jnp.logical_and(fast, j == i))
    def _():
        tile_fast(True)
        lam = _lam_value(lq1_ref[...], lk1_ref[...], lq2_ref[...], lk2_ref[...], lam_init)
        for g in range(G_B):
            a0, a1 = acct_scr[2 * g], acct_scr[2 * g + 1]
            ot = a0[:LANES] / a0[LANES:LANES + 1] - lam * (a1[:LANES] / a1[LANES:LANES + 1])
            o_ref[:, g * LANES:(g + 1) * LANES] = ot.T

    slow = jnp.logical_not(fast)

    @pl.when(jnp.logical_and(slow, j == 0))
    def _():
        m_scr[...] = jnp.full_like(m_scr, NEG_BIG)
        l_scr[...] = jnp.zeros_like(l_scr)
        acc_scr[...] = jnp.zeros_like(acc_scr)

    def tile_slow(masked):
        k_bf = k_ref[...].astype(BF16)
        v_bf = v_ref[...].astype(BF16)
        col = lax.broadcasted_iota(jnp.int32, (1, B), 1).astype(F32)
        rel = col - ((i - j) * B).astype(F32)
        if masked:
            keep = causal_keep()
        for g in range(G_B):
            bias = head_slope(g) * rel
            for c in range(2):
                s = _dot_nt(q_map(g, c).astype(BF16), k_bf) + bias
                if masked:
                    s = jnp.where(keep, s, NEG_BIG)
                _online_update(g * 2 + c, s, v_bf, m_scr, l_scr, acc_scr)

    @pl.when(jnp.logical_and(slow, j < i))
    def _():
        tile_slow(False)

    @pl.when(jnp.logical_and(slow, j == i))
    def _():
        tile_slow(True)
        finish(lambda mi: l_scr[mi])


def _position_features(B):
    c = np.arange(B)
    f = np.zeros((B, LANES), np.float32)
    f[:, 0:3] = (c >> 7)[:, None]
    f[:, 3:6] = (c & 127)[:, None]
    f[:, 6:9] = 1.0
    return jnp.asarray(f)


def _diff_prompt(qb, kb, vb, k_gain, fast_flag, lq1, lk1, lq2, lk2, lam_init, B):
    T = qb.shape[0]
    n_heads = qb.shape[1] // LANES
    nb = T // B
    assert T % B == 0 and n_heads == KVH_B * G_B and B % LANES == 0 and (nb * B) // LANES <= 256
    qi = np.concatenate([np.full(i + 1, i, np.int32) for i in range(nb)])
    kj = np.concatenate([np.arange(i + 1, dtype=np.int32) for i in range(nb)])
    base = _position_features(B)
    const = lambda shape: pl.BlockSpec(shape, lambda n, p, qi, kj, fl: (0,) * len(shape))
    grid_spec = pltpu.PrefetchScalarGridSpec(
        num_scalar_prefetch=3,
        grid=(KVH_B, len(qi)),
        in_specs=[pl.BlockSpec((B, G_B * LANES), lambda n, p, qi, kj, fl: (qi[p], n)),
                  pl.BlockSpec((B, LANES), lambda n, p, qi, kj, fl: (kj[p], n)),
                  pl.BlockSpec((B, LANES), lambda n, p, qi, kj, fl: (kj[p], n)),
                  const(k_gain.shape), const(base.shape),
                  const(lq1.shape), const(lk1.shape), const(lq2.shape), const(lk2.shape)],
        out_specs=pl.BlockSpec((B, G_B * LANES), lambda n, p, qi, kj, fl: (qi[p], n)),
        scratch_shapes=[pltpu.VMEM((2 * G_B, 2 * LANES, B), BF16),
                        pltpu.VMEM((2 * G_B, LANES + ONES_ROWS, B), F32)] + [pltpu.VMEM((2 * G_B, B, LANES), F32)] * 3,
    )
    return pl.pallas_call(
        functools.partial(_diff_prompt_kernel, B=B, n_heads=n_heads, lam_init=lam_init),
        out_shape=jax.ShapeDtypeStruct(qb.shape, F32),
        grid_spec=grid_spec,
        compiler_params=_cparams(("parallel", "arbitrary")),
        name="diff_prompt",
    )(jnp.asarray(qi), jnp.asarray(kj), fast_flag, qb, kb, vb, k_gain, base, lq1, lk1, lq2, lk2)


def _diff_sample_kernel(pt_ref, flag_ref, q_ref, kn_ref, vn_ref, kg_ref, lq1_ref, lk1_ref, lq2_ref, lk2_ref,
                        ck_ref, cv_ref, o_ref, kbuf, vbuf, ksem, vsem, m_scr, l_scr, acc_scr,
                        *, P, page, n_chunks, n_seq, T, n_heads, lam_init):
    b = pl.program_id(0)
    c = pl.program_id(1)
    step = b * n_chunks + c
    slot = lax.rem(step, 2)
    past = n_chunks * P * page
    R = G_B * 2 * T
    fast = flag_ref[0] != 0

    def k_copy(pg, sl, p):
        return pltpu.make_async_copy(ck_ref.at[pg], kbuf.at[sl, p], ksem.at[sl])

    def v_copy(pg, sl, p):
        return pltpu.make_async_copy(cv_ref.at[pg], vbuf.at[sl, p], vsem.at[sl])

    def issue(bb, cc, sl):
        for p in range(P):
            pg = pt_ref[bb, cc * P + p]
            k_copy(pg, sl, p).start()
            v_copy(pg, sl, p).start()

    @pl.when(step == 0)
    def _():
        issue(0, 0, 0)

    nxt = step + 1

    @pl.when(nxt < n_seq * n_chunks)
    def _():
        issue(nxt // n_chunks, lax.rem(nxt, n_chunks), 1 - slot)

    for p in range(P):
        k_copy(0, slot, p).wait()
        v_copy(0, slot, p).wait()

    lane = lax.broadcasted_iota(jnp.int32, (T, LANES), 1)
    rrow = lax.broadcasted_iota(jnp.int32, (R, 1), 0)
    trow1 = lax.rem(rrow, T).astype(F32)
    col = lax.broadcasted_iota(jnp.int32, (1, P * page), 1)
    rel = (col + (c * (P * page) - past)).astype(F32)

    def lhs_and_slopes(n):
        blocks = []
        for g in range(G_B):
            qh = q_ref[:, (n * G_B + g) * LANES:(n * G_B + g + 1) * LANES]
            for cs in range(2):
                half = (lane < D_B) if cs == 0 else (lane >= D_B)
                blocks.append(jnp.where(half, qh, 0.0))
        lhs = jnp.concatenate(blocks, axis=0)
        slopes = jnp.where(rrow < 2 * T, _alibi_slope(n_heads, n * G_B), _alibi_slope(n_heads, n * G_B + 1)) * LOG2E
        return lhs, slopes

    def cached_kv(n):
        kt = jnp.concatenate([kbuf[slot, p, n * LANES:(n + 1) * LANES, :] for p in range(P)], axis=1).astype(BF16)
        vv = jnp.concatenate([vbuf[slot, p, pl.ds(n, page, stride=KVH_B), :] for p in range(P)], axis=0).astype(BF16)
        return kt, vv

    def new_kv(n):
        zpad = jnp.zeros((LANES - T, LANES), F32)
        kn = jnp.concatenate([kn_ref[:, n * LANES:(n + 1) * LANES], zpad], axis=0).astype(BF16)
        vn = jnp.concatenate([vn_ref[:, n * LANES:(n + 1) * LANES], zpad], axis=0).astype(BF16)
        ncol = lax.broadcasted_iota(jnp.int32, (R, LANES), 1)
        keep = ncol <= lax.rem(lax.broadcasted_iota(jnp.int32, (R, LANES), 0), T)
        return kn, vn, ncol.astype(F32), keep

    def write_out(n, res, lam):
        for g in range(G_B):
            r0 = g * 2 * T
            o_ref[:, (n * G_B + g) * LANES:(n * G_B + g + 1) * LANES] = res[r0:r0 + T] - lam * res[r0 + T:r0 + 2 * T]

    def run(use_bound):
        @pl.when(c == 0)
        def _():
            if not use_bound:
                m_scr[...] = jnp.full_like(m_scr, NEG_BIG)
            l_scr[...] = jnp.zeros_like(l_scr)
            acc_scr[...] = jnp.zeros_like(acc_scr)

        kmax = math.sqrt(D_B) * jnp.max(jnp.abs(kg_ref[...]), axis=-1, keepdims=True)

        def row_shift(lhs, slopes):
            return slopes * trow1 + jnp.sqrt(jnp.sum(lhs * lhs, axis=1, keepdims=True)) * kmax

        def accumulate(n, s, vv):
            if use_bound:
                pr = jnp.exp2(s)
                l_scr[n] = l_scr[n] + _lane_tile_sum(pr)
                acc_scr[n] = acc_scr[n] + _dot(pr.astype(BF16), vv)
            else:
                _online_update(n, s, vv, m_scr, l_scr, acc_scr)

        for n in range(KVH_B):
            lhs, slopes = lhs_and_slopes(n)
            kt, vv = cached_kv(n)
            s = _dot(lhs.astype(BF16), kt) + slopes * rel
            if use_bound:
                s = s - row_shift(lhs, slopes)
            accumulate(n, s, vv)

        @pl.when(c == n_chunks - 1)
        def _():
            lam = _lam_value(lq1_ref[...], lk1_ref[...], lq2_ref[...], lk2_ref[...], lam_init)
            for n in range(KVH_B):
                lhs, slopes = lhs_and_slopes(n)
                kn, vn, ncol, keep = new_kv(n)
                s = _dot_nt(lhs.astype(BF16), kn) + slopes * ncol
                if use_bound:
                    s = s - row_shift(lhs, slopes)
                accumulate(n, jnp.where(keep, s, NEG_BIG), vn)
                den = jnp.sum(l_scr[n], axis=1, keepdims=True) if use_bound else l_scr[n]
                write_out(n, acc_scr[n] / den, lam)

    @pl.when(fast)
    def _():
        run(True)

    @pl.when(jnp.logical_not(fast))
    def _():
        run(False)


def _diff_sample(qb, kb_new, vb_new, page_table, kt_pages, v_pages, k_gain, fast_flag, lq1, lk1, lq2, lk2, lam_init,
                 T, P):
    n_seq, n_pages = page_table.shape
    page = kt_pages.shape[2]
    n_heads = qb.shape[1] // LANES
    assert n_pages % P == 0 and qb.shape[0] == n_seq * T and page == LANES
    n_chunks = n_pages // P
    R = G_B * 2 * T
    small = pl.BlockSpec(lq1.shape, lambda b, c, pt, fl: (0, 0))
    rows = lambda w: pl.BlockSpec((T, w), lambda b, c, pt, fl: (b, 0))
    grid_spec = pltpu.PrefetchScalarGridSpec(
        num_scalar_prefetch=2,
        grid=(n_seq, n_chunks),
        in_specs=[rows(qb.shape[1]), rows(kb_new.shape[1]), rows(vb_new.shape[1]), small, small, small, small, small,
                  pl.BlockSpec(memory_space=pl.ANY), pl.BlockSpec(memory_space=pl.ANY)],
        out_specs=rows(qb.shape[1]),
        scratch_shapes=[pltpu.VMEM((2, P) + kt_pages.shape[1:], F32),
                        pltpu.VMEM((2, P) + v_pages.shape[1:], F32),
                        pltpu.SemaphoreType.DMA((2,)), pltpu.SemaphoreType.DMA((2,)),
                        pltpu.VMEM((KVH_B, R, LANES), F32), pltpu.VMEM((KVH_B, R, LANES), F32),
                        pltpu.VMEM((KVH_B, R, LANES), F32)],
    )
    return pl.pallas_call(
        functools.partial(_diff_sample_kernel, P=P, page=page, n_chunks=n_chunks, n_seq=n_seq, T=T,
                          n_heads=n_heads, lam_init=lam_init),
        out_shape=jax.ShapeDtypeStruct(qb.shape, F32),
        grid_spec=grid_spec,
        compiler_params=_cparams(("arbitrary", "arbitrary")),
        name="diff_sample",
    )(page_table, fast_flag, qb, kb_new, vb_new, k_gain, lq1, lk1, lq2, lk2, kt_pages, v_pages)


def _outproj_a_kernel(x_ref, oa_ref, gsa_ref, ob_ref, gsb_ref, og_ref, sg_ref, w_ref, y_ref, *, lam_init):
    def branch(o_ref, gate_ref, gain, scale):
        parts = []
        for h in range(o_ref.shape[1] // LANES):
            sl = slice(h * LANES, (h + 1) * LANES)
            parts.append(_rms_rows(o_ref[:, sl], gain) * scale * gate_ref[:, sl])
        return jnp.concatenate(parts, axis=1).astype(BF16)

    ya = branch(oa_ref, gsa_ref, og_ref[...], 1.0)
    yb = branch(ob_ref, gsb_ref, sg_ref[...], 1.0 - lam_init)
    wa = oa_ref.shape[1]
    y_ref[...] = x_ref[...] + _dot(ya, w_ref[0:wa, :]) + _dot(yb, w_ref[wa:, :])


def _outproj_a(x, oa, gsa, ob, gsb, o_gain, subln, w_bf16, lam_init, tm):
    T, D = x.shape
    row = lambda w: pl.BlockSpec((tm, w), lambda i: (i, 0))
    return pl.pallas_call(
        functools.partial(_outproj_a_kernel, lam_init=lam_init),
        out_shape=jax.ShapeDtypeStruct((T, D), F32),
        grid=(T // tm,),
        in_specs=[row(D), row(oa.shape[1]), row(oa.shape[1]), row(ob.shape[1]), row(ob.shape[1]),
                  _full(o_gain.shape), _full(subln.shape), _full(w_bf16.shape)],
        out_specs=row(D),
        compiler_params=_cparams(("parallel",)),
        name="outproj_a",
    )(x, oa, gsa, ob, gsb, o_gain, subln, w_bf16)


def _inproj_c_kernel(x_ref, g_ref, w_ref, qg_ref, kg_ref, bd_ref, q_ref, k_ref, v_ref, gs_ref, *, wq, wkv):
    hb = _rms_rows(x_ref[...], g_ref[...]).astype(BF16)
    bd = bd_ref[...]
    q_ref[...] = _group_rms(_dot(hb, w_ref[:, 0:wq]), qg_ref[...], bd) * (HD_C ** -0.5 * LOG2E)
    k_ref[...] = _group_rms(_dot(hb, w_ref[:, wq:wq + wkv]), kg_ref[...], bd)
    v_ref[...] = _dot(hb, w_ref[:, wq + wkv:wq + 2 * wkv])
    gs_ref[...] = _silu(_dot(hb, w_ref[:, wq + 2 * wkv:]))


def _inproj_c(x, g, w_bf16, q_gain_t, k_gain_t, bd, tm):
    T, D = x.shape
    wq, wkv = q_gain_t.shape[1], k_gain_t.shape[1]
    widths = (wq, wkv, wkv, wq)
    row = lambda w: pl.BlockSpec((tm, w), lambda i: (i, 0))
    return pl.pallas_call(
        functools.partial(_inproj_c_kernel, wq=wq, wkv=wkv),
        out_shape=[jax.ShapeDtypeStruct((T, w), F32) for w in widths],
        grid=(T // tm,),
        in_specs=[row(D), _full(g.shape), _full(w_bf16.shape), _full(q_gain_t.shape), _full(k_gain_t.shape),
                  _full(bd.shape)],
        out_specs=[row(w) for w in widths],
        compiler_params=_cparams(("parallel",)),
        name="inproj_c",
    )(x, g, w_bf16, q_gain_t, k_gain_t, bd)


def _swa_softmax_pv(s, sink, v_list):
    m = sink
    for piece in s:
        m = jnp.maximum(m, jnp.max(piece, axis=1, keepdims=True))
    den = jnp.exp2(sink - m)
    out = None
    for piece, pv in zip(s, v_list):
        p = jnp.exp2(piece - m)
        den = den + jnp.sum(p, axis=1, keepdims=True)
        term = pv(p.astype(BF16))
        out = term if out is None else out + term
    return out / den


def _swa_prompt_kernel(flag_ref, q_ref, kc_ref, kp_ref, vc_ref, vp_ref, snk_ref, qg_ref, kg_ref, o_ref, *, n_heads):
    i = pl.program_id(0)
    W = q_ref.shape[0]
    n_pairs = n_heads // KVH_C
    half_rows = n_pairs * W
    lane = lax.broadcasted_iota(jnp.int32, (W, LANES), 1)
    blocks = []
    for n in range(KVH_C):
        half = (lane < HD_C) if n == 0 else (lane >= HD_C)
        for m in range(n_pairs):
            blocks.append(jnp.where(half, q_ref[:, m * LANES:(m + 1) * LANES], 0.0))
    lhs = jnp.concatenate(blocks, axis=0).astype(BF16)
    keys = jnp.concatenate([kp_ref[...], kc_ref[...]], axis=0).astype(BF16)
    vals = jnp.concatenate([vp_ref[...], vc_ref[...]], axis=0).astype(BF16)
    r = lax.broadcasted_iota(jnp.int32, (W, 2 * W), 0)
    c = lax.broadcasted_iota(jnp.int32, (W, 2 * W), 1)
    dist = r + W - c
    valid = jnp.logical_and(jnp.logical_and(dist >= 0, dist <= W), jnp.logical_or(c >= W, i > 0))
    neg_dist = jnp.where(valid, -dist.astype(F32), NEG_BIG)

    def write(res):
        for m in range(n_pairs):
            lo = res[m * W:(m + 1) * W]
            hi = res[half_rows + m * W:half_rows + (m + 1) * W]
            o_ref[:, m * LANES:(m + 1) * LANES] = jnp.where(lane < HD_C, lo, hi)

    @pl.when(flag_ref[0] != 0)
    def _():
        bound = ((LOG2E * math.sqrt(HD_C)) * jnp.max(jnp.abs(qg_ref[...]), axis=-1, keepdims=True)
                 * jnp.max(jnp.abs(kg_ref[...]), axis=-1, keepdims=True))
        biases, sink_terms = [], []
        for h in range(n_heads):
            sink_h = snk_ref[:, h:h + 1] * LOG2E
            shift = jnp.maximum(bound, sink_h)
            biases.append((_alibi_slope(n_heads, h) * LOG2E) * neg_dist - shift)
            sink_terms.append(jnp.broadcast_to(jnp.exp2(sink_h - shift), (W, LANES)))
        pr = jnp.exp2(_dot_nt(lhs, keys) + jnp.concatenate(biases, axis=0))
        vals1 = jnp.concatenate([vals, jnp.ones_like(vals)], axis=1)
        res = _dot(pr.astype(BF16), vals1)
        den = res[:, LANES:] + jnp.concatenate(sink_terms, axis=0)
        write(res[:, :LANES] / den)

    @pl.when(flag_ref[0] == 0)
    def _():
        slopes, sinks = [], []
        for h in range(n_heads):
            slopes.append(jnp.full((W, 1), _alibi_slope(n_heads, h) * LOG2E, F32))
            sinks.append(jnp.broadcast_to(snk_ref[:, h:h + 1] * LOG2E, (W, 1)))
        slope = jnp.concatenate(slopes, axis=0)
        sink = jnp.concatenate(sinks, axis=0)
        s = _dot_nt(lhs, keys) + slope * jnp.tile(neg_dist, (n_heads, 1))
        m_row = jnp.maximum(jnp.max(s, axis=1, keepdims=True), sink)
        pr = jnp.exp2(s - m_row)
        den = jnp.sum(pr, axis=1, keepdims=True) + jnp.exp2(sink - m_row)
        write(_dot(pr.astype(BF16), vals) / den)


def _swa_prompt(q, k, v, sinks, q_gain, k_gain, fast_flag):
    T, Wq = q.shape
    n_heads = Wq // HD_C
    nb = T // WINDOW
    cur = lambda w: pl.BlockSpec((WINDOW, w), lambda i, fl: (i, 0))
    prev = lambda w: pl.BlockSpec((WINDOW, w), lambda i, fl: (jnp.maximum(i - 1, 0), 0))
    const = lambda a: pl.BlockSpec(a.shape, lambda i, fl: (0,) * a.ndim)
    kw = k.shape[1]
    grid_spec = pltpu.PrefetchScalarGridSpec(
        num_scalar_prefetch=1,
        grid=(nb,),
        in_specs=[cur(Wq), cur(kw), prev(kw), cur(kw), prev(kw), const(sinks), const(q_gain), const(k_gain)],
        out_specs=cur(Wq),
    )
    return pl.pallas_call(
        functools.partial(_swa_prompt_kernel, n_heads=n_heads),
        out_shape=jax.ShapeDtypeStruct(q.shape, F32),
        grid_spec=grid_spec,
        compiler_params=_cparams(("parallel",)),
        name="swa_prompt",
    )(fast_flag, q, k, k, v, v, sinks, q_gain, k_gain)


def _swa_sample_kernel(q_ref, kn_ref, vn_ref, ckt_ref, cvt_ref, snk_ref, o_ref, *, T, n_heads, n_sub):
    n_pairs = n_heads // KVH_C
    R = n_heads * T
    lane = lax.broadcasted_iota(jnp.int32, (T, LANES), 1)
    rr = lax.broadcasted_iota(jnp.int32, (R, WINDOW), 0)
    cc = lax.broadcasted_iota(jnp.int32, (R, WINDOW), 1)
    t = lax.rem(rr, T)
    hrow = lax.broadcasted_iota(jnp.int32, (R, 1), 0) // T
    slope = jnp.exp2(-8.0 * (hrow + 1).astype(F32) / n_heads) * LOG2E
    sink = jnp.zeros((R, 1), F32)
    for h in range(n_heads):
        sink = jnp.where(hrow == h, snk_ref[:, h:h + 1], sink)
    sink = sink * LOG2E
    bias_c = jnp.where(cc >= t, -slope * (t + WINDOW - cc).astype(F32), NEG_BIG)
    bias_n = jnp.where(cc <= t, -slope * (t - cc).astype(F32), NEG_BIG)
    zpad = jnp.zeros((WINDOW - T, LANES), F32)
    half_rows = n_pairs * T
    for sq in range(n_sub):
        rows = slice(sq * T, (sq + 1) * T)
        blocks = []
        for n in range(KVH_C):
            half = (lane < HD_C) if n == 0 else (lane >= HD_C)
            for m in range(n_pairs):
                blocks.append(jnp.where(half, q_ref[rows, m * LANES:(m + 1) * LANES], 0.0))
        lhs = jnp.concatenate(blocks, axis=0).astype(BF16)
        kn = jnp.concatenate([kn_ref[rows, :], zpad], axis=0).astype(BF16)
        vn = jnp.concatenate([vn_ref[rows, :], zpad], axis=0).astype(BF16)
        ckt = ckt_ref[sq].astype(BF16)
        cvt = cvt_ref[sq].astype(BF16)
        s_c = _dot(lhs, ckt) + bias_c
        s_n = _dot_nt(lhs, kn) + bias_n
        res = _swa_softmax_pv([s_c, s_n], sink, [lambda p: _dot_nt(p, cvt), lambda p: _dot(p, vn)])
        for m in range(n_pairs):
            lo = res[m * T:(m + 1) * T]
            hi = res[half_rows + m * T:half_rows + (m + 1) * T]
            o_ref[rows, m * LANES:(m + 1) * LANES] = jnp.where(lane < HD_C, lo, hi)


def _swa_sample(q, k_new, v_new, ckt, cvt, sinks, T, n_sub):
    BT, Wq = q.shape
    B = ckt.shape[0]
    n_heads = Wq // HD_C
    assert B % n_sub == 0
    rows = lambda w: pl.BlockSpec((n_sub * T, w), lambda b: (b, 0))
    cache = pl.BlockSpec((n_sub,) + ckt.shape[1:], lambda b: (b, 0, 0))
    return pl.pallas_call(
        functools.partial(_swa_sample_kernel, T=T, n_heads=n_heads, n_sub=n_sub),
        out_shape=jax.ShapeDtypeStruct(q.shape, F32),
        grid=(B // n_sub,),
        in_specs=[rows(Wq), rows(k_new.shape[1]), rows(v_new.shape[1]), cache, cache, _full(sinks.shape)],
        out_specs=rows(Wq),
        compiler_params=_cparams(("parallel",)),
        name="swa_sample",
    )(q, k_new, v_new, ckt, cvt, sinks)


def _outproj_c_kernel(x_ref, o_ref, gs_ref, w_ref, y_ref):
    y_ref[...] = x_ref[...] + _dot((o_ref[...] * gs_ref[...]).astype(BF16), w_ref[...])


def _outproj_c(x, o, gs, w_bf16, tm):
    T, D = x.shape
    row = lambda w: pl.BlockSpec((tm, w), lambda i: (i, 0))
    return pl.pallas_call(
        _outproj_c_kernel,
        out_shape=jax.ShapeDtypeStruct((T, D), F32),
        grid=(T // tm,),
        in_specs=[row(D), row(o.shape[1]), row(o.shape[1]), _full(w_bf16.shape)],
        out_specs=row(D),
        compiler_params=_cparams(("parallel",)),
        name="outproj_c",
    )(x, o, gs, w_bf16)


TOKEN_TILE = 512
GLA_CHUNK = 128
DIFF_BLOCK = 1024
DIFF_PAGES_PER_STEP = 32
SWA_SEQS_PER_STEP = 8
GLA_SEQS_PER_STEP = 4
GLA_HEADS_PER_STEP = 4


def _pair_perm(n_heads):
    n_pairs = n_heads // KVH_C
    cols = []
    for m in range(n_pairs):
        for n in range(KVH_C):
            h = n * n_pairs + m
            cols.extend(range(h * HD_C, (h + 1) * HD_C))
    return np.asarray(cols, np.int32)


def kernel(x_prompt, x_sample, state_hgrn, cache_k_diff, cache_v_diff, cache_k_swa, cache_v_swa, page_table, norm_a, w_in_a, w_out_a, lb_logits, hgrn_out_gain, diff_q_gain, diff_k_gain, diff_subln_gain, lam_q1, lam_k1, lam_q2, lam_k2, norm_c, w_in_c, w_out_c, swa_q_gain, swa_k_gain, sinks):
    batch, seq, d_model = x_prompt.shape
    n_seq, t_dec, _ = x_sample.shape
    assert batch == 1
    n_even, n_odd = norm_a.shape[0], norm_c.shape[0]
    depth = n_even + n_odd
    h_a = state_hgrn.shape[2]
    n_pool, page = cache_k_diff.shape[1], cache_k_diff.shape[2]
    w_b = w_out_a.shape[1] - h_a * DV_A
    h_b = w_b // (2 * D_B)
    h_c = sinks.shape[1]

    xp = x_prompt.reshape(seq, d_model)
    xs = x_sample.reshape(n_seq * t_dec, d_model)
    tile_p, tile_s = min(TOKEN_TILE, seq), min(TOKEN_TILE, n_seq * t_dec)
    bd64 = _block_diag_mean(2 * LANES, D_B)
    bd64_c = _block_diag_mean(LANES, HD_C)
    perm = _pair_perm(h_c)

    hs_p, hs_s, kdp, vdp, kds, vds, ksp, vsp, kss, vss = ([] for _ in range(10))
    for l in range(depth):
        if l % 2 == 0:
            e = l // 2
            lam_init = 0.8 - 0.6 * math.exp(-0.3 * l)
            w_in = w_in_a[e].astype(BF16)
            w_out = w_out_a[e].astype(BF16)
            g = norm_a[e:e + 1]
            qg = jnp.tile(diff_q_gain[e:e + 1], (1, h_b * 2))
            kg = jnp.tile(diff_k_gain[e:e + 1], (1, KVH_B * 2))
            lam_args = (lam_q1[e:e + 1], lam_k1[e:e + 1], lam_q2[e:e + 1], lam_k2[e:e + 1])
            kt_pages = jnp.transpose(cache_k_diff[e], (0, 2, 3, 4, 1)).reshape(n_pool, KVH_B * 2 * D_B, page)
            v_pages = cache_v_diff[e].reshape(n_pool, page * KVH_B, 2 * D_B)

            pa = _inproj_a(xp, g, w_in, lb_logits, qg, kg, bd64, e, tile_p)
            sa = _inproj_a(xs, g, w_in, lb_logits, qg, kg, bd64, e, tile_s)
            qa_p, lf_p, kk_p, ia_p, gsa_p, qb_p, kb_p, vb_p, gsb_p = pa
            qa_s, lf_s, kk_s, ia_s, gsa_s, qb_s, kb_s, vb_s, gsb_s = sa

            lb_min = jnp.min(jnp.cumsum(jax.nn.softmax(lb_logits.astype(F32), axis=0), axis=0)[e])
            gla_flag = (-GLA_SUB * jnp.log(lb_min) <= GLA_FAST_DECAY).astype(jnp.int32).reshape(1)
            oa_p, st_p = _gla_prompt(qa_p, kk_p, ia_p, lf_p, gla_flag, GLA_CHUNK, GLA_HEADS_PER_STEP)
            oa_s, st_s = _gla_sample(qa_s, kk_s, ia_s, lf_s, state_hgrn[e], t_dec, GLA_SEQS_PER_STEP)
            logit_bound = (LOG2E * math.sqrt(D_B)) * jnp.max(jnp.abs(diff_q_gain[e])) * jnp.max(jnp.abs(diff_k_gain[e]))
            fast_flag = (logit_bound <= FAST_LOGIT_BOUND).astype(jnp.int32).reshape(1)
            k_gain = diff_k_gain[e:e + 1]
            ob_p = _diff_prompt(qb_p, kb_p, vb_p, k_gain, fast_flag, *lam_args, lam_init, DIFF_BLOCK)
            ob_s = _diff_sample(qb_s, kb_s, vb_s, page_table, kt_pages, v_pages, k_gain, fast_flag, *lam_args,
                                lam_init, t_dec, DIFF_PAGES_PER_STEP)

            og, sg = hgrn_out_gain[e:e + 1], diff_subln_gain[e:e + 1]
            xp = _outproj_a(xp, oa_p, gsa_p, ob_p, gsb_p, og, sg, w_out, lam_init, tile_p)
            xs = _outproj_a(xs, oa_s, gsa_s, ob_s, gsb_s, og, sg, w_out, lam_init, tile_s)

            hs_p.append(st_p[None])
            hs_s.append(st_s)
            kdp.append(kb_p.reshape(batch, seq, KVH_B, 2, D_B))
            vdp.append(vb_p.reshape(batch, seq, KVH_B, 2 * D_B))
            kds.append(kb_s.reshape(n_seq, t_dec, KVH_B, 2, D_B))
            vds.append(vb_s.reshape(n_seq, t_dec, KVH_B, 2 * D_B))
        else:
            o = l // 2
            wq = h_c * HD_C
            wkv = KVH_C * HD_C
            w_full = w_in_c[o]
            w_in = jnp.concatenate([w_full[:, :wq][:, perm], w_full[:, wq:wq + 2 * wkv],
                                    w_full[:, wq + 2 * wkv:][:, perm]], axis=1).astype(BF16)
            w_out = w_out_c[o][perm, :].astype(BF16)
            g = norm_c[o:o + 1]
            qg = jnp.tile(swa_q_gain[o:o + 1], (1, h_c))
            kg = jnp.tile(swa_k_gain[o:o + 1], (1, KVH_C))
            snk = sinks[o:o + 1]
            ckt = jnp.transpose(cache_k_swa[o], (0, 2, 3, 1)).reshape(n_seq, wkv, WINDOW)
            cvt = jnp.transpose(cache_v_swa[o], (0, 2, 3, 1)).reshape(n_seq, wkv, WINDOW)

            q_p, k_p, v_p, gs_p = _inproj_c(xp, g, w_in, qg, kg, bd64_c, tile_p)
            q_s, k_s, v_s, gs_s = _inproj_c(xs, g, w_in, qg, kg, bd64_c, tile_s)
            swa_bound = ((LOG2E * math.sqrt(HD_C)) * jnp.max(jnp.abs(swa_q_gain[o]))
                         * jnp.max(jnp.abs(swa_k_gain[o])))
            swa_flag = (swa_bound <= FAST_LOGIT_BOUND).astype(jnp.int32).reshape(1)
            o_p = _swa_prompt(q_p, k_p, v_p, snk, swa_q_gain[o:o + 1], swa_k_gain[o:o + 1], swa_flag)
            o_s = _swa_sample(q_s, k_s, v_s, ckt, cvt, snk, t_dec, SWA_SEQS_PER_STEP)
            xp = _outproj_c(xp, o_p, gs_p, w_out, tile_p)
            xs = _outproj_c(xs, o_s, gs_s, w_out, tile_s)

            ksp.append(k_p[-WINDOW:].reshape(batch, WINDOW, KVH_C, HD_C))
            vsp.append(v_p[-WINDOW:].reshape(batch, WINDOW, KVH_C, HD_C))
            k_new = k_s.reshape(n_seq, t_dec, KVH_C, HD_C)
            v_new = v_s.reshape(n_seq, t_dec, KVH_C, HD_C)
            kss.append(jnp.concatenate([cache_k_swa[o], k_new], axis=1)[:, -WINDOW:])
            vss.append(jnp.concatenate([cache_v_swa[o], v_new], axis=1)[:, -WINDOW:])
    return (xp.reshape(batch, seq, d_model), xs.reshape(n_seq, t_dec, d_model),
            jnp.stack(hs_p), jnp.stack(hs_s), jnp.stack(kdp), jnp.stack(vdp), jnp.stack(kds), jnp.stack(vds),
            jnp.stack(ksp), jnp.stack(vsp), jnp.stack(kss), jnp.stack(vss))
```

```python
import functools
import math

import jax
import jax.numpy as jnp
import ml_dtypes
import numpy as np
from jax import lax
from jax.experimental import pallas as pl
from jax.experimental.pallas import tpu as pltpu

F32 = jnp.float32
BF16 = jnp.bfloat16

EPS = 1e-6
LOG2E = 1.4426950408889634
NEG_BIG = -1e30
LANES = 128
VMEM_LIMIT_BYTES = 56 * 1024 * 1024

DK_A = 128
DV_A = 128
D_B = 64
KVH_B = 2
G_B = 2
HD_C = 64
KVH_C = 2
WINDOW = 128
GLA_SUB = 16
FAST_LOGIT_BOUND = 60.0
GLA_FAST_DECAY = 80.0
ONES_ROWS = 16


def _dot(a, b):
    return jnp.dot(a, b, preferred_element_type=F32)


def _dot_nt(a, b):
    return lax.dot_general(a, b, (((1,), (1,)), ((), ())), preferred_element_type=F32)


def _dot_tn(a, b):
    return lax.dot_general(a, b, (((0,), (0,)), ((), ())), preferred_element_type=F32)


def _silu(x):
    return x * jax.nn.sigmoid(x)


def _rms_rows(x, gain):
    return x * lax.rsqrt(jnp.mean(x * x, axis=-1, keepdims=True) + EPS) * gain


def _split3(x):
    x1 = x.astype(BF16)
    r1 = x - x1.astype(F32)
    x2 = r1.astype(BF16)
    x3 = (r1 - x2.astype(F32)).astype(BF16)
    return x1, x2, x3


def _group_meansq(x, bd):
    s = x * x
    hi = s.astype(BF16)
    lo = (s - hi.astype(F32)).astype(BF16)
    return _dot(hi, bd) + _dot(lo, bd)


def _group_rms(x, gain, bd):
    wb = bd.shape[0]
    outs = []
    for j in range(x.shape[1] // wb):
        part = x[:, j * wb:(j + 1) * wb]
        ms = _group_meansq(part, bd)
        outs.append(part * lax.rsqrt(ms + EPS) * gain[:, j * wb:(j + 1) * wb])
    return outs[0] if len(outs) == 1 else jnp.concatenate(outs, axis=1)


def _block_diag_mean(width, group):
    r = np.arange(width)[:, None] // group
    c = np.arange(width)[None, :] // group
    return jnp.asarray((r == c).astype(np.float32) / group, dtype=BF16)


def _lam_value(lq1, lk1, lq2, lk2, lam_init):
    a = jnp.sum(lq1 * lk1, axis=-1, keepdims=True)
    b = jnp.sum(lq2 * lk2, axis=-1, keepdims=True)
    return jnp.exp(a) - jnp.exp(b) + lam_init


def _split3_const(x):
    x = np.float32(x)
    a = np.float32(ml_dtypes.bfloat16(x))
    b = np.float32(ml_dtypes.bfloat16(np.float32(x - a)))
    c = np.float32(ml_dtypes.bfloat16(np.float32(np.float32(x - a) - b)))
    return float(a), float(b), float(c)


def _lane_tile_sum(p):
    out = p[:, 0:LANES]
    for t in range(1, p.shape[1] // LANES):
        out = out + p[:, t * LANES:(t + 1) * LANES]
    return out


def _cparams(sem):
    return pltpu.CompilerParams(dimension_semantics=sem, vmem_limit_bytes=VMEM_LIMIT_BYTES)


def _full(shape):
    nd = len(shape)
    return pl.BlockSpec(shape, lambda *_: (0,) * nd)


def _inproj_a_kernel(x_ref, g_ref, w_ref, lbl_ref, qg_ref, kg_ref, bd_ref,
                     qa_ref, lf_ref, kk_ref, ia_ref, gsa_ref, qb_ref, kb_ref, vb_ref, gsb_ref, *, e, wa, wb, wkv):
    hb = _rms_rows(x_ref[...], g_ref[...]).astype(BF16)
    bd = bd_ref[...]
    o = [0]

    def seg(width):
        z = _dot(hb, w_ref[:, o[0]:o[0] + width])
        o[0] += width
        return z

    qa = seg(wa)
    qa_ref[...] = _silu(qa) * (DK_A ** -0.5)
    lbl = lbl_ref[...]
    ex = jnp.exp(lbl - jnp.max(lbl, axis=0, keepdims=True))
    sm = ex / jnp.sum(ex, axis=0, keepdims=True)
    lb = jnp.sum(sm[0:e + 1, :], axis=0, keepdims=True)
    f = lb + (1.0 - lb) * jax.nn.sigmoid(seg(wa))
    lf_ref[...] = jnp.log(f)
    kk_ref[...] = 1.0 - f
    ia_ref[...] = seg(wa)
    gsa_ref[...] = _silu(seg(wa))
    qb_ref[...] = _group_rms(seg(wb), qg_ref[...], bd) * (D_B ** -0.5 * LOG2E)
    kb_ref[...] = _group_rms(seg(wkv), kg_ref[...], bd)
    vb_ref[...] = seg(wkv)
    gsb_ref[...] = _silu(seg(wb))


def _inproj_a(x, g, w_bf16, lb_logits, q_gain_t, k_gain_t, bd, e, tm):
    T, D = x.shape
    wa, wb, wkv = lb_logits.shape[1], q_gain_t.shape[1], k_gain_t.shape[1]
    n_in = w_bf16.shape[1]
    assert n_in == 4 * wa + 2 * wb + 2 * wkv and T % tm == 0
    widths = (wa, wa, wa, wa, wa, wb, wkv, wkv, wb)
    row = lambda w: pl.BlockSpec((tm, w), lambda i: (i, 0))
    return pl.pallas_call(
        functools.partial(_inproj_a_kernel, e=e, wa=wa, wb=wb, wkv=wkv),
        out_shape=[jax.ShapeDtypeStruct((T, w), F32) for w in widths],
        grid=(T // tm,),
        in_specs=[row(D), _full(g.shape), _full(w_bf16.shape), _full(lb_logits.shape),
                  _full(q_gain_t.shape), _full(k_gain_t.shape), _full(bd.shape)],
        out_specs=[row(w) for w in widths],
        compiler_params=_cparams(("parallel",)),
        name="inproj_a",
    )(x, g, w_bf16, lb_logits, q_gain_t, k_gain_t, bd)


def _gla_intra(q, k, b, bx, v_bf, C, c):
    n = C // c
    lane = lax.broadcasted_iota(jnp.int32, (c, C), 1)
    trow = lax.broadcasted_iota(jnp.int32, (c, LANES), 0)
    ones = jnp.ones((LANES, LANES), BF16)
    rows = []
    for i in range(n):
        sl = slice(i * c, (i + 1) * c)
        qi, ki, bi = q[sl], k[sl], b[sl]
        if i > 0:
            b0 = bx[i * c:i * c + 1, :]
            ks = k * jnp.exp(jnp.minimum(b0 - b, 0.0))
            qh = qi * jnp.exp(bi - b0)
            a_i = jnp.where(lane < i * c, _dot_nt(qh.astype(BF16), ks.astype(BF16)), 0.0)
        else:
            a_i = jnp.zeros((c, C), F32)
        pieces = []
        for s in range(c):
            d = jnp.minimum(bi - bi[s:s + 1, :], 0.0)
            pieces.append(jnp.where(trow >= s, qi * ki[s:s + 1, :] * jnp.exp(d), 0.0))
        r = _dot(jnp.concatenate(pieces, axis=0).astype(BF16), ones)
        for s in range(c):
            a_i = jnp.where(lane == i * c + s, r[s * c:(s + 1) * c, :C], a_i)
        rows.append(a_i)
    a = rows[0] if n == 1 else jnp.concatenate(rows, axis=0)
    return _dot(a.astype(BF16), v_bf)


def _gla_intra_bounded(q, k, b, bx, v_bf, C, c):
    lane = lax.broadcasted_iota(jnp.int32, (c, C), 1)
    trow = lax.broadcasted_iota(jnp.int32, (c, C), 0)
    rows = []
    for i in range(C // c):
        sl = slice(i * c, (i + 1) * c)
        b0 = bx[i * c:i * c + 1, :]
        ks = k * jnp.exp(jnp.minimum(b0 - b, GLA_FAST_DECAY))
        qh = q[sl] * jnp.exp(b[sl] - b0)
        rows.append(jnp.where(lane <= trow + i * c, _dot_nt(qh.astype(BF16), ks.astype(BF16)), 0.0))
    return _dot(jnp.concatenate(rows, axis=0).astype(BF16), v_bf)


def _gla_prompt_kernel(flag_ref, q_ref, k_ref, v_ref, lf_ref, o_ref, s_ref, st_scr, *, C, c, hps):
    j = pl.program_id(1)

    @pl.when(j == 0)
    def _():
        st_scr[...] = jnp.zeros_like(st_scr)

    def chunk(intra):
        rr = lax.broadcasted_iota(jnp.int32, (C, C), 0)
        cc = lax.broadcasted_iota(jnp.int32, (C, C), 1)
        tri = (cc <= rr).astype(BF16)
        for h in range(hps):
            sl = slice(h * DK_A, (h + 1) * DK_A)
            q, k, v, lf = q_ref[:, sl], k_ref[:, sl], v_ref[:, sl], lf_ref[:, sl]
            l1, l2, l3 = _split3(lf)
            b = _dot(tri, l1) + _dot(tri, l2) + _dot(tri, l3)
            st = st_scr[h]
            v_bf = v.astype(BF16)
            o = _dot_nt((q * jnp.exp(b)).astype(BF16), st.astype(BF16))
            o_ref[:, sl] = o + intra(q, k, b, b - lf, v_bf, C, c)
            b_end = b[C - 1:C, :]
            kt = k * jnp.exp(b_end - b)
            st_scr[h] = st * jnp.exp(b_end) + _dot_tn(v_bf, kt.astype(BF16))

    @pl.when(flag_ref[0] != 0)
    def _():
        chunk(_gla_intra_bounded)

    @pl.when(flag_ref[0] == 0)
    def _():
        chunk(_gla_intra)

    @pl.when(j == pl.num_programs(1) - 1)
    def _():
        for h in range(hps):
            s_ref[h] = st_scr[h].T


def _gla_prompt(q, k, v, lf, bounded_flag, C, hps):
    T, W = q.shape
    H = W // DK_A
    assert T % C == 0 and C % GLA_SUB == 0 and H % hps == 0
    blk = pl.BlockSpec((C, hps * DK_A), lambda h, j, fl: (j, h))
    grid_spec = pltpu.PrefetchScalarGridSpec(
        num_scalar_prefetch=1,
        grid=(H // hps, T // C),
        in_specs=[blk, blk, blk, blk],
        out_specs=[blk, pl.BlockSpec((hps, DK_A, DV_A), lambda h, j, fl: (h, 0, 0))],
        scratch_shapes=[pltpu.VMEM((hps, DV_A, DK_A), F32)],
    )
    return pl.pallas_call(
        functools.partial(_gla_prompt_kernel, C=C, c=GLA_SUB, hps=hps),
        out_shape=[jax.ShapeDtypeStruct((T, W), F32), jax.ShapeDtypeStruct((H, DK_A, DV_A), F32)],
        grid_spec=grid_spec,
        compiler_params=_cparams(("parallel", "arbitrary")),
        name="gla_prompt",
    )(bounded_flag, q, k, v, lf)


def _gla_sample_kernel(q_ref, k_ref, v_ref, lf_ref, s0_ref, o_ref, s_ref, *, T, H, n_sub):
    pad = LANES - T
    row = lax.broadcasted_iota(jnp.int32, (T, LANES), 0)
    zpad = jnp.zeros((pad, LANES), F32)
    for sq, h in [(sq, h) for sq in range(n_sub) for h in range(H)]:
        sl = slice(h * DK_A, (h + 1) * DK_A)
        rs = slice(sq * T, (sq + 1) * T)
        q, k, v, lf = q_ref[rs, sl], k_ref[rs, sl], v_ref[rs, sl], lf_ref[rs, sl]
        b = jnp.zeros((T, LANES), F32)
        for t in range(T):
            bt = jnp.sum(jnp.where(row <= t, lf, 0.0), axis=0, keepdims=True)
            b = jnp.where(row == t, bt, b)
        st = s0_ref[sq, h].T
        v_bf = jnp.concatenate([v, zpad], axis=0).astype(BF16)
        qd = jnp.concatenate([q * jnp.exp(b), zpad], axis=0).astype(BF16)
        o = _dot_nt(qd, st.astype(BF16))[:T]
        ones = jnp.ones((LANES, LANES), BF16)
        lane = lax.broadcasted_iota(jnp.int32, (T, LANES), 1)
        pieces = []
        for s in range(T):
            d = jnp.minimum(b - b[s:s + 1, :], 0.0)
            pieces.append(jnp.where(row >= s, q * k[s:s + 1, :] * jnp.exp(d), 0.0))
        r = _dot(jnp.concatenate(pieces, axis=0).astype(BF16), ones)
        a = jnp.zeros((T, LANES), F32)
        for s in range(T):
            a = jnp.where(lane == s, r[s * T:(s + 1) * T, :], a)
        a_bf = jnp.concatenate([a, zpad], axis=0).astype(BF16)
        o_ref[rs, sl] = o + _dot(a_bf, v_bf)[:T]
        b_end = b[T - 1:T, :]
        kt = jnp.concatenate([k * jnp.exp(b_end - b), zpad], axis=0).astype(BF16)
        st_new = st * jnp.exp(b_end) + _dot_tn(v_bf, kt)
        s_ref[sq, h] = st_new.T


def _gla_sample(q, k, v, lf, s0, T, n_sub):
    BT, W = q.shape
    B, H = s0.shape[0], s0.shape[1]
    assert BT == B * T and W == H * DK_A and T % 8 == 0 and T <= GLA_SUB and B % n_sub == 0
    blk = pl.BlockSpec((n_sub * T, W), lambda i: (i, 0))
    sblk = pl.BlockSpec((n_sub, H, DK_A, DV_A), lambda i: (i, 0, 0, 0))
    return pl.pallas_call(
        functools.partial(_gla_sample_kernel, T=T, H=H, n_sub=n_sub),
        out_shape=[jax.ShapeDtypeStruct((BT, W), F32), jax.ShapeDtypeStruct(s0.shape, F32)],
        grid=(B // n_sub,),
        in_specs=[blk, blk, blk, blk, sblk],
        out_specs=[blk, sblk],
        compiler_params=_cparams(("parallel",)),
        name="gla_sample",
    )(q, k, v, lf, s0)


def _alibi_slope(n_heads, h):
    return 2.0 ** (-8.0 * (h + 1) / n_heads)


def _online_update(mi, s, v_bf, m_scr, l_scr, acc_scr):
    reps = s.shape[1] // LANES
    m_prev = m_scr[mi]
    m_new = jnp.maximum(m_prev, jnp.max(s, axis=1, keepdims=True))
    alpha = jnp.exp2(m_prev - m_new)
    p = jnp.exp2(s - jnp.tile(m_new, (1, reps)))
    l_scr[mi] = alpha * l_scr[mi] + jnp.sum(p, axis=1, keepdims=True)
    acc_scr[mi] = alpha * acc_scr[mi] + _dot(p.astype(BF16), v_bf)
    m_scr[mi] = m_new


def _diff_prompt_kernel(qi_ref, kj_ref, flag_ref, q_ref, k_ref, v_ref, kg_ref, base_ref,
                        lq1_ref, lk1_ref, lq2_ref, lk2_ref, o_ref, qxt_scr, acct_scr, m_scr, l_scr, acc_scr,
                        *, B, n_heads, lam_init):
    n = pl.program_id(0)
    p = pl.program_id(1)
    i = qi_ref[p]
    j = kj_ref[p]
    fast = flag_ref[0] != 0
    lane = lax.broadcasted_iota(jnp.int32, (B, LANES), 1)

    def head_slope(g):
        return jnp.where(n == 0, _alibi_slope(n_heads, g), _alibi_slope(n_heads, G_B + g)) * LOG2E

    def q_map(g, c):
        qh = q_ref[:, g * LANES:(g + 1) * LANES]
        return jnp.where((lane < D_B) if c == 0 else (lane >= D_B), qh, 0.0)

    def causal_keep():
        return lax.broadcasted_iota(jnp.int32, (B, B), 1) <= lax.broadcasted_iota(jnp.int32, (B, B), 0)

    def finish(l_of):
        lam = _lam_value(lq1_ref[...], lk1_ref[...], lq2_ref[...], lk2_ref[...], lam_init)
        for g in range(G_B):
            o_ref[:, g * LANES:(g + 1) * LANES] = (acc_scr[2 * g] / l_of(2 * g)
                                                   - lam * (acc_scr[2 * g + 1] / l_of(2 * g + 1)))

    @pl.when(jnp.logical_and(fast, j == 0))
    def _():
        kmax = math.sqrt(D_B) * jnp.max(jnp.abs(kg_ref[...]), axis=-1, keepdims=True)
        rloc = lax.broadcasted_iota(jnp.int32, (B, 1), 0).astype(F32)
        lane1 = lax.broadcasted_iota(jnp.int32, (1, LANES), 1)
        for g in range(G_B):
            slope = head_slope(g)
            c_n0 = _split3_const(_alibi_slope(n_heads, g) * LOG2E)
            c_n1 = _split3_const(_alibi_slope(n_heads, G_B + g) * LOG2E)
            crow = jnp.zeros((1, LANES), F32)
            for t in range(3):
                st = jnp.where(n == 0, c_n0[t], c_n1[t])
                crow = jnp.where(lane1 == t, st * LANES, crow)
                crow = jnp.where(lane1 == 3 + t, st, crow)
            for c in range(2):
                qm = q_map(g, c)
                qn = jnp.sqrt(jnp.sum(qm * qm, axis=1, keepdims=True))
                m1, m2, m3 = _split3(slope * rloc + qn * kmax)
                feat = jnp.where(lane == 6, -m1.astype(F32),
                                 jnp.where(lane == 7, -m2.astype(F32), jnp.where(lane == 8, -m3.astype(F32), crow)))
                qxt_scr[2 * g + c] = jnp.concatenate([qm.T, feat.T], axis=0).astype(BF16)
        acct_scr[...] = jnp.zeros_like(acct_scr)

    def tile_fast(masked):
        a = ((i - j) * (B // LANES)).astype(F32)
        ind = (lax.broadcasted_iota(jnp.int32, (1, LANES), 1) < 3).astype(F32)
        featk = (base_ref[...] - a * ind).astype(BF16)
        kx = jnp.concatenate([k_ref[...].astype(BF16), featk], axis=1)
        v1t = jnp.concatenate([v_ref[...].T.astype(BF16), jnp.ones((ONES_ROWS, B), BF16)], axis=0)
        if masked:
            keep = lax.broadcasted_iota(jnp.int32, (B, B), 0) <= lax.broadcasted_iota(jnp.int32, (B, B), 1)
        for mi in range(2 * G_B):
            st = _dot(kx, qxt_scr[mi])
            if masked:
                st = jnp.where(keep, st, NEG_BIG)
            acct_scr[mi] = acct_scr[mi] + _dot(v1t, jnp.exp2(st).astype(BF16))

    @pl.when(jnp.logical_and(fast, j < i))
    def _():
        tile_fast(False)

    @pl.when(jnp.logical_and(fast, j == i))
    def _():
        tile_fast(True)
        lam = _lam_value(lq1_ref[...], lk1_ref[...], lq2_ref[...], lk2_ref[...], lam_init)
        for g in range(G_B):
            a0, a1 = acct_scr[2 * g], acct_scr[2 * g + 1]
            ot = a0[:LANES] / a0[LANES:LANES + 1] - lam * (a1[:LANES] / a1[LANES:LANES + 1])
            o_ref[:, g * LANES:(g + 1) * LANES] = ot.T

    slow = jnp.logical_not(fast)

    @pl.when(jnp.logical_and(slow, j == 0))
    def _():
        m_scr[...] = jnp.full_like(m_scr, NEG_BIG)
        l_scr[...] = jnp.zeros_like(l_scr)
        acc_scr[...] = jnp.zeros_like(acc_scr)

    def tile_slow(masked):
        k_bf = k_ref[...].astype(BF16)
        v_bf = v_ref[...].astype(BF16)
        col = lax.broadcasted_iota(jnp.int32, (1, B), 1).astype(F32)
        rel = col - ((i - j) * B).astype(F32)
        if masked:
            keep = causal_keep()
        for g in range(G_B):
            bias = head_slope(g) * rel
            for c in range(2):
                s = _dot_nt(q_map(g, c).astype(BF16), k_bf) + bias
                if masked:
                    s = jnp.where(keep, s, NEG_BIG)
                _online_update(g * 2 + c, s, v_bf, m_scr, l_scr, acc_scr)

    @pl.when(jnp.logical_and(slow, j < i))
    def _():
        tile_slow(False)

    @pl.when(jnp.logical_and(slow, j == i))
    def _():
        tile_slow(True)
        finish(lambda mi: l_scr[mi])


def _position_features(B):
    c = np.arange(B)
    f = np.zeros((B, LANES), np.float32)
    f[:, 0:3] = (c >> 7)[:, None]
    f[:, 3:6] = (c & 127)[:, None]
    f[:, 6:9] = 1.0
    return jnp.asarray(f)


def _diff_prompt(qb, kb, vb, k_gain, fast_flag, lq1, lk1, lq2, lk2, lam_init, B):
    T = qb.shape[0]
    n_heads = qb.shape[1] // LANES
    nb = T // B
    assert T % B == 0 and n_heads == KVH_B * G_B and B % LANES == 0 and (nb * B) // LANES <= 256
    qi = np.concatenate([np.full(i + 1, i, np.int32) for i in range(nb)])
    kj = np.concatenate([np.arange(i + 1, dtype=np.int32) for i in range(nb)])
    base = _position_features(B)
    const = lambda shape: pl.BlockSpec(shape, lambda n, p, qi, kj, fl: (0,) * len(shape))
    grid_spec = pltpu.PrefetchScalarGridSpec(
        num_scalar_prefetch=3,
        grid=(KVH_B, len(qi)),
        in_specs=[pl.BlockSpec((B, G_B * LANES), lambda n, p, qi, kj, fl: (qi[p], n)),
                  pl.BlockSpec((B, LANES), lambda n, p, qi, kj, fl: (kj[p], n)),
                  pl.BlockSpec((B, LANES), lambda n, p, qi, kj, fl: (kj[p], n)),
                  const(k_gain.shape), const(base.shape),
                  const(lq1.shape), const(lk1.shape), const(lq2.shape), const(lk2.shape)],
        out_specs=pl.BlockSpec((B, G_B * LANES), lambda n, p, qi, kj, fl: (qi[p], n)),
        scratch_shapes=[pltpu.VMEM((2 * G_B, 2 * LANES, B), BF16),
                        pltpu.VMEM((2 * G_B, LANES + ONES_ROWS, B), F32)] + [pltpu.VMEM((2 * G_B, B, LANES), F32)] * 3,
    )
    return pl.pallas_call(
        functools.partial(_diff_prompt_kernel, B=B, n_heads=n_heads, lam_init=lam_init),
        out_shape=jax.ShapeDtypeStruct(qb.shape, F32),
        grid_spec=grid_spec,
        compiler_params=_cparams(("parallel", "arbitrary")),
        name="diff_prompt",
    )(jnp.asarray(qi), jnp.asarray(kj), fast_flag, qb, kb, vb, k_gain, base, lq1, lk1, lq2, lk2)


def _diff_sample_kernel(pt_ref, flag_ref, q_ref, kn_ref, vn_ref, kg_ref, lq1_ref, lk1_ref, lq2_ref, lk2_ref,
                        ck_ref, cv_ref, o_ref, kbuf, vbuf, ksem, vsem, m_scr, l_scr, acc_scr,
                        *, P, page, n_chunks, n_seq, T, n_heads, lam_init):
    b = pl.program_id(0)
    c = pl.program_id(1)
    step = b * n_chunks + c
    slot = lax.rem(step, 2)
    past = n_chunks * P * page
    R = G_B * 2 * T
    fast = flag_ref[0] != 0

    def k_copy(pg, sl, p):
        return pltpu.make_async_copy(ck_ref.at[pg], kbuf.at[sl, p], ksem.at[sl])

    def v_copy(pg, sl, p):
        return pltpu.make_async_copy(cv_ref.at[pg], vbuf.at[sl, p], vsem.at[sl])

    def issue(bb, cc, sl):
        for p in range(P):
            pg = pt_ref[bb, cc * P + p]
            k_copy(pg, sl, p).start()
            v_copy(pg, sl, p).start()

    @pl.when(step == 0)
    def _():
        issue(0, 0, 0)

    nxt = step + 1

    @pl.when(nxt < n_seq * n_chunks)
    def _():
        issue(nxt // n_chunks, lax.rem(nxt, n_chunks), 1 - slot)

    for p in range(P):
        k_copy(0, slot, p).wait()
        v_copy(0, slot, p).wait()

    lane = lax.broadcasted_iota(jnp.int32, (T, LANES), 1)
    rrow = lax.broadcasted_iota(jnp.int32, (R, 1), 0)
    trow1 = lax.rem(rrow, T).astype(F32)
    col = lax.broadcasted_iota(jnp.int32, (1, P * page), 1)
    rel = (col + (c * (P * page) - past)).astype(F32)

    def lhs_and_slopes(n):
        blocks = []
        for g in range(G_B):
            qh = q_ref[:, (n * G_B + g) * LANES:(n * G_B + g + 1) * LANES]
            for cs in range(2):
                half = (lane < D_B) if cs == 0 else (lane >= D_B)
                blocks.append(jnp.where(half, qh, 0.0))
        lhs = jnp.concatenate(blocks, axis=0)
        slopes = jnp.where(rrow < 2 * T, _alibi_slope(n_heads, n * G_B), _alibi_slope(n_heads, n * G_B + 1)) * LOG2E
        return lhs, slopes

    def cached_kv(n):
        kt = jnp.concatenate([kbuf[slot, p, n * LANES:(n + 1) * LANES, :] for p in range(P)], axis=1).astype(BF16)
        vv = jnp.concatenate([vbuf[slot, p, pl.ds(n, page, stride=KVH_B), :] for p in range(P)], axis=0).astype(BF16)
        return kt, vv

    def new_kv(n):
        zpad = jnp.zeros((LANES - T, LANES), F32)
        kn = jnp.concatenate([kn_ref[:, n * LANES:(n + 1) * LANES], zpad], axis=0).astype(BF16)
        vn = jnp.concatenate([vn_ref[:, n * LANES:(n + 1) * LANES], zpad], axis=0).astype(BF16)
        ncol = lax.broadcasted_iota(jnp.int32, (R, LANES), 1)
        keep = ncol <= lax.rem(lax.broadcasted_iota(jnp.int32, (R, LANES), 0), T)
        return kn, vn, ncol.astype(F32), keep

    def write_out(n, res, lam):
        for g in range(G_B):
            r0 = g * 2 * T
            o_ref[:, (n * G_B + g) * LANES:(n * G_B + g + 1) * LANES] = res[r0:r0 + T] - lam * res[r0 + T:r0 + 2 * T]

    def run(use_bound):
        @pl.when(c == 0)
        def _():
            if not use_bound:
                m_scr[...] = jnp.full_like(m_scr, NEG_BIG)
            l_scr[...] = jnp.zeros_like(l_scr)
            acc_scr[...] = jnp.zeros_like(acc_scr)

        kmax = math.sqrt(D_B) * jnp.max(jnp.abs(kg_ref[...]), axis=-1, keepdims=True)

        def row_shift(lhs, slopes):
            return slopes * trow1 + jnp.sqrt(jnp.sum(lhs * lhs, axis=1, keepdims=True)) * kmax

        def accumulate(n, s, vv):
            if use_bound:
                pr = jnp.exp2(s)
                l_scr[n] = l_scr[n] + _lane_tile_sum(pr)
                acc_scr[n] = acc_scr[n] + _dot(pr.astype(BF16), vv)
            else:
                _online_update(n, s, vv, m_scr, l_scr, acc_scr)

        for n in range(KVH_B):
            lhs, slopes = lhs_and_slopes(n)
            kt, vv = cached_kv(n)
            s = _dot(lhs.astype(BF16), kt) + slopes * rel
            if use_bound:
                s = s - row_shift(lhs, slopes)
            accumulate(n, s, vv)

        @pl.when(c == n_chunks - 1)
        def _():
            lam = _lam_value(lq1_ref[...], lk1_ref[...], lq2_ref[...], lk2_ref[...], lam_init)
            for n in range(KVH_B):
                lhs, slopes = lhs_and_slopes(n)
                kn, vn, ncol, keep = new_kv(n)
                s = _dot_nt(lhs.astype(BF16), kn) + slopes * ncol
                if use_bound:
                    s = s - row_shift(lhs, slopes)
                accumulate(n, jnp.where(keep, s, NEG_BIG), vn)
                den = jnp.sum(l_scr[n], axis=1, keepdims=True) if use_bound else l_scr[n]
                write_out(n, acc_scr[n] / den, lam)

    @pl.when(fast)
    def _():
        run(True)

    @pl.when(jnp.logical_not(fast))
    def _():
        run(False)


def _diff_sample(qb, kb_new, vb_new, page_table, kt_pages, v_pages, k_gain, fast_flag, lq1, lk1, lq2, lk2, lam_init,
                 T, P):
    n_seq, n_pages = page_table.shape
    page = kt_pages.shape[2]
    n_heads = qb.shape[1] // LANES
    assert n_pages % P == 0 and qb.shape[0] == n_seq * T and page == LANES
    n_chunks = n_pages // P
    R = G_B * 2 * T
    small = pl.BlockSpec(lq1.shape, lambda b, c, pt, fl: (0, 0))
    rows = lambda w: pl.BlockSpec((T, w), lambda b, c, pt, fl: (b, 0))
    grid_spec = pltpu.PrefetchScalarGridSpec(
        num_scalar_prefetch=2,
        grid=(n_seq, n_chunks),
        in_specs=[rows(qb.shape[1]), rows(kb_new.shape[1]), rows(vb_new.shape[1]), small, small, small, small, small,
                  pl.BlockSpec(memory_space=pl.ANY), pl.BlockSpec(memory_space=pl.ANY)],
        out_specs=rows(qb.shape[1]),
        scratch_shapes=[pltpu.VMEM((2, P) + kt_pages.shape[1:], F32),
                        pltpu.VMEM((2, P) + v_pages.shape[1:], F32),
                        pltpu.SemaphoreType.DMA((2,)), pltpu.SemaphoreType.DMA((2,)),
                        pltpu.VMEM((KVH_B, R, LANES), F32), pltpu.VMEM((KVH_B, R, LANES), F32),
                        pltpu.VMEM((KVH_B, R, LANES), F32)],
    )
    return pl.pallas_call(
        functools.partial(_diff_sample_kernel, P=P, page=page, n_chunks=n_chunks, n_seq=n_seq, T=T,
                          n_heads=n_heads, lam_init=lam_init),
        out_shape=jax.ShapeDtypeStruct(qb.shape, F32),
        grid_spec=grid_spec,
        compiler_params=_cparams(("arbitrary", "arbitrary")),
        name="diff_sample",
    )(page_table, fast_flag, qb, kb_new, vb_new, k_gain, lq1, lk1, lq2, lk2, kt_pages, v_pages)


def _diff_fused_kernel(qi_ref, kj_ref, pt_ref, q_ref, k_ref, v_ref, kg_ref, base_ref, lq1_ref, lk1_ref, lq2_ref,
                       lk2_ref, qs_ref, kn_ref, vn_ref, ck_ref, cv_ref, o_ref, os_ref,
                       qxt_scr, acct_scr, kbuf, vbuf, ksem, vsem, ls_scr, accs_scr,
                       *, B, P, CH, page, n_chunks, n_seq, T, n_heads, lam_init):
    n = pl.program_id(0)
    p = pl.program_id(1)
    i = qi_ref[p]
    j = kj_ref[p]
    lam = _lam_value(lq1_ref[...], lk1_ref[...], lq2_ref[...], lk2_ref[...], lam_init)
    kmax = math.sqrt(D_B) * jnp.max(jnp.abs(kg_ref[...]), axis=-1, keepdims=True)

    total = n_seq * n_chunks
    step = n * pl.num_programs(1) + p
    group = lax.rem(step, 2)
    past = n_chunks * P * page
    R = G_B * 2 * T

    def k_copy(pg, sl, pp):
        return pltpu.make_async_copy(ck_ref.at[pg], kbuf.at[sl, pp], ksem.at[sl])

    def v_copy(pg, sl, pp):
        return pltpu.make_async_copy(cv_ref.at[pg], vbuf.at[sl, pp], vsem.at[sl])

    def issue(chunk, sl):
        bb = chunk // n_chunks
        cc = lax.rem(chunk, n_chunks)
        for pp in range(P):
            pg = pt_ref[bb, cc * P + pp]
            k_copy(pg, sl, pp).start()
            v_copy(pg, sl, pp).start()

    def issue_step(s, grp):
        for u in range(CH):
            issue(jnp.minimum(s * CH + u, total - 1), grp * CH + u)

    def wait_group(grp):
        for u in range(CH):
            for pp in range(P):
                k_copy(0, grp * CH + u, pp).wait()
                v_copy(0, grp * CH + u, pp).wait()

    @pl.when(step == 0)
    def _():
        issue_step(0, 0)
        issue_step(1, 1)

    wait_group(group)

    def prefetch_ahead():
        issue_step(step + 2, group)

    def sample_chunk(u, write):
        cid = step * CH + u
        slot = group * CH + u
        valid = cid < total
        c = lax.rem(cid, n_chunks)
        first = jnp.logical_and(valid, c == 0)
        gate = valid.astype(F32)
        lane = lax.broadcasted_iota(jnp.int32, (T, LANES), 1)
        rrow = lax.broadcasted_iota(jnp.int32, (R, 1), 0)
        trow1 = lax.rem(rrow, T).astype(F32)
        col = lax.broadcasted_iota(jnp.int32, (1, P * page), 1)
        rel = (col + (c * (P * page) - past)).astype(F32)
        zpad = jnp.zeros((LANES - T, LANES), F32)
        ncol = lax.broadcasted_iota(jnp.int32, (R, LANES), 1)
        keep_new = ncol <= lax.rem(lax.broadcasted_iota(jnp.int32, (R, LANES), 0), T)
        for nh in range(KVH_B):
            blocks = []
            for g in range(G_B):
                qh = qs_ref[:, (nh * G_B + g) * LANES:(nh * G_B + g + 1) * LANES]
                for cs in range(2):
                    blocks.append(jnp.where((lane < D_B) if cs == 0 else (lane >= D_B), qh, 0.0))
            lhs = jnp.concatenate(blocks, axis=0)
            slopes = jnp.where(rrow < 2 * T, _alibi_slope(n_heads, nh * G_B),
                               _alibi_slope(n_heads, nh * G_B + 1)) * LOG2E
            shift = slopes * trow1 + jnp.sqrt(jnp.sum(lhs * lhs, axis=1, keepdims=True)) * kmax
            lhs_bf = lhs.astype(BF16)
            kt = jnp.concatenate([kbuf[slot, pp, nh * LANES:(nh + 1) * LANES, :] for pp in range(P)],
                                 axis=1).astype(BF16)
            vv = jnp.concatenate([vbuf[slot, pp, pl.ds(nh, page, stride=KVH_B), :] for pp in range(P)],
                                 axis=0).astype(BF16)
            pr = jnp.exp2(_dot(lhs_bf, kt) + (slopes * rel - shift))
            l_new = jnp.where(first, 0.0, ls_scr[nh]) + gate * _lane_tile_sum(pr)
            acc_new = jnp.where(first, 0.0, accs_scr[nh]) + gate * _dot(pr.astype(BF16), vv)
            ls_scr[nh] = l_new
            accs_scr[nh] = acc_new
            if not write:
                continue
            kn = jnp.concatenate([kn_ref[:, nh * LANES:(nh + 1) * LANES], zpad], axis=0).astype(BF16)
            vn = jnp.concatenate([vn_ref[:, nh * LANES:(nh + 1) * LANES], zpad], axis=0).astype(BF16)
            s_new = _dot_nt(lhs_bf, kn) + (slopes * ncol.astype(F32) - shift)
            pn = jnp.exp2(jnp.where(keep_new, s_new, NEG_BIG))
            den = jnp.sum(l_new + pn, axis=1, keepdims=True)
            res = (acc_new + _dot(pn.astype(BF16), vn)) / den
            for g in range(G_B):
                r0 = g * 2 * T
                os_ref[:, (nh * G_B + g) * LANES:(nh * G_B + g + 1) * LANES] = (
                    res[r0:r0 + T] - lam * res[r0 + T:r0 + 2 * T])

    lane_b = lax.broadcasted_iota(jnp.int32, (B, LANES), 1)

    @pl.when(j == 0)
    def _():
        rloc = lax.broadcasted_iota(jnp.int32, (B, 1), 0).astype(F32)
        lane1 = lax.broadcasted_iota(jnp.int32, (1, LANES), 1)
        for g in range(G_B):
            slope = jnp.where(n == 0, _alibi_slope(n_heads, g), _alibi_slope(n_heads, G_B + g)) * LOG2E
            c_n0 = _split3_const(_alibi_slope(n_heads, g) * LOG2E)
            c_n1 = _split3_const(_alibi_slope(n_heads, G_B + g) * LOG2E)
            crow = jnp.zeros((1, LANES), F32)
            for t in range(3):
                st = jnp.where(n == 0, c_n0[t], c_n1[t])
                crow = jnp.where(lane1 == t, st * LANES, crow)
                crow = jnp.where(lane1 == 3 + t, st, crow)
            qh = q_ref[:, g * LANES:(g + 1) * LANES]
            for c in range(2):
                qm = jnp.where((lane_b < D_B) if c == 0 else (lane_b >= D_B), qh, 0.0)
                qn = jnp.sqrt(jnp.sum(qm * qm, axis=1, keepdims=True))
                m1, m2, m3 = _split3(slope * rloc + qn * kmax)
                feat = jnp.where(lane_b == 6, -m1.astype(F32),
                                 jnp.where(lane_b == 7, -m2.astype(F32),
                                           jnp.where(lane_b == 8, -m3.astype(F32), crow)))
                qxt_scr[2 * g + c] = jnp.concatenate([qm.T, feat.T], axis=0).astype(BF16)
        acct_scr[...] = jnp.zeros_like(acct_scr)

    def prompt_tile(masked):
        a = ((i - j) * (B // LANES)).astype(F32)
        ind = (lax.broadcasted_iota(jnp.int32, (1, LANES), 1) < 3).astype(F32)
        featk = (base_ref[...] - a * ind).astype(BF16)
        kx = jnp.concatenate([k_ref[...].astype(BF16), featk], axis=1)
        v1t = jnp.concatenate([v_ref[...].T.astype(BF16), jnp.ones((ONES_ROWS, B), BF16)], axis=0)
        if masked:
            keep = lax.broadcasted_iota(jnp.int32, (B, B), 0) <= lax.broadcasted_iota(jnp.int32, (B, B), 1)
        for mi in range(2 * G_B):
            st = _dot(kx, qxt_scr[mi])
            if masked:
                st = jnp.where(keep, st, NEG_BIG)
            acct_scr[mi] = acct_scr[mi] + _dot(v1t, jnp.exp2(st).astype(BF16))

    @pl.when(j < i)
    def _():
        prompt_tile(False)
        for u in range(CH):
            sample_chunk(u, u == CH - 1)
        prefetch_ahead()

    @pl.when(j == i)
    def _():
        prompt_tile(True)
        for u in range(CH):
            sample_chunk(u, u == CH - 1)
        for g in range(G_B):
            a0, a1 = acct_scr[2 * g], acct_scr[2 * g + 1]
            ot = a0[:LANES] / a0[LANES:LANES + 1] - lam * (a1[:LANES] / a1[LANES:LANES + 1])
            o_ref[:, g * LANES:(g + 1) * LANES] = ot.T
        prefetch_ahead()

    @pl.when(step == pl.num_programs(0) * pl.num_programs(1) - 1)
    def _():
        wait_group(0)
        wait_group(1)


def _diff_fused(qb, kb, vb, qs, kb_new, vb_new, page_table, kt_pages, v_pages, k_gain, lq1, lk1, lq2, lk2,
                lam_init, B, T, P, CH):
    Tp = qb.shape[0]
    n_heads = qb.shape[1] // LANES
    nb = Tp // B
    n_seq, n_pages = page_table.shape
    page = kt_pages.shape[2]
    n_chunks = n_pages // P
    qi = np.concatenate([np.full(i + 1, i, np.int32) for i in range(nb)])
    kj = np.concatenate([np.arange(i + 1, dtype=np.int32) for i in range(nb)])
    n_pairs = len(qi)
    assert Tp % B == 0 and n_pages % P == 0 and page == LANES and qs.shape[0] == n_seq * T
    assert n_chunks % CH == 0 and KVH_B * n_pairs * CH >= n_seq * n_chunks, "every page chunk needs a grid step"
    base = _position_features(B)
    R = G_B * 2 * T
    const = lambda a: pl.BlockSpec(a.shape, lambda n, p, qi, kj, pt: (0,) * a.ndim)
    seq_of = lambda n, p: jnp.minimum(((n * n_pairs + p) * CH) // n_chunks, n_seq - 1)
    srow = lambda w: pl.BlockSpec((T, w), lambda n, p, qi, kj, pt: (seq_of(n, p), 0))
    grid_spec = pltpu.PrefetchScalarGridSpec(
        num_scalar_prefetch=3,
        grid=(KVH_B, n_pairs),
        in_specs=[pl.BlockSpec((B, G_B * LANES), lambda n, p, qi, kj, pt: (qi[p], n)),
                  pl.BlockSpec((B, LANES), lambda n, p, qi, kj, pt: (kj[p], n)),
                  pl.BlockSpec((B, LANES), lambda n, p, qi, kj, pt: (kj[p], n)),
                  const(k_gain), const(base), const(lq1), const(lk1), const(lq2), const(lk2),
                  srow(qs.shape[1]), srow(kb_new.shape[1]), srow(vb_new.shape[1]),
                  pl.BlockSpec(memory_space=pl.ANY), pl.BlockSpec(memory_space=pl.ANY)],
        out_specs=[pl.BlockSpec((B, G_B * LANES), lambda n, p, qi, kj, pt: (qi[p], n)), srow(qs.shape[1])],
        scratch_shapes=[pltpu.VMEM((2 * G_B, 2 * LANES, B), BF16),
                        pltpu.VMEM((2 * G_B, LANES + ONES_ROWS, B), F32),
                        pltpu.VMEM((2 * CH, P) + kt_pages.shape[1:], F32),
                        pltpu.VMEM((2 * CH, P) + v_pages.shape[1:], F32),
                        pltpu.SemaphoreType.DMA((2 * CH,)), pltpu.SemaphoreType.DMA((2 * CH,)),
                        pltpu.VMEM((KVH_B, R, LANES), F32), pltpu.VMEM((KVH_B, R, LANES), F32)],
    )
    return pl.pallas_call(
        functools.partial(_diff_fused_kernel, B=B, P=P, CH=CH, page=page, n_chunks=n_chunks, n_seq=n_seq, T=T,
                          n_heads=n_heads, lam_init=lam_init),
        out_shape=[jax.ShapeDtypeStruct(qb.shape, F32), jax.ShapeDtypeStruct(qs.shape, F32)],
        grid_spec=grid_spec,
        compiler_params=_cparams(("arbitrary", "arbitrary")),
        name="diff_fused",
    )(jnp.asarray(qi), jnp.asarray(kj), page_table, qb, kb, vb, k_gain, base, lq1, lk1, lq2, lk2,
      qs, kb_new, vb_new, kt_pages, v_pages)


def _outproj_a_kernel(x_ref, oa_ref, gsa_ref, ob_ref, gsb_ref, og_ref, sg_ref, w_ref, y_ref, *, lam_init):
    def branch(o_ref, gate_ref, gain, scale):
        parts = []
        for h in range(o_ref.shape[1] // LANES):
            sl = slice(h * LANES, (h + 1) * LANES)
            parts.append(_rms_rows(o_ref[:, sl], gain) * scale * gate_ref[:, sl])
        return jnp.concatenate(parts, axis=1).astype(BF16)

    ya = branch(oa_ref, gsa_ref, og_ref[...], 1.0)
    yb = branch(ob_ref, gsb_ref, sg_ref[...], 1.0 - lam_init)
    wa = oa_ref.shape[1]
    y_ref[...] = x_ref[...] + _dot(ya, w_ref[0:wa, :]) + _dot(yb, w_ref[wa:, :])


def _outproj_a(x, oa, gsa, ob, gsb, o_gain, subln, w_bf16, lam_init, tm):
    T, D = x.shape
    row = lambda w: pl.BlockSpec((tm, w), lambda i: (i, 0))
    return pl.pallas_call(
        functools.partial(_outproj_a_kernel, lam_init=lam_init),
        out_shape=jax.ShapeDtypeStruct((T, D), F32),
        grid=(T // tm,),
        in_specs=[row(D), row(oa.shape[1]), row(oa.shape[1]), row(ob.shape[1]), row(ob.shape[1]),
                  _full(o_gain.shape), _full(subln.shape), _full(w_bf16.shape)],
        out_specs=row(D),
        compiler_params=_cparams(("parallel",)),
        name="outproj_a",
    )(x, oa, gsa, ob, gsb, o_gain, subln, w_bf16)


def _inproj_c_kernel(x_ref, g_ref, w_ref, qg_ref, kg_ref, bd_ref, q_ref, k_ref, v_ref, gs_ref, *, wq, wkv):
    hb = _rms_rows(x_ref[...], g_ref[...]).astype(BF16)
    bd = bd_ref[...]
    q_ref[...] = _group_rms(_dot(hb, w_ref[:, 0:wq]), qg_ref[...], bd) * (HD_C ** -0.5 * LOG2E)
    k_ref[...] = _group_rms(_dot(hb, w_ref[:, wq:wq + wkv]), kg_ref[...], bd)
    v_ref[...] = _dot(hb, w_ref[:, wq + wkv:wq + 2 * wkv])
    gs_ref[...] = _silu(_dot(hb, w_ref[:, wq + 2 * wkv:]))


def _inproj_c(x, g, w_bf16, q_gain_t, k_gain_t, bd, tm):
    T, D = x.shape
    wq, wkv = q_gain_t.shape[1], k_gain_t.shape[1]
    widths = (wq, wkv, wkv, wq)
    row = lambda w: pl.BlockSpec((tm, w), lambda i: (i, 0))
    return pl.pallas_call(
        functools.partial(_inproj_c_kernel, wq=wq, wkv=wkv),
        out_shape=[jax.ShapeDtypeStruct((T, w), F32) for w in widths],
        grid=(T // tm,),
        in_specs=[row(D), _full(g.shape), _full(w_bf16.shape), _full(q_gain_t.shape), _full(k_gain_t.shape),
                  _full(bd.shape)],
        out_specs=[row(w) for w in widths],
        compiler_params=_cparams(("parallel",)),
        name="inproj_c",
    )(x, g, w_bf16, q_gain_t, k_gain_t, bd)


def _swa_softmax_pv(s, sink, v_list):
    m = sink
    for piece in s:
        m = jnp.maximum(m, jnp.max(piece, axis=1, keepdims=True))
    den = jnp.exp2(sink - m)
    out = None
    for piece, pv in zip(s, v_list):
        p = jnp.exp2(piece - m)
        den = den + jnp.sum(p, axis=1, keepdims=True)
        term = pv(p.astype(BF16))
        out = term if out is None else out + term
    return out / den


def _swa_prompt_kernel(flag_ref, q_ref, kc_ref, kp_ref, vc_ref, vp_ref, snk_ref, qg_ref, kg_ref, o_ref, *, n_heads):
    i = pl.program_id(0)
    W = q_ref.shape[0]
    n_pairs = n_heads // KVH_C
    half_rows = n_pairs * W
    lane = lax.broadcasted_iota(jnp.int32, (W, LANES), 1)
    blocks = []
    for n in range(KVH_C):
        half = (lane < HD_C) if n == 0 else (lane >= HD_C)
        for m in range(n_pairs):
            blocks.append(jnp.where(half, q_ref[:, m * LANES:(m + 1) * LANES], 0.0))
    lhs = jnp.concatenate(blocks, axis=0).astype(BF16)
    keys = jnp.concatenate([kp_ref[...], kc_ref[...]], axis=0).astype(BF16)
    vals = jnp.concatenate([vp_ref[...], vc_ref[...]], axis=0).astype(BF16)
    r = lax.broadcasted_iota(jnp.int32, (W, 2 * W), 0)
    c = lax.broadcasted_iota(jnp.int32, (W, 2 * W), 1)
    dist = r + W - c
    valid = jnp.logical_and(jnp.logical_and(dist >= 0, dist <= W), jnp.logical_or(c >= W, i > 0))
    neg_dist = jnp.where(valid, -dist.astype(F32), NEG_BIG)

    def write(res):
        for m in range(n_pairs):
            lo = res[m * W:(m + 1) * W]
            hi = res[half_rows + m * W:half_rows + (m + 1) * W]
            o_ref[:, m * LANES:(m + 1) * LANES] = jnp.where(lane < HD_C, lo, hi)

    @pl.when(flag_ref[0] != 0)
    def _():
        bound = ((LOG2E * math.sqrt(HD_C)) * jnp.max(jnp.abs(qg_ref[...]), axis=-1, keepdims=True)
                 * jnp.max(jnp.abs(kg_ref[...]), axis=-1, keepdims=True))
        biases, sink_terms = [], []
        for h in range(n_heads):
            sink_h = snk_ref[:, h:h + 1] * LOG2E
            shift = jnp.maximum(bound, sink_h)
            biases.append((_alibi_slope(n_heads, h) * LOG2E) * neg_dist - shift)
            sink_terms.append(jnp.broadcast_to(jnp.exp2(sink_h - shift), (W, LANES)))
        pr = jnp.exp2(_dot_nt(lhs, keys) + jnp.concatenate(biases, axis=0))
        vals1 = jnp.concatenate([vals, jnp.ones_like(vals)], axis=1)
        res = _dot(pr.astype(BF16), vals1)
        den = res[:, LANES:] + jnp.concatenate(sink_terms, axis=0)
        write(res[:, :LANES] / den)

    @pl.when(flag_ref[0] == 0)
    def _():
        slopes, sinks = [], []
        for h in range(n_heads):
            slopes.append(jnp.full((W, 1), _alibi_slope(n_heads, h) * LOG2E, F32))
            sinks.append(jnp.broadcast_to(snk_ref[:, h:h + 1] * LOG2E, (W, 1)))
        slope = jnp.concatenate(slopes, axis=0)
        sink = jnp.concatenate(sinks, axis=0)
        s = _dot_nt(lhs, keys) + slope * jnp.tile(neg_dist, (n_heads, 1))
        m_row = jnp.maximum(jnp.max(s, axis=1, keepdims=True), sink)
        pr = jnp.exp2(s - m_row)
        den = jnp.sum(pr, axis=1, keepdims=True) + jnp.exp2(sink - m_row)
        write(_dot(pr.astype(BF16), vals) / den)


def _swa_prompt(q, k, v, sinks, q_gain, k_gain, fast_flag):
    T, Wq = q.shape
    n_heads = Wq // HD_C
    nb = T // WINDOW
    cur = lambda w: pl.BlockSpec((WINDOW, w), lambda i, fl: (i, 0))
    prev = lambda w: pl.BlockSpec((WINDOW, w), lambda i, fl: (jnp.maximum(i - 1, 0), 0))
    const = lambda a: pl.BlockSpec(a.shape, lambda i, fl: (0,) * a.ndim)
    kw = k.shape[1]
    grid_spec = pltpu.PrefetchScalarGridSpec(
        num_scalar_prefetch=1,
        grid=(nb,),
        in_specs=[cur(Wq), cur(kw), prev(kw), cur(kw), prev(kw), const(sinks), const(q_gain), const(k_gain)],
        out_specs=cur(Wq),
    )
    return pl.pallas_call(
        functools.partial(_swa_prompt_kernel, n_heads=n_heads),
        out_shape=jax.ShapeDtypeStruct(q.shape, F32),
        grid_spec=grid_spec,
        compiler_params=_cparams(("parallel",)),
        name="swa_prompt",
    )(fast_flag, q, k, k, v, v, sinks, q_gain, k_gain)


def _swa_sample_kernel(q_ref, kn_ref, vn_ref, ckt_ref, cvt_ref, snk_ref, o_ref, *, T, n_heads, n_sub):
    n_pairs = n_heads // KVH_C
    R = n_heads * T
    lane = lax.broadcasted_iota(jnp.int32, (T, LANES), 1)
    rr = lax.broadcasted_iota(jnp.int32, (R, WINDOW), 0)
    cc = lax.broadcasted_iota(jnp.int32, (R, WINDOW), 1)
    t = lax.rem(rr, T)
    hrow = lax.broadcasted_iota(jnp.int32, (R, 1), 0) // T
    slope = jnp.exp2(-8.0 * (hrow + 1).astype(F32) / n_heads) * LOG2E
    sink = jnp.zeros((R, 1), F32)
    for h in range(n_heads):
        sink = jnp.where(hrow == h, snk_ref[:, h:h + 1], sink)
    sink = sink * LOG2E
    bias_c = jnp.where(cc >= t, -slope * (t + WINDOW - cc).astype(F32), NEG_BIG)
    bias_n = jnp.where(cc <= t, -slope * (t - cc).astype(F32), NEG_BIG)
    zpad = jnp.zeros((WINDOW - T, LANES), F32)
    half_rows = n_pairs * T
    for sq in range(n_sub):
        rows = slice(sq * T, (sq + 1) * T)
        blocks = []
        for n in range(KVH_C):
            half = (lane < HD_C) if n == 0 else (lane >= HD_C)
            for m in range(n_pairs):
                blocks.append(jnp.where(half, q_ref[rows, m * LANES:(m + 1) * LANES], 0.0))
        lhs = jnp.concatenate(blocks, axis=0).astype(BF16)
        kn = jnp.concatenate([kn_ref[rows, :], zpad], axis=0).astype(BF16)
        vn = jnp.concatenate([vn_ref[rows, :], zpad], axis=0).astype(BF16)
        ckt = ckt_ref[sq].astype(BF16)
        cvt = cvt_ref[sq].astype(BF16)
        s_c = _dot(lhs, ckt) + bias_c
        s_n = _dot_nt(lhs, kn) + bias_n
        res = _swa_softmax_pv([s_c, s_n], sink, [lambda p: _dot_nt(p, cvt), lambda p: _dot(p, vn)])
        for m in range(n_pairs):
            lo = res[m * T:(m + 1) * T]
            hi = res[half_rows + m * T:half_rows + (m + 1) * T]
            o_ref[rows, m * LANES:(m + 1) * LANES] = jnp.where(lane < HD_C, lo, hi)


def _swa_sample(q, k_new, v_new, ckt, cvt, sinks, T, n_sub):
    BT, Wq = q.shape
    B = ckt.shape[0]
    n_heads = Wq // HD_C
    assert B % n_sub == 0
    rows = lambda w: pl.BlockSpec((n_sub * T, w), lambda b: (b, 0))
    cache = pl.BlockSpec((n_sub,) + ckt.shape[1:], lambda b: (b, 0, 0))
    return pl.pallas_call(
        functools.partial(_swa_sample_kernel, T=T, n_heads=n_heads, n_sub=n_sub),
        out_shape=jax.ShapeDtypeStruct(q.shape, F32),
        grid=(B // n_sub,),
        in_specs=[rows(Wq), rows(k_new.shape[1]), rows(v_new.shape[1]), cache, cache, _full(sinks.shape)],
        out_specs=rows(Wq),
        compiler_params=_cparams(("parallel",)),
        name="swa_sample",
    )(q, k_new, v_new, ckt, cvt, sinks)


def _outproj_c_kernel(x_ref, o_ref, gs_ref, w_ref, y_ref):
    y_ref[...] = x_ref[...] + _dot((o_ref[...] * gs_ref[...]).astype(BF16), w_ref[...])


def _outproj_c(x, o, gs, w_bf16, tm):
    T, D = x.shape
    row = lambda w: pl.BlockSpec((tm, w), lambda i: (i, 0))
    return pl.pallas_call(
        _outproj_c_kernel,
        out_shape=jax.ShapeDtypeStruct((T, D), F32),
        grid=(T // tm,),
        in_specs=[row(D), row(o.shape[1]), row(o.shape[1]), _full(w_bf16.shape)],
        out_specs=row(D),
        compiler_params=_cparams(("parallel",)),
        name="outproj_c",
    )(x, o, gs, w_bf16)


TOKEN_TILE = 512
GLA_CHUNK = 128
DIFF_BLOCK = 1024
DIFF_PAGES_PER_STEP = 32
RIDE_CHUNKS_PER_STEP = 2
SWA_SEQS_PER_STEP = 8
GLA_SEQS_PER_STEP = 4
GLA_HEADS_PER_STEP = 4


def _pair_perm(n_heads):
    n_pairs = n_heads // KVH_C
    cols = []
    for m in range(n_pairs):
        for n in range(KVH_C):
            h = n * n_pairs + m
            cols.extend(range(h * HD_C, (h + 1) * HD_C))
    return np.asarray(cols, np.int32)


def kernel(x_prompt, x_sample, state_hgrn, cache_k_diff, cache_v_diff, cache_k_swa, cache_v_swa, page_table, norm_a, w_in_a, w_out_a, lb_logits, hgrn_out_gain, diff_q_gain, diff_k_gain, diff_subln_gain, lam_q1, lam_k1, lam_q2, lam_k2, norm_c, w_in_c, w_out_c, swa_q_gain, swa_k_gain, sinks):
    batch, seq, d_model = x_prompt.shape
    n_seq, t_dec, _ = x_sample.shape
    assert batch == 1
    n_even, n_odd = norm_a.shape[0], norm_c.shape[0]
    depth = n_even + n_odd
    h_a = state_hgrn.shape[2]
    n_pool, page = cache_k_diff.shape[1], cache_k_diff.shape[2]
    w_b = w_out_a.shape[1] - h_a * DV_A
    h_b = w_b // (2 * D_B)
    h_c = sinks.shape[1]

    xp = x_prompt.reshape(seq, d_model)
    xs = x_sample.reshape(n_seq * t_dec, d_model)
    tile_p, tile_s = min(TOKEN_TILE, seq), min(TOKEN_TILE, n_seq * t_dec)
    bd64 = _block_diag_mean(2 * LANES, D_B)
    bd64_c = _block_diag_mean(LANES, HD_C)
    perm = _pair_perm(h_c)

    hs_p, hs_s, kdp, vdp, kds, vds, ksp, vsp, kss, vss = ([] for _ in range(10))
    for l in range(depth):
        if l % 2 == 0:
            e = l // 2
            lam_init = 0.8 - 0.6 * math.exp(-0.3 * l)
            w_in = w_in_a[e].astype(BF16)
            w_out = w_out_a[e].astype(BF16)
            g = norm_a[e:e + 1]
            qg = jnp.tile(diff_q_gain[e:e + 1], (1, h_b * 2))
            kg = jnp.tile(diff_k_gain[e:e + 1], (1, KVH_B * 2))
            lam_args = (lam_q1[e:e + 1], lam_k1[e:e + 1], lam_q2[e:e + 1], lam_k2[e:e + 1])
            kt_pages = jnp.transpose(cache_k_diff[e], (0, 2, 3, 4, 1)).reshape(n_pool, KVH_B * 2 * D_B, page)
            v_pages = cache_v_diff[e].reshape(n_pool, page * KVH_B, 2 * D_B)

            pa = _inproj_a(xp, g, w_in, lb_logits, qg, kg, bd64, e, tile_p)
            sa = _inproj_a(xs, g, w_in, lb_logits, qg, kg, bd64, e, tile_s)
            qa_p, lf_p, kk_p, ia_p, gsa_p, qb_p, kb_p, vb_p, gsb_p = pa
            qa_s, lf_s, kk_s, ia_s, gsa_s, qb_s, kb_s, vb_s, gsb_s = sa

            lb_min = jnp.min(jnp.cumsum(jax.nn.softmax(lb_logits.astype(F32), axis=0), axis=0)[e])
            gla_flag = (-GLA_SUB * jnp.log(lb_min) <= GLA_FAST_DECAY).astype(jnp.int32).reshape(1)
            oa_p, st_p = _gla_prompt(qa_p, kk_p, ia_p, lf_p, gla_flag, GLA_CHUNK, GLA_HEADS_PER_STEP)
            oa_s, st_s = _gla_sample(qa_s, kk_s, ia_s, lf_s, state_hgrn[e], t_dec, min(GLA_SEQS_PER_STEP, n_seq))
            logit_bound = (LOG2E * math.sqrt(D_B)) * jnp.max(jnp.abs(diff_q_gain[e])) * jnp.max(jnp.abs(diff_k_gain[e]))
            fast_flag = (logit_bound <= FAST_LOGIT_BOUND).astype(jnp.int32).reshape(1)
            k_gain = diff_k_gain[e:e + 1]
            def diff_separate():
                return (_diff_prompt(qb_p, kb_p, vb_p, k_gain, fast_flag, *lam_args, lam_init, DIFF_BLOCK),
                        _diff_sample(qb_s, kb_s, vb_s, page_table, kt_pages, v_pages, k_gain, fast_flag, *lam_args,
                                     lam_init, t_dec, DIFF_PAGES_PER_STEP))

            n_pages = page_table.shape[1]
            n_blk = seq // DIFF_BLOCK if seq % DIFF_BLOCK == 0 else 0
            n_ride = 0
            if n_pages % (DIFF_PAGES_PER_STEP * RIDE_CHUNKS_PER_STEP) == 0:
                steps_per_seq = n_pages // (DIFF_PAGES_PER_STEP * RIDE_CHUNKS_PER_STEP)
                n_ride = min(n_seq, (KVH_B * n_blk * (n_blk + 1) // 2) // steps_per_seq)

            def diff_fused():
                rows = n_ride * t_dec
                ob_p_f, ob_s_f = _diff_fused(qb_p, kb_p, vb_p, qb_s[:rows], kb_s[:rows], vb_s[:rows],
                                             page_table[:n_ride], kt_pages, v_pages, k_gain, *lam_args, lam_init,
                                             DIFF_BLOCK, t_dec, DIFF_PAGES_PER_STEP, RIDE_CHUNKS_PER_STEP)
                if n_ride == n_seq:
                    return ob_p_f, ob_s_f
                rest = _diff_sample(qb_s[rows:], kb_s[rows:], vb_s[rows:], page_table[n_ride:], kt_pages, v_pages,
                                    k_gain, fast_flag, *lam_args, lam_init, t_dec, DIFF_PAGES_PER_STEP)
                return ob_p_f, jnp.concatenate([ob_s_f, rest], axis=0)

            if n_ride > 0:
                ob_p, ob_s = lax.cond(fast_flag[0] != 0, diff_fused, diff_separate)
            else:
                ob_p, ob_s = diff_separate()

            og, sg = hgrn_out_gain[e:e + 1], diff_subln_gain[e:e + 1]
            xp = _outproj_a(xp, oa_p, gsa_p, ob_p, gsb_p, og, sg, w_out, lam_init, tile_p)
            xs = _outproj_a(xs, oa_s, gsa_s, ob_s, gsb_s, og, sg, w_out, lam_init, tile_s)

            hs_p.append(st_p[None])
            hs_s.append(st_s)
            kdp.append(kb_p.reshape(batch, seq, KVH_B, 2, D_B))
            vdp.append(vb_p.reshape(batch, seq, KVH_B, 2 * D_B))
            kds.append(kb_s.reshape(n_seq, t_dec, KVH_B, 2, D_B))
            vds.append(vb_s.reshape(n_seq, t_dec, KVH_B, 2 * D_B))
        else:
            o = l // 2
            wq = h_c * HD_C
            wkv = KVH_C * HD_C
            w_full = w_in_c[o]
            w_in = jnp.concatenate([w_full[:, :wq][:, perm], w_full[:, wq:wq + 2 * wkv],
                                    w_full[:, wq + 2 * wkv:][:, perm]], axis=1).astype(BF16)
            w_out = w_out_c[o][perm, :].astype(BF16)
            g = norm_c[o:o + 1]
            qg = jnp.tile(swa_q_gain[o:o + 1], (1, h_c))
            kg = jnp.tile(swa_k_gain[o:o + 1], (1, KVH_C))
            snk = sinks[o:o + 1]
            ckt = jnp.transpose(cache_k_swa[o], (0, 2, 3, 1)).reshape(n_seq, wkv, WINDOW)
            cvt = jnp.transpose(cache_v_swa[o], (0, 2, 3, 1)).reshape(n_seq, wkv, WINDOW)

            q_p, k_p, v_p, gs_p = _inproj_c(xp, g, w_in, qg, kg, bd64_c, tile_p)
            q_s, k_s, v_s, gs_s = _inproj_c(xs, g, w_in, qg, kg, bd64_c, tile_s)
            swa_bound = ((LOG2E * math.sqrt(HD_C)) * jnp.max(jnp.abs(swa_q_gain[o]))
                         * jnp.max(jnp.abs(swa_k_gain[o])))
            swa_flag = (swa_bound <= FAST_LOGIT_BOUND).astype(jnp.int32).reshape(1)
            o_p = _swa_prompt(q_p, k_p, v_p, snk, swa_q_gain[o:o + 1], swa_k_gain[o:o + 1], swa_flag)
            o_s = _swa_sample(q_s, k_s, v_s, ckt, cvt, snk, t_dec, min(SWA_SEQS_PER_STEP, n_seq))
            xp = _outproj_c(xp, o_p, gs_p, w_out, tile_p)
            xs = _outproj_c(xs, o_s, gs_s, w_out, tile_s)

            ksp.append(k_p[-WINDOW:].reshape(batch, WINDOW, KVH_C, HD_C))
            vsp.append(v_p[-WINDOW:].reshape(batch, WINDOW, KVH_C, HD_C))
            k_new = k_s.reshape(n_seq, t_dec, KVH_C, HD_C)
            v_new = v_s.reshape(n_seq, t_dec, KVH_C, HD_C)
            kss.append(jnp.concatenate([cache_k_swa[o], k_new], axis=1)[:, -WINDOW:])
            vss.append(jnp.concatenate([cache_v_swa[o], v_new], axis=1)[:, -WINDOW:])
    return (xp.reshape(batch, seq, d_model), xs.reshape(n_seq, t_dec, d_model),
            jnp.stack(hs_p), jnp.stack(hs_s), jnp.stack(kdp), jnp.stack(vdp), jnp.stack(kds), jnp.stack(vds),
            jnp.stack(ksp), jnp.stack(vsp), jnp.stack(kss), jnp.stack(vss))
```

```python
import functools
import math

import jax
import jax.numpy as jnp
import ml_dtypes
import numpy as np
from jax import lax
from jax.experimental import pallas as pl
from jax.experimental.pallas import tpu as pltpu

F32 = jnp.float32
BF16 = jnp.bfloat16

EPS = 1e-6
LOG2E = 1.4426950408889634
NEG_BIG = -1e30
LANES = 128
VMEM_LIMIT_BYTES = 56 * 1024 * 1024

DK_A = 128
DV_A = 128
D_B = 64
KVH_B = 2
G_B = 2
HD_C = 64
KVH_C = 2
WINDOW = 128
GLA_SUB = 16
FAST_LOGIT_BOUND = 60.0
GLA_FAST_DECAY = 80.0
ONES_ROWS = 16


def _dot(a, b):
    return jnp.dot(a, b, preferred_element_type=F32)


def _dot_nt(a, b):
    return lax.dot_general(a, b, (((1,), (1,)), ((), ())), preferred_element_type=F32)


def _dot_tn(a, b):
    return lax.dot_general(a, b, (((0,), (0,)), ((), ())), preferred_element_type=F32)


def _silu(x):
    return x * jax.nn.sigmoid(x)


def _rms_rows(x, gain):
    return x * lax.rsqrt(jnp.mean(x * x, axis=-1, keepdims=True) + EPS) * gain


def _split3(x):
    x1 = x.astype(BF16)
    r1 = x - x1.astype(F32)
    x2 = r1.astype(BF16)
    x3 = (r1 - x2.astype(F32)).astype(BF16)
    return x1, x2, x3


def _group_meansq(x, bd):
    return _dot((x * x).astype(BF16), bd)


def _group_rms(x, gain, bd):
    wb = bd.shape[0]
    outs = []
    for j in range(x.shape[1] // wb):
        part = x[:, j * wb:(j + 1) * wb]
        ms = _group_meansq(part, bd)
        outs.append(part * lax.rsqrt(ms + EPS) * gain[:, j * wb:(j + 1) * wb])
    return outs[0] if len(outs) == 1 else jnp.concatenate(outs, axis=1)


def _block_diag_mean(width, group):
    r = np.arange(width)[:, None] // group
    c = np.arange(width)[None, :] // group
    return jnp.asarray((r == c).astype(np.float32) / group, dtype=BF16)


def _lam_value(lq1, lk1, lq2, lk2, lam_init):
    a = jnp.sum(lq1 * lk1, axis=-1, keepdims=True)
    b = jnp.sum(lq2 * lk2, axis=-1, keepdims=True)
    return jnp.exp(a) - jnp.exp(b) + lam_init


def _split3_const(x):
    x = np.float32(x)
    a = np.float32(ml_dtypes.bfloat16(x))
    b = np.float32(ml_dtypes.bfloat16(np.float32(x - a)))
    c = np.float32(ml_dtypes.bfloat16(np.float32(np.float32(x - a) - b)))
    return float(a), float(b), float(c)


def _lane_tile_sum(p):
    out = p[:, 0:LANES]
    for t in range(1, p.shape[1] // LANES):
        out = out + p[:, t * LANES:(t + 1) * LANES]
    return out


def _cparams(sem):
    return pltpu.CompilerParams(dimension_semantics=sem, vmem_limit_bytes=VMEM_LIMIT_BYTES)


def _full(shape):
    nd = len(shape)
    return pl.BlockSpec(shape, lambda *_: (0,) * nd)


def _inproj_a_kernel(x_ref, g_ref, w_ref, lbl_ref, qg_ref, kg_ref, bd_ref,
                     qa_ref, lf_ref, kk_ref, ia_ref, gsa_ref, qb_ref, kb_ref, vb_ref, gsb_ref, *, e, wa, wb, wkv):
    hb = _rms_rows(x_ref[...], g_ref[...]).astype(BF16)
    bd = bd_ref[...]
    o = [0]

    def seg(width):
        z = _dot(hb, w_ref[:, o[0]:o[0] + width])
        o[0] += width
        return z

    qa = seg(wa)
    qa_ref[...] = _silu(qa) * (DK_A ** -0.5)
    lbl = lbl_ref[...]
    ex = jnp.exp(lbl - jnp.max(lbl, axis=0, keepdims=True))
    sm = ex / jnp.sum(ex, axis=0, keepdims=True)
    lb = jnp.sum(sm[0:e + 1, :], axis=0, keepdims=True)
    f = lb + (1.0 - lb) * jax.nn.sigmoid(seg(wa))
    lf_ref[...] = jnp.log(f)
    kk_ref[...] = 1.0 - f
    ia_ref[...] = seg(wa)
    gsa_ref[...] = _silu(seg(wa)).astype(gsa_ref.dtype)
    qb_ref[...] = _group_rms(seg(wb), qg_ref[...], bd) * (D_B ** -0.5 * LOG2E)
    kb_ref[...] = _group_rms(seg(wkv), kg_ref[...], bd)
    vb_ref[...] = seg(wkv)
    gsb_ref[...] = _silu(seg(wb)).astype(gsb_ref.dtype)


def _inproj_a(x, g, w_bf16, lb_logits, q_gain_t, k_gain_t, bd, e, tm):
    T, D = x.shape
    wa, wb, wkv = lb_logits.shape[1], q_gain_t.shape[1], k_gain_t.shape[1]
    n_in = w_bf16.shape[1]
    assert n_in == 4 * wa + 2 * wb + 2 * wkv and T % tm == 0
    widths = (wa, wa, wa, wa, wa, wb, wkv, wkv, wb)
    dtypes = (F32, F32, F32, F32, BF16, F32, F32, F32, BF16)
    row = lambda w: pl.BlockSpec((tm, w), lambda i: (i, 0))
    return pl.pallas_call(
        functools.partial(_inproj_a_kernel, e=e, wa=wa, wb=wb, wkv=wkv),
        out_shape=[jax.ShapeDtypeStruct((T, w), d) for w, d in zip(widths, dtypes)],
        grid=(T // tm,),
        in_specs=[row(D), _full(g.shape), _full(w_bf16.shape), _full(lb_logits.shape),
                  _full(q_gain_t.shape), _full(k_gain_t.shape), _full(bd.shape)],
        out_specs=[row(w) for w in widths],
        compiler_params=_cparams(("parallel",)),
        name="inproj_a",
    )(x, g, w_bf16, lb_logits, q_gain_t, k_gain_t, bd)


def _gla_intra(q, k, b, bx, v_bf, C, c):
    n = C // c
    lane = lax.broadcasted_iota(jnp.int32, (c, C), 1)
    trow = lax.broadcasted_iota(jnp.int32, (c, LANES), 0)
    ones = jnp.ones((LANES, LANES), BF16)
    rows = []
    for i in range(n):
        sl = slice(i * c, (i + 1) * c)
        qi, ki, bi = q[sl], k[sl], b[sl]
        if i > 0:
            b0 = bx[i * c:i * c + 1, :]
            ks = k * jnp.exp(jnp.minimum(b0 - b, 0.0))
            qh = qi * jnp.exp(bi - b0)
            a_i = jnp.where(lane < i * c, _dot_nt(qh.astype(BF16), ks.astype(BF16)), 0.0)
        else:
            a_i = jnp.zeros((c, C), F32)
        pieces = []
        for s in range(c):
            d = jnp.minimum(bi - bi[s:s + 1, :], 0.0)
            pieces.append(jnp.where(trow >= s, qi * ki[s:s + 1, :] * jnp.exp(d), 0.0))
        r = _dot(jnp.concatenate(pieces, axis=0).astype(BF16), ones)
        for s in range(c):
            a_i = jnp.where(lane == i * c + s, r[s * c:(s + 1) * c, :C], a_i)
        rows.append(a_i)
    a = rows[0] if n == 1 else jnp.concatenate(rows, axis=0)
    return _dot(a.astype(BF16), v_bf)


def _gla_intra_bounded(q, k, b, bx, v_bf, C, c):
    lane = lax.broadcasted_iota(jnp.int32, (c, C), 1)
    trow = lax.broadcasted_iota(jnp.int32, (c, C), 0)
    rows = []
    for i in range(C // c):
        sl = slice(i * c, (i + 1) * c)
        b0 = bx[i * c:i * c + 1, :]
        ks = k * jnp.exp(jnp.minimum(b0 - b, GLA_FAST_DECAY))
        qh = q[sl] * jnp.exp(b[sl] - b0)
        rows.append(jnp.where(lane <= trow + i * c, _dot_nt(qh.astype(BF16), ks.astype(BF16)), 0.0))
    return _dot(jnp.concatenate(rows, axis=0).astype(BF16), v_bf)


def _gla_prompt_kernel(flag_ref, q_ref, k_ref, v_ref, lf_ref, o_ref, s_ref, st_scr, *, C, c, hps):
    j = pl.program_id(1)

    @pl.when(j == 0)
    def _():
        st_scr[...] = jnp.zeros_like(st_scr)

    def chunk(intra):
        rr = lax.broadcasted_iota(jnp.int32, (C, C), 0)
        cc = lax.broadcasted_iota(jnp.int32, (C, C), 1)
        tri = (cc <= rr).astype(BF16)
        for h in range(hps):
            sl = slice(h * DK_A, (h + 1) * DK_A)
            q, k, v, lf = q_ref[:, sl], k_ref[:, sl], v_ref[:, sl], lf_ref[:, sl]
            l1, l2, l3 = _split3(lf)
            b = _dot(tri, l1) + _dot(tri, l2) + _dot(tri, l3)
            st = st_scr[h]
            v_bf = v.astype(BF16)
            o = _dot_nt((q * jnp.exp(b)).astype(BF16), st.astype(BF16))
            o_ref[:, sl] = o + intra(q, k, b, b - lf, v_bf, C, c)
            b_end = b[C - 1:C, :]
            kt = k * jnp.exp(b_end - b)
            st_scr[h] = st * jnp.exp(b_end) + _dot_tn(v_bf, kt.astype(BF16))

    @pl.when(flag_ref[0] != 0)
    def _():
        chunk(_gla_intra_bounded)

    @pl.when(flag_ref[0] == 0)
    def _():
        chunk(_gla_intra)

    @pl.when(j == pl.num_programs(1) - 1)
    def _():
        for h in range(hps):
            s_ref[h] = st_scr[h].T


def _gla_prompt(q, k, v, lf, bounded_flag, C, hps):
    T, W = q.shape
    H = W // DK_A
    assert T % C == 0 and C % GLA_SUB == 0 and H % hps == 0
    blk = pl.BlockSpec((C, hps * DK_A), lambda h, j, fl: (j, h))
    grid_spec = pltpu.PrefetchScalarGridSpec(
        num_scalar_prefetch=1,
        grid=(H // hps, T // C),
        in_specs=[blk, blk, blk, blk],
        out_specs=[blk, pl.BlockSpec((hps, DK_A, DV_A), lambda h, j, fl: (h, 0, 0))],
        scratch_shapes=[pltpu.VMEM((hps, DV_A, DK_A), F32)],
    )
    return pl.pallas_call(
        functools.partial(_gla_prompt_kernel, C=C, c=GLA_SUB, hps=hps),
        out_shape=[jax.ShapeDtypeStruct((T, W), F32), jax.ShapeDtypeStruct((H, DK_A, DV_A), F32)],
        grid_spec=grid_spec,
        compiler_params=_cparams(("parallel", "arbitrary")),
        name="gla_prompt",
    )(bounded_flag, q, k, v, lf)


def _gla_sample_kernel(q_ref, k_ref, v_ref, lf_ref, s0_ref, o_ref, s_ref, *, T, H, n_sub):
    pad = LANES - T
    row = lax.broadcasted_iota(jnp.int32, (T, LANES), 0)
    zpad = jnp.zeros((pad, LANES), F32)
    for sq, h in [(sq, h) for sq in range(n_sub) for h in range(H)]:
        sl = slice(h * DK_A, (h + 1) * DK_A)
        rs = slice(sq * T, (sq + 1) * T)
        q, k, v, lf = q_ref[rs, sl], k_ref[rs, sl], v_ref[rs, sl], lf_ref[rs, sl]
        b = jnp.zeros((T, LANES), F32)
        for t in range(T):
            bt = jnp.sum(jnp.where(row <= t, lf, 0.0), axis=0, keepdims=True)
            b = jnp.where(row == t, bt, b)
        st = s0_ref[sq, h].T
        v_bf = jnp.concatenate([v, zpad], axis=0).astype(BF16)
        qd = jnp.concatenate([q * jnp.exp(b), zpad], axis=0).astype(BF16)
        o = _dot_nt(qd, st.astype(BF16))[:T]
        ones = jnp.ones((LANES, LANES), BF16)
        lane = lax.broadcasted_iota(jnp.int32, (T, LANES), 1)
        pieces = []
        for s in range(T):
            d = jnp.minimum(b - b[s:s + 1, :], 0.0)
            pieces.append(jnp.where(row >= s, q * k[s:s + 1, :] * jnp.exp(d), 0.0))
        r = _dot(jnp.concatenate(pieces, axis=0).astype(BF16), ones)
        a = jnp.zeros((T, LANES), F32)
        for s in range(T):
            a = jnp.where(lane == s, r[s * T:(s + 1) * T, :], a)
        a_bf = jnp.concatenate([a, zpad], axis=0).astype(BF16)
        o_ref[rs, sl] = o + _dot(a_bf, v_bf)[:T]
        b_end = b[T - 1:T, :]
        kt = jnp.concatenate([k * jnp.exp(b_end - b), zpad], axis=0).astype(BF16)
        st_new = st * jnp.exp(b_end) + _dot_tn(v_bf, kt)
        s_ref[sq, h] = st_new.T


def _gla_sample(q, k, v, lf, s0, T, n_sub):
    BT, W = q.shape
    B, H = s0.shape[0], s0.shape[1]
    assert BT == B * T and W == H * DK_A and T % 8 == 0 and T <= GLA_SUB and B % n_sub == 0
    blk = pl.BlockSpec((n_sub * T, W), lambda i: (i, 0))
    sblk = pl.BlockSpec((n_sub, H, DK_A, DV_A), lambda i: (i, 0, 0, 0))
    return pl.pallas_call(
        functools.partial(_gla_sample_kernel, T=T, H=H, n_sub=n_sub),
        out_shape=[jax.ShapeDtypeStruct((BT, W), F32), jax.ShapeDtypeStruct(s0.shape, F32)],
        grid=(B // n_sub,),
        in_specs=[blk, blk, blk, blk, sblk],
        out_specs=[blk, sblk],
        compiler_params=_cparams(("parallel",)),
        name="gla_sample",
    )(q, k, v, lf, s0)


def _alibi_slope(n_heads, h):
    return 2.0 ** (-8.0 * (h + 1) / n_heads)


def _online_update(mi, s, v_bf, m_scr, l_scr, acc_scr):
    reps = s.shape[1] // LANES
    m_prev = m_scr[mi]
    m_new = jnp.maximum(m_prev, jnp.max(s, axis=1, keepdims=True))
    alpha = jnp.exp2(m_prev - m_new)
    p = jnp.exp2(s - jnp.tile(m_new, (1, reps)))
    l_scr[mi] = alpha * l_scr[mi] + jnp.sum(p, axis=1, keepdims=True)
    acc_scr[mi] = alpha * acc_scr[mi] + _dot(p.astype(BF16), v_bf)
    m_scr[mi] = m_new


def _diff_prompt_kernel(qi_ref, kj_ref, flag_ref, q_ref, k_ref, v_ref, kg_ref, base_ref,
                        lq1_ref, lk1_ref, lq2_ref, lk2_ref, o_ref, qxt_scr, acct_scr, m_scr, l_scr, acc_scr,
                        *, B, n_heads, lam_init):
    n = pl.program_id(0)
    p = pl.program_id(1)
    i = qi_ref[p]
    j = kj_ref[p]
    fast = flag_ref[0] != 0
    lane = lax.broadcasted_iota(jnp.int32, (B, LANES), 1)

    def head_slope(g):
        return jnp.where(n == 0, _alibi_slope(n_heads, g), _alibi_slope(n_heads, G_B + g)) * LOG2E

    def q_map(g, c):
        qh = q_ref[:, g * LANES:(g + 1) * LANES]
        return jnp.where((lane < D_B) if c == 0 else (lane >= D_B), qh, 0.0)

    def causal_keep():
        return lax.broadcasted_iota(jnp.int32, (B, B), 1) <= lax.broadcasted_iota(jnp.int32, (B, B), 0)

    def finish(l_of):
        lam = _lam_value(lq1_ref[...], lk1_ref[...], lq2_ref[...], lk2_ref[...], lam_init)
        for g in range(G_B):
            o_ref[:, g * LANES:(g + 1) * LANES] = (acc_scr[2 * g] / l_of(2 * g)
                                                   - lam * (acc_scr[2 * g + 1] / l_of(2 * g + 1)))

    @pl.when(jnp.logical_and(fast, j == 0))
    def _():
        kmax = math.sqrt(D_B) * jnp.max(jnp.abs(kg_ref[...]), axis=-1, keepdims=True)
        rloc = lax.broadcasted_iota(jnp.int32, (B, 1), 0).astype(F32)
        lane1 = lax.broadcasted_iota(jnp.int32, (1, LANES), 1)
        for g in range(G_B):
            slope = head_slope(g)
            c_n0 = _split3_const(_alibi_slope(n_heads, g) * LOG2E)
            c_n1 = _split3_const(_alibi_slope(n_heads, G_B + g) * LOG2E)
            crow = jnp.zeros((1, LANES), F32)
            for t in range(3):
                st = jnp.where(n == 0, c_n0[t], c_n1[t])
                crow = jnp.where(lane1 == t, st * LANES, crow)
                crow = jnp.where(lane1 == 3 + t, st, crow)
            for c in range(2):
                qm = q_map(g, c)
                qn = jnp.sqrt(jnp.sum(qm * qm, axis=1, keepdims=True))
                m1, m2, m3 = _split3(slope * rloc + qn * kmax)
                feat = jnp.where(lane == 6, -m1.astype(F32),
                                 jnp.where(lane == 7, -m2.astype(F32), jnp.where(lane == 8, -m3.astype(F32), crow)))
                qxt_scr[2 * g + c] = jnp.concatenate([qm.T, feat.T], axis=0).astype(BF16)
        acct_scr[...] = jnp.zeros_like(acct_scr)

    def tile_fast(masked):
        a = ((i - j) * (B // LANES)).astype(F32)
        ind = (lax.broadcasted_iota(jnp.int32, (1, LANES), 1) < 3).astype(F32)
        featk = (base_ref[...] - a * ind).astype(BF16)
        kx = jnp.concatenate([k_ref[...].astype(BF16), featk], axis=1)
        v1t = jnp.concatenate([v_ref[...].T.astype(BF16), jnp.ones((ONES_ROWS, B), BF16)], axis=0)
        if masked:
            keep = lax.broadcasted_iota(jnp.int32, (B, B), 0) <= lax.broadcasted_iota(jnp.int32, (B, B), 1)
        for mi in range(2 * G_B):
            st = _dot(kx, qxt_scr[mi])
            if masked:
                st = jnp.where(keep, st, NEG_BIG)
            acct_scr[mi] = acct_scr[mi] + _dot(v1t, jnp.exp2(st).astype(BF16))

    @pl.when(jnp.logical_and(fast, j < i))
    def _():
        tile_fast(False)

    @pl.when(jnp.logical_and(fast, j == i))
    def _():
        tile_fast(True)
        lam = _lam_value(lq1_ref[...], lk1_ref[...], lq2_ref[...], lk2_ref[...], lam_init)
        for g in range(G_B):
            a0, a1 = acct_scr[2 * g], acct_scr[2 * g + 1]
            ot = a0[:LANES] / a0[LANES:LANES + 1] - lam * (a1[:LANES] / a1[LANES:LANES + 1])
            o_ref[:, g * LANES:(g + 1) * LANES] = ot.T

    slow = jnp.logical_not(fast)

    @pl.when(jnp.logical_and(slow, j == 0))
    def _():
        m_scr[...] = jnp.full_like(m_scr, NEG_BIG)
        l_scr[...] = jnp.zeros_like(l_scr)
        acc_scr[...] = jnp.zeros_like(acc_scr)

    def tile_slow(masked):
        k_bf = k_ref[...].astype(BF16)
        v_bf = v_ref[...].astype(BF16)
        col = lax.broadcasted_iota(jnp.int32, (1, B), 1).astype(F32)
        rel = col - ((i - j) * B).astype(F32)
        if masked:
            keep = causal_keep()
        for g in range(G_B):
            bias = head_slope(g) * rel
            for c in range(2):
                s = _dot_nt(q_map(g, c).astype(BF16), k_bf) + bias
                if masked:
                    s = jnp.where(keep, s, NEG_BIG)
                _online_update(g * 2 + c, s, v_bf, m_scr, l_scr, acc_scr)

    @pl.when(jnp.logical_and(slow, j < i))
    def _():
        tile_slow(False)

    @pl.when(jnp.logical_and(slow, j == i))
    def _():
        tile_slow(True)
        finish(lambda mi: l_scr[mi])


def _position_features(B):
    c = np.arange(B)
    f = np.zeros((B, LANES), np.float32)
    f[:, 0:3] = (c >> 7)[:, None]
    f[:, 3:6] = (c & 127)[:, None]
    f[:, 6:9] = 1.0
    return jnp.asarray(f)


def _diff_prompt(qb, kb, vb, k_gain, fast_flag, lq1, lk1, lq2, lk2, lam_init, B):
    T = qb.shape[0]
    n_heads = qb.shape[1] // LANES
    nb = T // B
    assert T % B == 0 and n_heads == KVH_B * G_B and B % LANES == 0 and (nb * B) // LANES <= 256
    qi = np.concatenate([np.full(i + 1, i, np.int32) for i in range(nb)])
    kj = np.concatenate([np.arange(i + 1, dtype=np.int32) for i in range(nb)])
    base = _position_features(B)
    const = lambda shape: pl.BlockSpec(shape, lambda n, p, qi, kj, fl: (0,) * len(shape))
    grid_spec = pltpu.PrefetchScalarGridSpec(
        num_scalar_prefetch=3,
        grid=(KVH_B, len(qi)),
        in_specs=[pl.BlockSpec((B, G_B * LANES), lambda n, p, qi, kj, fl: (qi[p], n)),
                  pl.BlockSpec((B, LANES), lambda n, p, qi, kj, fl: (kj[p], n)),
                  pl.BlockSpec((B, LANES), lambda n, p, qi, kj, fl: (kj[p], n)),
                  const(k_gain.shape), const(base.shape),
                  const(lq1.shape), const(lk1.shape), const(lq2.shape), const(lk2.shape)],
        out_specs=pl.BlockSpec((B, G_B * LANES), lambda n, p, qi, kj, fl: (qi[p], n)),
        scratch_shapes=[pltpu.VMEM((2 * G_B, 2 * LANES, B), BF16),
                        pltpu.VMEM((2 * G_B, LANES + ONES_ROWS, B), F32)] + [pltpu.VMEM((2 * G_B, B, LANES), F32)] * 3,
    )
    return pl.pallas_call(
        functools.partial(_diff_prompt_kernel, B=B, n_heads=n_heads, lam_init=lam_init),
        out_shape=jax.ShapeDtypeStruct(qb.shape, F32),
        grid_spec=grid_spec,
        compiler_params=_cparams(("parallel", "arbitrary")),
        name="diff_prompt",
    )(jnp.asarray(qi), jnp.asarray(kj), fast_flag, qb, kb, vb, k_gain, base, lq1, lk1, lq2, lk2)


def _diff_sample_kernel(pt_ref, flag_ref, q_ref, kn_ref, vn_ref, kg_ref, lq1_ref, lk1_ref, lq2_ref, lk2_ref,
                        ck_ref, cv_ref, o_ref, kbuf, vbuf, ksem, vsem, m_scr, l_scr, acc_scr,
                        *, P, page, n_chunks, n_seq, T, n_heads, lam_init):
    b = pl.program_id(0)
    c = pl.program_id(1)
    step = b * n_chunks + c
    slot = lax.rem(step, 2)
    past = n_chunks * P * page
    R = G_B * 2 * T
    fast = flag_ref[0] != 0

    def k_copy(pg, sl, p):
        return pltpu.make_async_copy(ck_ref.at[pg], kbuf.at[sl, p], ksem.at[sl])

    def v_copy(pg, sl, p):
        return pltpu.make_async_copy(cv_ref.at[pg], vbuf.at[sl, p], vsem.at[sl])

    def issue(bb, cc, sl):
        for p in range(P):
            pg = pt_ref[bb, cc * P + p]
            k_copy(pg, sl, p).start()
            v_copy(pg, sl, p).start()

    @pl.when(step == 0)
    def _():
        issue(0, 0, 0)

    nxt = step + 1

    @pl.when(nxt < n_seq * n_chunks)
    def _():
        issue(nxt // n_chunks, lax.rem(nxt, n_chunks), 1 - slot)

    for p in range(P):
        k_copy(0, slot, p).wait()
        v_copy(0, slot, p).wait()

    lane = lax.broadcasted_iota(jnp.int32, (T, LANES), 1)
    rrow = lax.broadcasted_iota(jnp.int32, (R, 1), 0)
    trow1 = lax.rem(rrow, T).astype(F32)
    col = lax.broadcasted_iota(jnp.int32, (1, P * page), 1)
    rel = (col + (c * (P * page) - past)).astype(F32)

    def lhs_and_slopes(n):
        blocks = []
        for g in range(G_B):
            qh = q_ref[:, (n * G_B + g) * LANES:(n * G_B + g + 1) * LANES]
            for cs in range(2):
                half = (lane < D_B) if cs == 0 else (lane >= D_B)
                blocks.append(jnp.where(half, qh, 0.0))
        lhs = jnp.concatenate(blocks, axis=0)
        slopes = jnp.where(rrow < 2 * T, _alibi_slope(n_heads, n * G_B), _alibi_slope(n_heads, n * G_B + 1)) * LOG2E
        return lhs, slopes

    def cached_kv(n):
        kt = jnp.concatenate([kbuf[slot, p, n * LANES:(n + 1) * LANES, :] for p in range(P)], axis=1).astype(BF16)
        vv = jnp.concatenate([vbuf[slot, p, pl.ds(n, page, stride=KVH_B), :] for p in range(P)], axis=0).astype(BF16)
        return kt, vv

    def new_kv(n):
        zpad = jnp.zeros((LANES - T, LANES), F32)
        kn = jnp.concatenate([kn_ref[:, n * LANES:(n + 1) * LANES], zpad], axis=0).astype(BF16)
        vn = jnp.concatenate([vn_ref[:, n * LANES:(n + 1) * LANES], zpad], axis=0).astype(BF16)
        ncol = lax.broadcasted_iota(jnp.int32, (R, LANES), 1)
        keep = ncol <= lax.rem(lax.broadcasted_iota(jnp.int32, (R, LANES), 0), T)
        return kn, vn, ncol.astype(F32), keep

    def write_out(n, res, lam):
        for g in range(G_B):
            r0 = g * 2 * T
            o_ref[:, (n * G_B + g) * LANES:(n * G_B + g + 1) * LANES] = res[r0:r0 + T] - lam * res[r0 + T:r0 + 2 * T]

    def run(use_bound):
        @pl.when(c == 0)
        def _():
            if not use_bound:
                m_scr[...] = jnp.full_like(m_scr, NEG_BIG)
            l_scr[...] = jnp.zeros_like(l_scr)
            acc_scr[...] = jnp.zeros_like(acc_scr)

        kmax = math.sqrt(D_B) * jnp.max(jnp.abs(kg_ref[...]), axis=-1, keepdims=True)

        def row_shift(lhs, slopes):
            return slopes * trow1 + jnp.sqrt(jnp.sum(lhs * lhs, axis=1, keepdims=True)) * kmax

        def accumulate(n, s, vv):
            if use_bound:
                pr = jnp.exp2(s)
                l_scr[n] = l_scr[n] + _lane_tile_sum(pr)
                acc_scr[n] = acc_scr[n] + _dot(pr.astype(BF16), vv)
            else:
                _online_update(n, s, vv, m_scr, l_scr, acc_scr)

        for n in range(KVH_B):
            lhs, slopes = lhs_and_slopes(n)
            kt, vv = cached_kv(n)
            s = _dot(lhs.astype(BF16), kt) + slopes * rel
            if use_bound:
                s = s - row_shift(lhs, slopes)
            accumulate(n, s, vv)

        @pl.when(c == n_chunks - 1)
        def _():
            lam = _lam_value(lq1_ref[...], lk1_ref[...], lq2_ref[...], lk2_ref[...], lam_init)
            for n in range(KVH_B):
                lhs, slopes = lhs_and_slopes(n)
                kn, vn, ncol, keep = new_kv(n)
                s = _dot_nt(lhs.astype(BF16), kn) + slopes * ncol
                if use_bound:
                    s = s - row_shift(lhs, slopes)
                accumulate(n, jnp.where(keep, s, NEG_BIG), vn)
                den = jnp.sum(l_scr[n], axis=1, keepdims=True) if use_bound else l_scr[n]
                write_out(n, acc_scr[n] / den, lam)

    @pl.when(fast)
    def _():
        run(True)

    @pl.when(jnp.logical_not(fast))
    def _():
        run(False)


def _diff_sample(qb, kb_new, vb_new, page_table, kt_pages, v_pages, k_gain, fast_flag, lq1, lk1, lq2, lk2, lam_init,
                 T, P):
    n_seq, n_pages = page_table.shape
    page = kt_pages.shape[2]
    n_heads = qb.shape[1] // LANES
    assert n_pages % P == 0 and qb.shape[0] == n_seq * T and page == LANES
    n_chunks = n_pages // P
    R = G_B * 2 * T
    small = pl.BlockSpec(lq1.shape, lambda b, c, pt, fl: (0, 0))
    rows = lambda w: pl.BlockSpec((T, w), lambda b, c, pt, fl: (b, 0))
    grid_spec = pltpu.PrefetchScalarGridSpec(
        num_scalar_prefetch=2,
        grid=(n_seq, n_chunks),
        in_specs=[rows(qb.shape[1]), rows(kb_new.shape[1]), rows(vb_new.shape[1]), small, small, small, small, small,
                  pl.BlockSpec(memory_space=pl.ANY), pl.BlockSpec(memory_space=pl.ANY)],
        out_specs=rows(qb.shape[1]),
        scratch_shapes=[pltpu.VMEM((2, P) + kt_pages.shape[1:], F32),
                        pltpu.VMEM((2, P) + v_pages.shape[1:], F32),
                        pltpu.SemaphoreType.DMA((2,)), pltpu.SemaphoreType.DMA((2,)),
                        pltpu.VMEM((KVH_B, R, LANES), F32), pltpu.VMEM((KVH_B, R, LANES), F32),
                        pltpu.VMEM((KVH_B, R, LANES), F32)],
    )
    return pl.pallas_call(
        functools.partial(_diff_sample_kernel, P=P, page=page, n_chunks=n_chunks, n_seq=n_seq, T=T,
                          n_heads=n_heads, lam_init=lam_init),
        out_shape=jax.ShapeDtypeStruct(qb.shape, F32),
        grid_spec=grid_spec,
        compiler_params=_cparams(("arbitrary", "arbitrary")),
        name="diff_sample",
    )(page_table, fast_flag, qb, kb_new, vb_new, k_gain, lq1, lk1, lq2, lk2, kt_pages, v_pages)


def _diff_fused_kernel(qi_ref, kj_ref, pt_ref, q_ref, k_ref, v_ref, kg_ref, base_ref, lq1_ref, lk1_ref, lq2_ref,
                       lk2_ref, qs_ref, kn_ref, vn_ref, ck_ref, cv_ref, o_ref, os_ref,
                       qxt_scr, acct_scr, kbuf, vbuf, ksem, vsem, ls_scr, accs_scr,
                       *, B, P, CH, page, n_chunks, n_seq, T, n_heads, lam_init):
    n = pl.program_id(0)
    p = pl.program_id(1)
    i = qi_ref[p]
    j = kj_ref[p]
    lam = _lam_value(lq1_ref[...], lk1_ref[...], lq2_ref[...], lk2_ref[...], lam_init)
    kmax = math.sqrt(D_B) * jnp.max(jnp.abs(kg_ref[...]), axis=-1, keepdims=True)

    total = n_seq * n_chunks
    step = n * pl.num_programs(1) + p
    group = lax.rem(step, 2)
    past = n_chunks * P * page
    R = G_B * 2 * T

    def k_copy(pg, sl, pp):
        return pltpu.make_async_copy(ck_ref.at[pg], kbuf.at[sl, pp], ksem.at[sl])

    def v_copy(pg, sl, pp):
        return pltpu.make_async_copy(cv_ref.at[pg], vbuf.at[sl, pp], vsem.at[sl])

    def issue(chunk, sl):
        bb = chunk // n_chunks
        cc = lax.rem(chunk, n_chunks)
        for pp in range(P):
            pg = pt_ref[bb, cc * P + pp]
            k_copy(pg, sl, pp).start()
            v_copy(pg, sl, pp).start()

    def issue_step(s, grp):
        for u in range(CH):
            issue(jnp.minimum(s * CH + u, total - 1), grp * CH + u)

    def wait_group(grp):
        for u in range(CH):
            for pp in range(P):
                k_copy(0, grp * CH + u, pp).wait()
                v_copy(0, grp * CH + u, pp).wait()

    @pl.when(step == 0)
    def _():
        issue_step(0, 0)
        issue_step(1, 1)

    wait_group(group)

    def prefetch_ahead():
        issue_step(step + 2, group)

    def sample_chunk(u, write):
        cid = step * CH + u
        slot = group * CH + u
        valid = cid < total
        c = lax.rem(cid, n_chunks)
        first = jnp.logical_and(valid, c == 0)
        gate = valid.astype(F32)
        lane = lax.broadcasted_iota(jnp.int32, (T, LANES), 1)
        rrow = lax.broadcasted_iota(jnp.int32, (R, 1), 0)
        trow1 = lax.rem(rrow, T).astype(F32)
        col = lax.broadcasted_iota(jnp.int32, (1, P * page), 1)
        rel = (col + (c * (P * page) - past)).astype(F32)
        zpad = jnp.zeros((LANES - T, LANES), F32)
        ncol = lax.broadcasted_iota(jnp.int32, (R, LANES), 1)
        keep_new = ncol <= lax.rem(lax.broadcasted_iota(jnp.int32, (R, LANES), 0), T)
        for nh in range(KVH_B):
            blocks = []
            for g in range(G_B):
                qh = qs_ref[:, (nh * G_B + g) * LANES:(nh * G_B + g + 1) * LANES]
                for cs in range(2):
                    blocks.append(jnp.where((lane < D_B) if cs == 0 else (lane >= D_B), qh, 0.0))
            lhs = jnp.concatenate(blocks, axis=0)
            slopes = jnp.where(rrow < 2 * T, _alibi_slope(n_heads, nh * G_B),
                               _alibi_slope(n_heads, nh * G_B + 1)) * LOG2E
            shift = slopes * trow1 + jnp.sqrt(jnp.sum(lhs * lhs, axis=1, keepdims=True)) * kmax
            lhs_bf = lhs.astype(BF16)
            kt = jnp.concatenate([kbuf[slot, pp, nh * LANES:(nh + 1) * LANES, :] for pp in range(P)],
                                 axis=1).astype(BF16)
            vv = jnp.concatenate([vbuf[slot, pp, pl.ds(nh, page, stride=KVH_B), :] for pp in range(P)],
                                 axis=0).astype(BF16)
            pr = jnp.exp2(_dot(lhs_bf, kt) + (slopes * rel - shift))
            l_new = jnp.where(first, 0.0, ls_scr[nh]) + gate * _lane_tile_sum(pr)
            acc_new = jnp.where(first, 0.0, accs_scr[nh]) + gate * _dot(pr.astype(BF16), vv)
            ls_scr[nh] = l_new
            accs_scr[nh] = acc_new
            if not write:
                continue
            kn = jnp.concatenate([kn_ref[:, nh * LANES:(nh + 1) * LANES], zpad], axis=0).astype(BF16)
            vn = jnp.concatenate([vn_ref[:, nh * LANES:(nh + 1) * LANES], zpad], axis=0).astype(BF16)
            s_new = _dot_nt(lhs_bf, kn) + (slopes * ncol.astype(F32) - shift)
            pn = jnp.exp2(jnp.where(keep_new, s_new, NEG_BIG))
            den = jnp.sum(l_new + pn, axis=1, keepdims=True)
            res = (acc_new + _dot(pn.astype(BF16), vn)) / den
            for g in range(G_B):
                r0 = g * 2 * T
                os_ref[:, (nh * G_B + g) * LANES:(nh * G_B + g + 1) * LANES] = (
                    res[r0:r0 + T] - lam * res[r0 + T:r0 + 2 * T])

    lane_b = lax.broadcasted_iota(jnp.int32, (B, LANES), 1)

    @pl.when(j == 0)
    def _():
        rloc = lax.broadcasted_iota(jnp.int32, (B, 1), 0).astype(F32)
        lane1 = lax.broadcasted_iota(jnp.int32, (1, LANES), 1)
        for g in range(G_B):
            slope = jnp.where(n == 0, _alibi_slope(n_heads, g), _alibi_slope(n_heads, G_B + g)) * LOG2E
            c_n0 = _split3_const(_alibi_slope(n_heads, g) * LOG2E)
            c_n1 = _split3_const(_alibi_slope(n_heads, G_B + g) * LOG2E)
            crow = jnp.zeros((1, LANES), F32)
            for t in range(3):
                st = jnp.where(n == 0, c_n0[t], c_n1[t])
                crow = jnp.where(lane1 == t, st * LANES, crow)
                crow = jnp.where(lane1 == 3 + t, st, crow)
            qh = q_ref[:, g * LANES:(g + 1) * LANES]
            for c in range(2):
                qm = jnp.where((lane_b < D_B) if c == 0 else (lane_b >= D_B), qh, 0.0)
                qn = jnp.sqrt(jnp.sum(qm * qm, axis=1, keepdims=True))
                m1, m2, m3 = _split3(slope * rloc + qn * kmax)
                feat = jnp.where(lane_b == 6, -m1.astype(F32),
                                 jnp.where(lane_b == 7, -m2.astype(F32),
                                           jnp.where(lane_b == 8, -m3.astype(F32), crow)))
                qxt_scr[2 * g + c] = jnp.concatenate([qm.T, feat.T], axis=0).astype(BF16)
        acct_scr[...] = jnp.zeros_like(acct_scr)

    def prompt_tile(masked):
        a = ((i - j) * (B // LANES)).astype(F32)
        ind = (lax.broadcasted_iota(jnp.int32, (1, LANES), 1) < 3).astype(F32)
        featk = (base_ref[...] - a * ind).astype(BF16)
        kx = jnp.concatenate([k_ref[...].astype(BF16), featk], axis=1)
        v1t = jnp.concatenate([v_ref[...].T.astype(BF16), jnp.ones((ONES_ROWS, B), BF16)], axis=0)
        if masked:
            keep = lax.broadcasted_iota(jnp.int32, (B, B), 0) <= lax.broadcasted_iota(jnp.int32, (B, B), 1)
        for mi in range(2 * G_B):
            st = _dot(kx, qxt_scr[mi])
            if masked:
                st = jnp.where(keep, st, NEG_BIG)
            acct_scr[mi] = acct_scr[mi] + _dot(v1t, jnp.exp2(st).astype(BF16))

    @pl.when(j < i)
    def _():
        prompt_tile(False)
        for u in range(CH):
            sample_chunk(u, u == CH - 1)
        prefetch_ahead()

    @pl.when(j == i)
    def _():
        prompt_tile(True)
        for u in range(CH):
            sample_chunk(u, u == CH - 1)
        for g in range(G_B):
            a0, a1 = acct_scr[2 * g], acct_scr[2 * g + 1]
            ot = a0[:LANES] / a0[LANES:LANES + 1] - lam * (a1[:LANES] / a1[LANES:LANES + 1])
            o_ref[:, g * LANES:(g + 1) * LANES] = ot.T
        prefetch_ahead()

    @pl.when(step == pl.num_programs(0) * pl.num_programs(1) - 1)
    def _():
        wait_group(0)
        wait_group(1)


def _diff_fused(qb, kb, vb, qs, kb_new, vb_new, page_table, kt_pages, v_pages, k_gain, lq1, lk1, lq2, lk2,
                lam_init, B, T, P, CH):
    Tp = qb.shape[0]
    n_heads = qb.shape[1] // LANES
    nb = Tp // B
    n_seq, n_pages = page_table.shape
    page = kt_pages.shape[2]
    n_chunks = n_pages // P
    qi = np.concatenate([np.full(i + 1, i, np.int32) for i in range(nb)])
    kj = np.concatenate([np.arange(i + 1, dtype=np.int32) for i in range(nb)])
    n_pairs = len(qi)
    assert Tp % B == 0 and n_pages % P == 0 and page == LANES and qs.shape[0] == n_seq * T
    assert n_chunks % CH == 0 and KVH_B * n_pairs * CH >= n_seq * n_chunks, "every page chunk needs a grid step"
    base = _position_features(B)
    R = G_B * 2 * T
    const = lambda a: pl.BlockSpec(a.shape, lambda n, p, qi, kj, pt: (0,) * a.ndim)
    seq_of = lambda n, p: jnp.minimum(((n * n_pairs + p) * CH) // n_chunks, n_seq - 1)
    srow = lambda w: pl.BlockSpec((T, w), lambda n, p, qi, kj, pt: (seq_of(n, p), 0))
    grid_spec = pltpu.PrefetchScalarGridSpec(
        num_scalar_prefetch=3,
        grid=(KVH_B, n_pairs),
        in_specs=[pl.BlockSpec((B, G_B * LANES), lambda n, p, qi, kj, pt: (qi[p], n)),
                  pl.BlockSpec((B, LANES), lambda n, p, qi, kj, pt: (kj[p], n)),
                  pl.BlockSpec((B, LANES), lambda n, p, qi, kj, pt: (kj[p], n)),
                  const(k_gain), const(base), const(lq1), const(lk1), const(lq2), const(lk2),
                  srow(qs.shape[1]), srow(kb_new.shape[1]), srow(vb_new.shape[1]),
                  pl.BlockSpec(memory_space=pl.ANY), pl.BlockSpec(memory_space=pl.ANY)],
        out_specs=[pl.BlockSpec((B, G_B * LANES), lambda n, p, qi, kj, pt: (qi[p], n)), srow(qs.shape[1])],
        scratch_shapes=[pltpu.VMEM((2 * G_B, 2 * LANES, B), BF16),
                        pltpu.VMEM((2 * G_B, LANES + ONES_ROWS, B), F32),
                        pltpu.VMEM((2 * CH, P) + kt_pages.shape[1:], F32),
                        pltpu.VMEM((2 * CH, P) + v_pages.shape[1:], F32),
                        pltpu.SemaphoreType.DMA((2 * CH,)), pltpu.SemaphoreType.DMA((2 * CH,)),
                        pltpu.VMEM((KVH_B, R, LANES), F32), pltpu.VMEM((KVH_B, R, LANES), F32)],
    )
    return pl.pallas_call(
        functools.partial(_diff_fused_kernel, B=B, P=P, CH=CH, page=page, n_chunks=n_chunks, n_seq=n_seq, T=T,
                          n_heads=n_heads, lam_init=lam_init),
        out_shape=[jax.ShapeDtypeStruct(qb.shape, F32), jax.ShapeDtypeStruct(qs.shape, F32)],
        grid_spec=grid_spec,
        compiler_params=_cparams(("arbitrary", "arbitrary")),
        name="diff_fused",
    )(jnp.asarray(qi), jnp.asarray(kj), page_table, qb, kb, vb, k_gain, base, lq1, lk1, lq2, lk2,
      qs, kb_new, vb_new, kt_pages, v_pages)


def _outproj_a_kernel(x_ref, oa_ref, gsa_ref, ob_ref, gsb_ref, og_ref, sg_ref, w_ref, y_ref, *, lam_init):
    def branch(o_ref, gate_ref, gain, scale):
        parts = []
        for h in range(o_ref.shape[1] // LANES):
            sl = slice(h * LANES, (h + 1) * LANES)
            parts.append(_rms_rows(o_ref[:, sl], gain) * scale * gate_ref[:, sl])
        return jnp.concatenate(parts, axis=1).astype(BF16)

    ya = branch(oa_ref, gsa_ref, og_ref[...], 1.0)
    yb = branch(ob_ref, gsb_ref, sg_ref[...], 1.0 - lam_init)
    wa = oa_ref.shape[1]
    y_ref[...] = x_ref[...] + _dot(ya, w_ref[0:wa, :]) + _dot(yb, w_ref[wa:, :])


def _outproj_a(x, oa, gsa, ob, gsb, o_gain, subln, w_bf16, lam_init, tm):
    T, D = x.shape
    row = lambda w: pl.BlockSpec((tm, w), lambda i: (i, 0))
    return pl.pallas_call(
        functools.partial(_outproj_a_kernel, lam_init=lam_init),
        out_shape=jax.ShapeDtypeStruct((T, D), F32),
        grid=(T // tm,),
        in_specs=[row(D), row(oa.shape[1]), row(oa.shape[1]), row(ob.shape[1]), row(ob.shape[1]),
                  _full(o_gain.shape), _full(subln.shape), _full(w_bf16.shape)],
        out_specs=row(D),
        compiler_params=_cparams(("parallel",)),
        name="outproj_a",
    )(x, oa, gsa, ob, gsb, o_gain, subln, w_bf16)


def _inproj_c_kernel(x_ref, g_ref, w_ref, qg_ref, kg_ref, bd_ref, q_ref, k_ref, v_ref, gs_ref, *, wq, wkv):
    hb = _rms_rows(x_ref[...], g_ref[...]).astype(BF16)
    bd = bd_ref[...]
    q_ref[...] = _group_rms(_dot(hb, w_ref[:, 0:wq]), qg_ref[...], bd) * (HD_C ** -0.5 * LOG2E)
    k_ref[...] = _group_rms(_dot(hb, w_ref[:, wq:wq + wkv]), kg_ref[...], bd)
    v_ref[...] = _dot(hb, w_ref[:, wq + wkv:wq + 2 * wkv])
    gs_ref[...] = _silu(_dot(hb, w_ref[:, wq + 2 * wkv:])).astype(gs_ref.dtype)


def _inproj_c(x, g, w_bf16, q_gain_t, k_gain_t, bd, tm):
    T, D = x.shape
    wq, wkv = q_gain_t.shape[1], k_gain_t.shape[1]
    widths = (wq, wkv, wkv, wq)
    dtypes = (F32, F32, F32, BF16)
    row = lambda w: pl.BlockSpec((tm, w), lambda i: (i, 0))
    return pl.pallas_call(
        functools.partial(_inproj_c_kernel, wq=wq, wkv=wkv),
        out_shape=[jax.ShapeDtypeStruct((T, w), d) for w, d in zip(widths, dtypes)],
        grid=(T // tm,),
        in_specs=[row(D), _full(g.shape), _full(w_bf16.shape), _full(q_gain_t.shape), _full(k_gain_t.shape),
                  _full(bd.shape)],
        out_specs=[row(w) for w in widths],
        compiler_params=_cparams(("parallel",)),
        name="inproj_c",
    )(x, g, w_bf16, q_gain_t, k_gain_t, bd)


def _swa_softmax_pv(s, sink, v_list):
    m = sink
    for piece in s:
        m = jnp.maximum(m, jnp.max(piece, axis=1, keepdims=True))
    den = jnp.exp2(sink - m)
    out = None
    for piece, pv in zip(s, v_list):
        p = jnp.exp2(piece - m)
        den = den + jnp.sum(p, axis=1, keepdims=True)
        term = pv(p.astype(BF16))
        out = term if out is None else out + term
    return out / den


def _swa_prompt_kernel(flag_ref, q_ref, kc_ref, kp_ref, vc_ref, vp_ref, snk_ref, qg_ref, kg_ref, o_ref, *, n_heads):
    i = pl.program_id(0)
    W = q_ref.shape[0]
    n_pairs = n_heads // KVH_C
    half_rows = n_pairs * W
    lane = lax.broadcasted_iota(jnp.int32, (W, LANES), 1)
    blocks = []
    for n in range(KVH_C):
        half = (lane < HD_C) if n == 0 else (lane >= HD_C)
        for m in range(n_pairs):
            blocks.append(jnp.where(half, q_ref[:, m * LANES:(m + 1) * LANES], 0.0))
    lhs = jnp.concatenate(blocks, axis=0).astype(BF16)
    keys = jnp.concatenate([kp_ref[...], kc_ref[...]], axis=0).astype(BF16)
    vals = jnp.concatenate([vp_ref[...], vc_ref[...]], axis=0).astype(BF16)
    r = lax.broadcasted_iota(jnp.int32, (W, 2 * W), 0)
    c = lax.broadcasted_iota(jnp.int32, (W, 2 * W), 1)
    dist = r + W - c
    valid = jnp.logical_and(jnp.logical_and(dist >= 0, dist <= W), jnp.logical_or(c >= W, i > 0))
    neg_dist = jnp.where(valid, -dist.astype(F32), NEG_BIG)

    def write(res):
        for m in range(n_pairs):
            lo = res[m * W:(m + 1) * W]
            hi = res[half_rows + m * W:half_rows + (m + 1) * W]
            o_ref[:, m * LANES:(m + 1) * LANES] = jnp.where(lane < HD_C, lo, hi)

    @pl.when(flag_ref[0] != 0)
    def _():
        bound = ((LOG2E * math.sqrt(HD_C)) * jnp.max(jnp.abs(qg_ref[...]), axis=-1, keepdims=True)
                 * jnp.max(jnp.abs(kg_ref[...]), axis=-1, keepdims=True))
        biases, sink_terms = [], []
        for h in range(n_heads):
            sink_h = snk_ref[:, h:h + 1] * LOG2E
            shift = jnp.maximum(bound, sink_h)
            biases.append((_alibi_slope(n_heads, h) * LOG2E) * neg_dist - shift)
            sink_terms.append(jnp.broadcast_to(jnp.exp2(sink_h - shift), (W, LANES)))
        pr = jnp.exp2(_dot_nt(lhs, keys) + jnp.concatenate(biases, axis=0))
        vals1 = jnp.concatenate([vals, jnp.ones_like(vals)], axis=1)
        res = _dot(pr.astype(BF16), vals1)
        den = res[:, LANES:] + jnp.concatenate(sink_terms, axis=0)
        write(res[:, :LANES] / den)

    @pl.when(flag_ref[0] == 0)
    def _():
        slopes, sinks = [], []
        for h in range(n_heads):
            slopes.append(jnp.full((W, 1), _alibi_slope(n_heads, h) * LOG2E, F32))
            sinks.append(jnp.broadcast_to(snk_ref[:, h:h + 1] * LOG2E, (W, 1)))
        slope = jnp.concatenate(slopes, axis=0)
        sink = jnp.concatenate(sinks, axis=0)
        s = _dot_nt(lhs, keys) + slope * jnp.tile(neg_dist, (n_heads, 1))
        m_row = jnp.maximum(jnp.max(s, axis=1, keepdims=True), sink)
        pr = jnp.exp2(s - m_row)
        den = jnp.sum(pr, axis=1, keepdims=True) + jnp.exp2(sink - m_row)
        write(_dot(pr.astype(BF16), vals) / den)


def _swa_prompt(q, k, v, sinks, q_gain, k_gain, fast_flag):
    T, Wq = q.shape
    n_heads = Wq // HD_C
    nb = T // WINDOW
    cur = lambda w: pl.BlockSpec((WINDOW, w), lambda i, fl: (i, 0))
    prev = lambda w: pl.BlockSpec((WINDOW, w), lambda i, fl: (jnp.maximum(i - 1, 0), 0))
    const = lambda a: pl.BlockSpec(a.shape, lambda i, fl: (0,) * a.ndim)
    kw = k.shape[1]
    grid_spec = pltpu.PrefetchScalarGridSpec(
        num_scalar_prefetch=1,
        grid=(nb,),
        in_specs=[cur(Wq), cur(kw), prev(kw), cur(kw), prev(kw), const(sinks), const(q_gain), const(k_gain)],
        out_specs=cur(Wq),
    )
    return pl.pallas_call(
        functools.partial(_swa_prompt_kernel, n_heads=n_heads),
        out_shape=jax.ShapeDtypeStruct(q.shape, F32),
        grid_spec=grid_spec,
        compiler_params=_cparams(("parallel",)),
        name="swa_prompt",
    )(fast_flag, q, k, k, v, v, sinks, q_gain, k_gain)


def _swa_sample_kernel(q_ref, kn_ref, vn_ref, ckt_ref, cvt_ref, snk_ref, o_ref, *, T, n_heads, n_sub):
    n_pairs = n_heads // KVH_C
    R = n_heads * T
    lane = lax.broadcasted_iota(jnp.int32, (T, LANES), 1)
    rr = lax.broadcasted_iota(jnp.int32, (R, WINDOW), 0)
    cc = lax.broadcasted_iota(jnp.int32, (R, WINDOW), 1)
    t = lax.rem(rr, T)
    hrow = lax.broadcasted_iota(jnp.int32, (R, 1), 0) // T
    slope = jnp.exp2(-8.0 * (hrow + 1).astype(F32) / n_heads) * LOG2E
    sink = jnp.zeros((R, 1), F32)
    for h in range(n_heads):
        sink = jnp.where(hrow == h, snk_ref[:, h:h + 1], sink)
    sink = sink * LOG2E
    bias_c = jnp.where(cc >= t, -slope * (t + WINDOW - cc).astype(F32), NEG_BIG)
    bias_n = jnp.where(cc <= t, -slope * (t - cc).astype(F32), NEG_BIG)
    zpad = jnp.zeros((WINDOW - T, LANES), F32)
    half_rows = n_pairs * T
    for sq in range(n_sub):
        rows = slice(sq * T, (sq + 1) * T)
        blocks = []
        for n in range(KVH_C):
            half = (lane < HD_C) if n == 0 else (lane >= HD_C)
            for m in range(n_pairs):
                blocks.append(jnp.where(half, q_ref[rows, m * LANES:(m + 1) * LANES], 0.0))
        lhs = jnp.concatenate(blocks, axis=0).astype(BF16)
        kn = jnp.concatenate([kn_ref[rows, :], zpad], axis=0).astype(BF16)
        vn = jnp.concatenate([vn_ref[rows, :], zpad], axis=0).astype(BF16)
        ckt = ckt_ref[sq].astype(BF16)
        cvt = cvt_ref[sq].astype(BF16)
        s_c = _dot(lhs, ckt) + bias_c
        s_n = _dot_nt(lhs, kn) + bias_n
        res = _swa_softmax_pv([s_c, s_n], sink, [lambda p: _dot_nt(p, cvt), lambda p: _dot(p, vn)])
        for m in range(n_pairs):
            lo = res[m * T:(m + 1) * T]
            hi = res[half_rows + m * T:half_rows + (m + 1) * T]
            o_ref[rows, m * LANES:(m + 1) * LANES] = jnp.where(lane < HD_C, lo, hi)


def _swa_sample(q, k_new, v_new, ckt, cvt, sinks, T, n_sub):
    BT, Wq = q.shape
    B = ckt.shape[0]
    n_heads = Wq // HD_C
    assert B % n_sub == 0
    rows = lambda w: pl.BlockSpec((n_sub * T, w), lambda b: (b, 0))
    cache = pl.BlockSpec((n_sub,) + ckt.shape[1:], lambda b: (b, 0, 0))
    return pl.pallas_call(
        functools.partial(_swa_sample_kernel, T=T, n_heads=n_heads, n_sub=n_sub),
        out_shape=jax.ShapeDtypeStruct(q.shape, F32),
        grid=(B // n_sub,),
        in_specs=[rows(Wq), rows(k_new.shape[1]), rows(v_new.shape[1]), cache, cache, _full(sinks.shape)],
        out_specs=rows(Wq),
        compiler_params=_cparams(("parallel",)),
        name="swa_sample",
    )(q, k_new, v_new, ckt, cvt, sinks)


def _outproj_c_kernel(x_ref, o_ref, gs_ref, w_ref, y_ref):
    y_ref[...] = x_ref[...] + _dot((o_ref[...] * gs_ref[...]).astype(BF16), w_ref[...])


def _outproj_c(x, o, gs, w_bf16, tm):
    T, D = x.shape
    row = lambda w: pl.BlockSpec((tm, w), lambda i: (i, 0))
    return pl.pallas_call(
        _outproj_c_kernel,
        out_shape=jax.ShapeDtypeStruct((T, D), F32),
        grid=(T // tm,),
        in_specs=[row(D), row(o.shape[1]), row(o.shape[1]), _full(w_bf16.shape)],
        out_specs=row(D),
        compiler_params=_cparams(("parallel",)),
        name="outproj_c",
    )(x, o, gs, w_bf16)


TOKEN_TILE = 512
GLA_CHUNK = 128
DIFF_BLOCK = 1024
DIFF_PAGES_PER_STEP = 32
RIDE_CHUNKS_PER_STEP = 2
SWA_SEQS_PER_STEP = 8
GLA_SEQS_PER_STEP = 4
GLA_HEADS_PER_STEP = 4


def _pair_perm(n_heads):
    n_pairs = n_heads // KVH_C
    cols = []
    for m in range(n_pairs):
        for n in range(KVH_C):
            h = n * n_pairs + m
            cols.extend(range(h * HD_C, (h + 1) * HD_C))
    return np.asarray(cols, np.int32)


def kernel(x_prompt, x_sample, state_hgrn, cache_k_diff, cache_v_diff, cache_k_swa, cache_v_swa, page_table, norm_a, w_in_a, w_out_a, lb_logits, hgrn_out_gain, diff_q_gain, diff_k_gain, diff_subln_gain, lam_q1, lam_k1, lam_q2, lam_k2, norm_c, w_in_c, w_out_c, swa_q_gain, swa_k_gain, sinks):
    batch, seq, d_model = x_prompt.shape
    n_seq, t_dec, _ = x_sample.shape
    assert batch == 1
    n_even, n_odd = norm_a.shape[0], norm_c.shape[0]
    depth = n_even + n_odd
    h_a = state_hgrn.shape[2]
    n_pool, page = cache_k_diff.shape[1], cache_k_diff.shape[2]
    w_b = w_out_a.shape[1] - h_a * DV_A
    h_b = w_b // (2 * D_B)
    h_c = sinks.shape[1]

    xp = x_prompt.reshape(seq, d_model)
    xs = x_sample.reshape(n_seq * t_dec, d_model)
    tile_p, tile_s = min(TOKEN_TILE, seq), min(TOKEN_TILE, n_seq * t_dec)
    bd64 = _block_diag_mean(2 * LANES, D_B)
    bd64_c = _block_diag_mean(LANES, HD_C)
    perm = _pair_perm(h_c)

    hs_p, hs_s, kdp, vdp, kds, vds, ksp, vsp, kss, vss = ([] for _ in range(10))
    for l in range(depth):
        if l % 2 == 0:
            e = l // 2
            lam_init = 0.8 - 0.6 * math.exp(-0.3 * l)
            w_in = w_in_a[e].astype(BF16)
            w_out = w_out_a[e].astype(BF16)
            g = norm_a[e:e + 1]
            qg = jnp.tile(diff_q_gain[e:e + 1], (1, h_b * 2))
            kg = jnp.tile(diff_k_gain[e:e + 1], (1, KVH_B * 2))
            lam_args = (lam_q1[e:e + 1], lam_k1[e:e + 1], lam_q2[e:e + 1], lam_k2[e:e + 1])
            kt_pages = jnp.transpose(cache_k_diff[e], (0, 2, 3, 4, 1)).reshape(n_pool, KVH_B * 2 * D_B, page)
            v_pages = cache_v_diff[e].reshape(n_pool, page * KVH_B, 2 * D_B)

            pa = _inproj_a(xp, g, w_in, lb_logits, qg, kg, bd64, e, tile_p)
            sa = _inproj_a(xs, g, w_in, lb_logits, qg, kg, bd64, e, tile_s)
            qa_p, lf_p, kk_p, ia_p, gsa_p, qb_p, kb_p, vb_p, gsb_p = pa
            qa_s, lf_s, kk_s, ia_s, gsa_s, qb_s, kb_s, vb_s, gsb_s = sa

            lb_min = jnp.min(jnp.cumsum(jax.nn.softmax(lb_logits.astype(F32), axis=0), axis=0)[e])
            gla_flag = (-GLA_SUB * jnp.log(lb_min) <= GLA_FAST_DECAY).astype(jnp.int32).reshape(1)
            oa_p, st_p = _gla_prompt(qa_p, kk_p, ia_p, lf_p, gla_flag, GLA_CHUNK, GLA_HEADS_PER_STEP)
            oa_s, st_s = _gla_sample(qa_s, kk_s, ia_s, lf_s, state_hgrn[e], t_dec, min(GLA_SEQS_PER_STEP, n_seq))
            logit_bound = (LOG2E * math.sqrt(D_B)) * jnp.max(jnp.abs(diff_q_gain[e])) * jnp.max(jnp.abs(diff_k_gain[e]))
            fast_flag = (logit_bound <= FAST_LOGIT_BOUND).astype(jnp.int32).reshape(1)
            k_gain = diff_k_gain[e:e + 1]
            def diff_separate():
                return (_diff_prompt(qb_p, kb_p, vb_p, k_gain, fast_flag, *lam_args, lam_init, DIFF_BLOCK),
                        _diff_sample(qb_s, kb_s, vb_s, page_table, kt_pages, v_pages, k_gain, fast_flag, *lam_args,
                                     lam_init, t_dec, DIFF_PAGES_PER_STEP))

            n_pages = page_table.shape[1]
            n_blk = seq // DIFF_BLOCK if seq % DIFF_BLOCK == 0 else 0
            n_ride = 0
            if n_pages % (DIFF_PAGES_PER_STEP * RIDE_CHUNKS_PER_STEP) == 0:
                steps_per_seq = n_pages // (DIFF_PAGES_PER_STEP * RIDE_CHUNKS_PER_STEP)
                n_ride = min(n_seq, (KVH_B * n_blk * (n_blk + 1) // 2) // steps_per_seq)

            def diff_fused():
                rows = n_ride * t_dec
                ob_p_f, ob_s_f = _diff_fused(qb_p, kb_p, vb_p, qb_s[:rows], kb_s[:rows], vb_s[:rows],
                                             page_table[:n_ride], kt_pages, v_pages, k_gain, *lam_args, lam_init,
                                             DIFF_BLOCK, t_dec, DIFF_PAGES_PER_STEP, RIDE_CHUNKS_PER_STEP)
                if n_ride == n_seq:
                    return ob_p_f, ob_s_f
                rest = _diff_sample(qb_s[rows:], kb_s[rows:], vb_s[rows:], page_table[n_ride:], kt_pages, v_pages,
                                    k_gain, fast_flag, *lam_args, lam_init, t_dec, DIFF_PAGES_PER_STEP)
                return ob_p_f, jnp.concatenate([ob_s_f, rest], axis=0)

            if n_ride > 0:
                ob_p, ob_s = lax.cond(fast_flag[0] != 0, diff_fused, diff_separate)
            else:
                ob_p, ob_s = diff_separate()

            og, sg = hgrn_out_gain[e:e + 1], diff_subln_gain[e:e + 1]
            xp = _outproj_a(xp, oa_p, gsa_p, ob_p, gsb_p, og, sg, w_out, lam_init, tile_p)
            xs = _outproj_a(xs, oa_s, gsa_s, ob_s, gsb_s, og, sg, w_out, lam_init, tile_s)

            hs_p.append(st_p[None])
            hs_s.append(st_s)
            kdp.append(kb_p.reshape(batch, seq, KVH_B, 2, D_B))
            vdp.append(vb_p.reshape(batch, seq, KVH_B, 2 * D_B))
            kds.append(kb_s.reshape(n_seq, t_dec, KVH_B, 2, D_B))
            vds.append(vb_s.reshape(n_seq, t_dec, KVH_B, 2 * D_B))
        else:
            o = l // 2
            wq = h_c * HD_C
            wkv = KVH_C * HD_C
            w_full = w_in_c[o]
            w_in = jnp.concatenate([w_full[:, :wq][:, perm], w_full[:, wq:wq + 2 * wkv],
                                    w_full[:, wq + 2 * wkv:][:, perm]], axis=1).astype(BF16)
            w_out = w_out_c[o][perm, :].astype(BF16)
            g = norm_c[o:o + 1]
            qg = jnp.tile(swa_q_gain[o:o + 1], (1, h_c))
            kg = jnp.tile(swa_k_gain[o:o + 1], (1, KVH_C))
            snk = sinks[o:o + 1]
            ckt = jnp.transpose(cache_k_swa[o], (0, 2, 3, 1)).reshape(n_seq, wkv, WINDOW)
            cvt = jnp.transpose(cache_v_swa[o], (0, 2, 3, 1)).reshape(n_seq, wkv, WINDOW)

            q_p, k_p, v_p, gs_p = _inproj_c(xp, g, w_in, qg, kg, bd64_c, tile_p)
            q_s, k_s, v_s, gs_s = _inproj_c(xs, g, w_in, qg, kg, bd64_c, tile_s)
            swa_bound = ((LOG2E * math.sqrt(HD_C)) * jnp.max(jnp.abs(swa_q_gain[o]))
                         * jnp.max(jnp.abs(swa_k_gain[o])))
            swa_flag = (swa_bound <= FAST_LOGIT_BOUND).astype(jnp.int32).reshape(1)
            o_p = _swa_prompt(q_p, k_p, v_p, snk, swa_q_gain[o:o + 1], swa_k_gain[o:o + 1], swa_flag)
            o_s = _swa_sample(q_s, k_s, v_s, ckt, cvt, snk, t_dec, min(SWA_SEQS_PER_STEP, n_seq))
            xp = _outproj_c(xp, o_p, gs_p, w_out, tile_p)
            xs = _outproj_c(xs, o_s, gs_s, w_out, tile_s)

            ksp.append(k_p[-WINDOW:].reshape(batch, WINDOW, KVH_C, HD_C))
            vsp.append(v_p[-WINDOW:].reshape(batch, WINDOW, KVH_C, HD_C))
            k_new = k_s.reshape(n_seq, t_dec, KVH_C, HD_C)
            v_new = v_s.reshape(n_seq, t_dec, KVH_C, HD_C)
            kss.append(jnp.concatenate([cache_k_swa[o], k_new], axis=1)[:, -WINDOW:])
            vss.append(jnp.concatenate([cache_v_swa[o], v_new], axis=1)[:, -WINDOW:])
    return (xp.reshape(batch, seq, d_model), xs.reshape(n_seq, t_dec, d_model),
            jnp.stack(hs_p), jnp.stack(hs_s), jnp.stack(kdp), jnp.stack(vdp), jnp.stack(kds), jnp.stack(vds),
            jnp.stack(ksp), jnp.stack(vsp), jnp.stack(kss), jnp.stack(vss))
```

```python
import functools
import math

import jax
import jax.numpy as jnp
import ml_dtypes
import numpy as np
from jax import lax
from jax.experimental import pallas as pl
from jax.experimental.pallas import tpu as pltpu

F32 = jnp.float32
BF16 = jnp.bfloat16

EPS = 1e-6
LOG2E = 1.4426950408889634
NEG_BIG = -1e30
LANES = 128
VMEM_LIMIT_BYTES = 56 * 1024 * 1024

DK_A = 128
DV_A = 128
D_B = 64
KVH_B = 2
G_B = 2
HD_C = 64
KVH_C = 2
WINDOW = 128
GLA_SUB = 16
FAST_LOGIT_BOUND = 60.0
GLA_FAST_DECAY = 80.0
ONES_ROWS = 16


def _dot(a, b):
    return jnp.dot(a, b, preferred_element_type=F32)


def _dot_nt(a, b):
    return lax.dot_general(a, b, (((1,), (1,)), ((), ())), preferred_element_type=F32)


def _dot_tn(a, b):
    return lax.dot_general(a, b, (((0,), (0,)), ((), ())), preferred_element_type=F32)


def _silu(x):
    return x * jax.nn.sigmoid(x)


def _rms_rows(x, gain):
    return x * lax.rsqrt(jnp.mean(x * x, axis=-1, keepdims=True) + EPS) * gain


def _split3(x):
    x1 = x.astype(BF16)
    r1 = x - x1.astype(F32)
    x2 = r1.astype(BF16)
    x3 = (r1 - x2.astype(F32)).astype(BF16)
    return x1, x2, x3


def _group_meansq(x, bd):
    return _dot((x * x).astype(BF16), bd)


def _group_rms(x, gain, bd):
    wb = bd.shape[0]
    outs = []
    for j in range(x.shape[1] // wb):
        part = x[:, j * wb:(j + 1) * wb]
        ms = _group_meansq(part, bd)
        outs.append(part * lax.rsqrt(ms + EPS) * gain[:, j * wb:(j + 1) * wb])
    return outs[0] if len(outs) == 1 else jnp.concatenate(outs, axis=1)


def _block_diag_mean(width, group):
    r = np.arange(width)[:, None] // group
    c = np.arange(width)[None, :] // group
    return jnp.asarray((r == c).astype(np.float32) / group, dtype=BF16)


def _lam_value(lq1, lk1, lq2, lk2, lam_init):
    a = jnp.sum(lq1 * lk1, axis=-1, keepdims=True)
    b = jnp.sum(lq2 * lk2, axis=-1, keepdims=True)
    return jnp.exp(a) - jnp.exp(b) + lam_init


def _split3_const(x):
    x = np.float32(x)
    a = np.float32(ml_dtypes.bfloat16(x))
    b = np.float32(ml_dtypes.bfloat16(np.float32(x - a)))
    c = np.float32(ml_dtypes.bfloat16(np.float32(np.float32(x - a) - b)))
    return float(a), float(b), float(c)


def _lane_tile_sum(p):
    out = p[:, 0:LANES]
    for t in range(1, p.shape[1] // LANES):
        out = out + p[:, t * LANES:(t + 1) * LANES]
    return out


def _cparams(sem):
    return pltpu.CompilerParams(dimension_semantics=sem, vmem_limit_bytes=VMEM_LIMIT_BYTES)


def _full(shape):
    nd = len(shape)
    return pl.BlockSpec(shape, lambda *_: (0,) * nd)


def _inproj_a_kernel(x_ref, g_ref, w_ref, lbl_ref, qg_ref, kg_ref, bd_ref,
                     qa_ref, lf_ref, kk_ref, ia_ref, gsa_ref, qb_ref, kb_ref, vb_ref, gsb_ref, *, e, wa, wb, wkv):
    hb = _rms_rows(x_ref[...], g_ref[...]).astype(BF16)
    bd = bd_ref[...]
    o = [0]

    def seg(width):
        z = _dot(hb, w_ref[:, o[0]:o[0] + width])
        o[0] += width
        return z

    qa = seg(wa)
    qa_ref[...] = _silu(qa) * (DK_A ** -0.5)
    lbl = lbl_ref[...]
    ex = jnp.exp(lbl - jnp.max(lbl, axis=0, keepdims=True))
    sm = ex / jnp.sum(ex, axis=0, keepdims=True)
    lb = jnp.sum(sm[0:e + 1, :], axis=0, keepdims=True)
    f = lb + (1.0 - lb) * jax.nn.sigmoid(seg(wa))
    lf_ref[...] = jnp.log(f)
    kk_ref[...] = 1.0 - f
    ia_ref[...] = seg(wa)
    gsa_ref[...] = _silu(seg(wa)).astype(gsa_ref.dtype)
    qb_ref[...] = _group_rms(seg(wb), qg_ref[...], bd) * (D_B ** -0.5 * LOG2E)
    kb_ref[...] = _group_rms(seg(wkv), kg_ref[...], bd)
    vb_ref[...] = seg(wkv)
    gsb_ref[...] = _silu(seg(wb)).astype(gsb_ref.dtype)


def _inproj_a(x, g, w_bf16, lb_logits, q_gain_t, k_gain_t, bd, e, tm):
    T, D = x.shape
    wa, wb, wkv = lb_logits.shape[1], q_gain_t.shape[1], k_gain_t.shape[1]
    n_in = w_bf16.shape[1]
    assert n_in == 4 * wa + 2 * wb + 2 * wkv and T % tm == 0
    widths = (wa, wa, wa, wa, wa, wb, wkv, wkv, wb)
    dtypes = (F32, F32, F32, F32, BF16, F32, F32, F32, BF16)
    row = lambda w: pl.BlockSpec((tm, w), lambda i: (i, 0))
    return pl.pallas_call(
        functools.partial(_inproj_a_kernel, e=e, wa=wa, wb=wb, wkv=wkv),
        out_shape=[jax.ShapeDtypeStruct((T, w), d) for w, d in zip(widths, dtypes)],
        grid=(T // tm,),
        in_specs=[row(D), _full(g.shape), _full(w_bf16.shape), _full(lb_logits.shape),
                  _full(q_gain_t.shape), _full(k_gain_t.shape), _full(bd.shape)],
        out_specs=[row(w) for w in widths],
        compiler_params=_cparams(("parallel",)),
        name="inproj_a",
    )(x, g, w_bf16, lb_logits, q_gain_t, k_gain_t, bd)


def _gla_intra(q, k, b, bx, v_bf, C, c):
    n = C // c
    lane = lax.broadcasted_iota(jnp.int32, (c, C), 1)
    trow = lax.broadcasted_iota(jnp.int32, (c, LANES), 0)
    ones = jnp.ones((LANES, LANES), BF16)
    rows = []
    for i in range(n):
        sl = slice(i * c, (i + 1) * c)
        qi, ki, bi = q[sl], k[sl], b[sl]
        if i > 0:
            b0 = bx[i * c:i * c + 1, :]
            ks = k * jnp.exp(jnp.minimum(b0 - b, 0.0))
            qh = qi * jnp.exp(bi - b0)
            a_i = jnp.where(lane < i * c, _dot_nt(qh.astype(BF16), ks.astype(BF16)), 0.0)
        else:
            a_i = jnp.zeros((c, C), F32)
        pieces = []
        for s in range(c):
            d = jnp.minimum(bi - bi[s:s + 1, :], 0.0)
            pieces.append(jnp.where(trow >= s, qi * ki[s:s + 1, :] * jnp.exp(d), 0.0))
        r = _dot(jnp.concatenate(pieces, axis=0).astype(BF16), ones)
        for s in range(c):
            a_i = jnp.where(lane == i * c + s, r[s * c:(s + 1) * c, :C], a_i)
        rows.append(a_i)
    a = rows[0] if n == 1 else jnp.concatenate(rows, axis=0)
    return _dot(a.astype(BF16), v_bf)


def _gla_intra_bounded(q, k, b, bx, v_bf, C, c):
    lane = lax.broadcasted_iota(jnp.int32, (c, C), 1)
    trow = lax.broadcasted_iota(jnp.int32, (c, C), 0)
    rows = []
    for i in range(C // c):
        sl = slice(i * c, (i + 1) * c)
        b0 = bx[i * c:i * c + 1, :]
        ks = k * jnp.exp(jnp.minimum(b0 - b, GLA_FAST_DECAY))
        qh = q[sl] * jnp.exp(b[sl] - b0)
        rows.append(jnp.where(lane <= trow + i * c, _dot_nt(qh.astype(BF16), ks.astype(BF16)), 0.0))
    return _dot(jnp.concatenate(rows, axis=0).astype(BF16), v_bf)


def _gla_prompt_kernel(flag_ref, q_ref, k_ref, v_ref, lf_ref, o_ref, s_ref, st_scr, *, C, c, hps):
    j = pl.program_id(1)

    @pl.when(j == 0)
    def _():
        st_scr[...] = jnp.zeros_like(st_scr)

    def chunk(intra):
        rr = lax.broadcasted_iota(jnp.int32, (C, C), 0)
        cc = lax.broadcasted_iota(jnp.int32, (C, C), 1)
        tri = (cc <= rr).astype(BF16)
        for h in range(hps):
            sl = slice(h * DK_A, (h + 1) * DK_A)
            q, k, v, lf = q_ref[:, sl], k_ref[:, sl], v_ref[:, sl], lf_ref[:, sl]
            l1, l2, l3 = _split3(lf)
            b = _dot(tri, l1) + _dot(tri, l2) + _dot(tri, l3)
            st = st_scr[h]
            v_bf = v.astype(BF16)
            o = _dot_nt((q * jnp.exp(b)).astype(BF16), st.astype(BF16))
            o_ref[:, sl] = o + intra(q, k, b, b - lf, v_bf, C, c)
            b_end = b[C - 1:C, :]
            kt = k * jnp.exp(b_end - b)
            st_scr[h] = st * jnp.exp(b_end) + _dot_tn(v_bf, kt.astype(BF16))

    @pl.when(flag_ref[0] != 0)
    def _():
        chunk(_gla_intra_bounded)

    @pl.when(flag_ref[0] == 0)
    def _():
        chunk(_gla_intra)

    @pl.when(j == pl.num_programs(1) - 1)
    def _():
        for h in range(hps):
            s_ref[h] = st_scr[h].T


def _gla_prompt(q, k, v, lf, bounded_flag, C, hps):
    T, W = q.shape
    H = W // DK_A
    assert T % C == 0 and C % GLA_SUB == 0 and H % hps == 0
    blk = pl.BlockSpec((C, hps * DK_A), lambda h, j, fl: (j, h))
    grid_spec = pltpu.PrefetchScalarGridSpec(
        num_scalar_prefetch=1,
        grid=(H // hps, T // C),
        in_specs=[blk, blk, blk, blk],
        out_specs=[blk, pl.BlockSpec((hps, DK_A, DV_A), lambda h, j, fl: (h, 0, 0))],
        scratch_shapes=[pltpu.VMEM((hps, DV_A, DK_A), F32)],
    )
    return pl.pallas_call(
        functools.partial(_gla_prompt_kernel, C=C, c=GLA_SUB, hps=hps),
        out_shape=[jax.ShapeDtypeStruct((T, W), F32), jax.ShapeDtypeStruct((H, DK_A, DV_A), F32)],
        grid_spec=grid_spec,
        compiler_params=_cparams(("parallel", "arbitrary")),
        name="gla_prompt",
    )(bounded_flag, q, k, v, lf)


def _gla_sample_kernel(q_ref, k_ref, v_ref, lf_ref, s0_ref, o_ref, s_ref, *, T, H, n_sub):
    pad = LANES - T
    row = lax.broadcasted_iota(jnp.int32, (T, LANES), 0)
    zpad = jnp.zeros((pad, LANES), F32)
    for sq, h in [(sq, h) for sq in range(n_sub) for h in range(H)]:
        sl = slice(h * DK_A, (h + 1) * DK_A)
        rs = slice(sq * T, (sq + 1) * T)
        q, k, v, lf = q_ref[rs, sl], k_ref[rs, sl], v_ref[rs, sl], lf_ref[rs, sl]
        b = jnp.zeros((T, LANES), F32)
        for t in range(T):
            bt = jnp.sum(jnp.where(row <= t, lf, 0.0), axis=0, keepdims=True)
            b = jnp.where(row == t, bt, b)
        st = s0_ref[sq, h].T
        v_bf = jnp.concatenate([v, zpad], axis=0).astype(BF16)
        qd = jnp.concatenate([q * jnp.exp(b), zpad], axis=0).astype(BF16)
        o = _dot_nt(qd, st.astype(BF16))[:T]
        ones = jnp.ones((LANES, LANES), BF16)
        lane = lax.broadcasted_iota(jnp.int32, (T, LANES), 1)
        pieces = []
        for s in range(T):
            d = jnp.minimum(b - b[s:s + 1, :], 0.0)
            pieces.append(jnp.where(row >= s, q * k[s:s + 1, :] * jnp.exp(d), 0.0))
        r = _dot(jnp.concatenate(pieces, axis=0).astype(BF16), ones)
        a = jnp.zeros((T, LANES), F32)
        for s in range(T):
            a = jnp.where(lane == s, r[s * T:(s + 1) * T, :], a)
        a_bf = jnp.concatenate([a, zpad], axis=0).astype(BF16)
        o_ref[rs, sl] = o + _dot(a_bf, v_bf)[:T]
        b_end = b[T - 1:T, :]
        kt = jnp.concatenate([k * jnp.exp(b_end - b), zpad], axis=0).astype(BF16)
        st_new = st * jnp.exp(b_end) + _dot_tn(v_bf, kt)
        s_ref[sq, h] = st_new.T


def _gla_sample(q, k, v, lf, s0, T, n_sub):
    BT, W = q.shape
    B, H = s0.shape[0], s0.shape[1]
    assert BT == B * T and W == H * DK_A and T % 8 == 0 and T <= GLA_SUB and B % n_sub == 0
    blk = pl.BlockSpec((n_sub * T, W), lambda i: (i, 0))
    sblk = pl.BlockSpec((n_sub, H, DK_A, DV_A), lambda i: (i, 0, 0, 0))
    return pl.pallas_call(
        functools.partial(_gla_sample_kernel, T=T, H=H, n_sub=n_sub),
        out_shape=[jax.ShapeDtypeStruct((BT, W), F32), jax.ShapeDtypeStruct(s0.shape, F32)],
        grid=(B // n_sub,),
        in_specs=[blk, blk, blk, blk, sblk],
        out_specs=[blk, sblk],
        compiler_params=_cparams(("parallel",)),
        name="gla_sample",
    )(q, k, v, lf, s0)


def _alibi_slope(n_heads, h):
    return 2.0 ** (-8.0 * (h + 1) / n_heads)


def _online_update(mi, s, v_bf, m_scr, l_scr, acc_scr):
    reps = s.shape[1] // LANES
    m_prev = m_scr[mi]
    m_new = jnp.maximum(m_prev, jnp.max(s, axis=1, keepdims=True))
    alpha = jnp.exp2(m_prev - m_new)
    p = jnp.exp2(s - jnp.tile(m_new, (1, reps)))
    l_scr[mi] = alpha * l_scr[mi] + jnp.sum(p, axis=1, keepdims=True)
    acc_scr[mi] = alpha * acc_scr[mi] + _dot(p.astype(BF16), v_bf)
    m_scr[mi] = m_new


def _diff_prompt_kernel(qi_ref, kj_ref, flag_ref, q_ref, k_ref, v_ref, kg_ref, base_ref,
                        lq1_ref, lk1_ref, lq2_ref, lk2_ref, o_ref, qxt_scr, acct_scr, m_scr, l_scr, acc_scr,
                        *, B, n_heads, lam_init):
    n = pl.program_id(0)
    p = pl.program_id(1)
    i = qi_ref[p]
    j = kj_ref[p]
    fast = flag_ref[0] != 0
    lane = lax.broadcasted_iota(jnp.int32, (B, LANES), 1)

    def head_slope(g):
        return jnp.where(n == 0, _alibi_slope(n_heads, g), _alibi_slope(n_heads, G_B + g)) * LOG2E

    def q_map(g, c):
        qh = q_ref[:, g * LANES:(g + 1) * LANES]
        return jnp.where((lane < D_B) if c == 0 else (lane >= D_B), qh, 0.0)

    def causal_keep():
        return lax.broadcasted_iota(jnp.int32, (B, B), 1) <= lax.broadcasted_iota(jnp.int32, (B, B), 0)

    def finish(l_of):
        lam = _lam_value(lq1_ref[...], lk1_ref[...], lq2_ref[...], lk2_ref[...], lam_init)
        for g in range(G_B):
            o_ref[:, g * LANES:(g + 1) * LANES] = (acc_scr[2 * g] / l_of(2 * g)
                                                   - lam * (acc_scr[2 * g + 1] / l_of(2 * g + 1)))

    @pl.when(jnp.logical_and(fast, j == 0))
    def _():
        kmax = math.sqrt(D_B) * jnp.max(jnp.abs(kg_ref[...]), axis=-1, keepdims=True)
        rloc = lax.broadcasted_iota(jnp.int32, (B, 1), 0).astype(F32)
        lane1 = lax.broadcasted_iota(jnp.int32, (1, LANES), 1)
        for g in range(G_B):
            slope = head_slope(g)
            c_n0 = _split3_const(_alibi_slope(n_heads, g) * LOG2E)
            c_n1 = _split3_const(_alibi_slope(n_heads, G_B + g) * LOG2E)
            crow = jnp.zeros((1, LANES), F32)
            for t in range(3):
                st = jnp.where(n == 0, c_n0[t], c_n1[t])
                crow = jnp.where(lane1 == t, st * LANES, crow)
                crow = jnp.where(lane1 == 3 + t, st, crow)
            for c in range(2):
                qm = q_map(g, c)
                qn = jnp.sqrt(jnp.sum(qm * qm, axis=1, keepdims=True))
                m1, m2, m3 = _split3(slope * rloc + qn * kmax)
                feat = jnp.where(lane == 6, -m1.astype(F32),
                                 jnp.where(lane == 7, -m2.astype(F32), jnp.where(lane == 8, -m3.astype(F32), crow)))
                qxt_scr[2 * g + c] = jnp.concatenate([qm.T, feat.T], axis=0).astype(BF16)
        acct_scr[...] = jnp.zeros_like(acct_scr)

    def tile_fast(masked):
        a = ((i - j) * (B // LANES)).astype(F32)
        ind = (lax.broadcasted_iota(jnp.int32, (1, LANES), 1) < 3).astype(F32)
        featk = (base_ref[...] - a * ind).astype(BF16)
        kx = jnp.concatenate([k_ref[...].astype(BF16), featk], axis=1)
        v1t = jnp.concatenate([v_ref[...].T.astype(BF16), jnp.ones((ONES_ROWS, B), BF16)], axis=0)
        if masked:
            keep = lax.broadcasted_iota(jnp.int32, (B, B), 0) <= lax.broadcasted_iota(jnp.int32, (B, B), 1)
        for mi in range(2 * G_B):
            st = _dot(kx, qxt_scr[mi])
            if masked:
                st = jnp.where(keep, st, NEG_BIG)
            acct_scr[mi] = acct_scr[mi] + _dot(v1t, jnp.exp2(st).astype(BF16))

    @pl.when(jnp.logical_and(fast, j < i))
    def _():
        tile_fast(False)

    @pl.when(jnp.logical_and(fast, j == i))
    def _():
        tile_fast(True)
        lam = _lam_value(lq1_ref[...], lk1_ref[...], lq2_ref[...], lk2_ref[...], lam_init)
        for g in range(G_B):
            a0, a1 = acct_scr[2 * g], acct_scr[2 * g + 1]
            ot = a0[:LANES] / a0[LANES:LANES + 1] - lam * (a1[:LANES] / a1[LANES:LANES + 1])
            o_ref[:, g * LANES:(g + 1) * LANES] = ot.T

    slow = jnp.logical_not(fast)

    @pl.when(jnp.logical_and(slow, j == 0))
    def _():
        m_scr[...] = jnp.full_like(m_scr, NEG_BIG)
        l_scr[...] = jnp.zeros_like(l_scr)
        acc_scr[...] = jnp.zeros_like(acc_scr)

    def tile_slow(masked):
        k_bf = k_ref[...].astype(BF16)
        v_bf = v_ref[...].astype(BF16)
        col = lax.broadcasted_iota(jnp.int32, (1, B), 1).astype(F32)
        rel = col - ((i - j) * B).astype(F32)
        if masked:
            keep = causal_keep()
        for g in range(G_B):
            bias = head_slope(g) * rel
            for c in range(2):
                s = _dot_nt(q_map(g, c).astype(BF16), k_bf) + bias
                if masked:
                    s = jnp.where(keep, s, NEG_BIG)
                _online_update(g * 2 + c, s, v_bf, m_scr, l_scr, acc_scr)

    @pl.when(jnp.logical_and(slow, j < i))
    def _():
        tile_slow(False)

    @pl.when(jnp.logical_and(slow, j == i))
    def _():
        tile_slow(True)
        finish(lambda mi: l_scr[mi])


def _position_features(B):
    c = np.arange(B)
    f = np.zeros((B, LANES), np.float32)
    f[:, 0:3] = (c >> 7)[:, None]
    f[:, 3:6] = (c & 127)[:, None]
    f[:, 6:9] = 1.0
    return jnp.asarray(f)


def _diff_prompt(qb, kb, vb, k_gain, fast_flag, lq1, lk1, lq2, lk2, lam_init, B):
    T = qb.shape[0]
    n_heads = qb.shape[1] // LANES
    nb = T // B
    assert T % B == 0 and n_heads == KVH_B * G_B and B % LANES == 0 and (nb * B) // LANES <= 256
    qi = np.concatenate([np.full(i + 1, i, np.int32) for i in range(nb)])
    kj = np.concatenate([np.arange(i + 1, dtype=np.int32) for i in range(nb)])
    base = _position_features(B)
    const = lambda shape: pl.BlockSpec(shape, lambda n, p, qi, kj, fl: (0,) * len(shape))
    grid_spec = pltpu.PrefetchScalarGridSpec(
        num_scalar_prefetch=3,
        grid=(KVH_B, len(qi)),
        in_specs=[pl.BlockSpec((B, G_B * LANES), lambda n, p, qi, kj, fl: (qi[p], n)),
                  pl.BlockSpec((B, LANES), lambda n, p, qi, kj, fl: (kj[p], n)),
                  pl.BlockSpec((B, LANES), lambda n, p, qi, kj, fl: (kj[p], n)),
                  const(k_gain.shape), const(base.shape),
                  const(lq1.shape), const(lk1.shape), const(lq2.shape), const(lk2.shape)],
        out_specs=pl.BlockSpec((B, G_B * LANES), lambda n, p, qi, kj, fl: (qi[p], n)),
        scratch_shapes=[pltpu.VMEM((2 * G_B, 2 * LANES, B), BF16),
                        pltpu.VMEM((2 * G_B, LANES + ONES_ROWS, B), F32)] + [pltpu.VMEM((2 * G_B, B, LANES), F32)] * 3,
    )
    return pl.pallas_call(
        functools.partial(_diff_prompt_kernel, B=B, n_heads=n_heads, lam_init=lam_init),
        out_shape=jax.ShapeDtypeStruct(qb.shape, F32),
        grid_spec=grid_spec,
        compiler_params=_cparams(("parallel", "arbitrary")),
        name="diff_prompt",
    )(jnp.asarray(qi), jnp.asarray(kj), fast_flag, qb, kb, vb, k_gain, base, lq1, lk1, lq2, lk2)


def _diff_sample_kernel(pt_ref, flag_ref, q_ref, kn_ref, vn_ref, kg_ref, lq1_ref, lk1_ref, lq2_ref, lk2_ref,
                        ck_ref, cv_ref, o_ref, kbuf, vbuf, ksem, vsem, m_scr, l_scr, acc_scr,
                        *, P, page, n_chunks, n_seq, T, n_heads, lam_init):
    b = pl.program_id(0)
    c = pl.program_id(1)
    step = b * n_chunks + c
    slot = lax.rem(step, 2)
    past = n_chunks * P * page
    R = G_B * 2 * T
    fast = flag_ref[0] != 0

    def k_copy(pg, sl, p):
        return pltpu.make_async_copy(ck_ref.at[pg], kbuf.at[sl, p], ksem.at[sl])

    def v_copy(pg, sl, p):
        return pltpu.make_async_copy(cv_ref.at[pg], vbuf.at[sl, p], vsem.at[sl])

    def issue(bb, cc, sl):
        for p in range(P):
            pg = pt_ref[bb, cc * P + p]
            k_copy(pg, sl, p).start()
            v_copy(pg, sl, p).start()

    @pl.when(step == 0)
    def _():
        issue(0, 0, 0)

    nxt = step + 1

    @pl.when(nxt < n_seq * n_chunks)
    def _():
        issue(nxt // n_chunks, lax.rem(nxt, n_chunks), 1 - slot)

    for p in range(P):
        k_copy(0, slot, p).wait()
        v_copy(0, slot, p).wait()

    lane = lax.broadcasted_iota(jnp.int32, (T, LANES), 1)
    rrow = lax.broadcasted_iota(jnp.int32, (R, 1), 0)
    trow1 = lax.rem(rrow, T).astype(F32)
    col = lax.broadcasted_iota(jnp.int32, (1, P * page), 1)
    rel = (col + (c * (P * page) - past)).astype(F32)

    def lhs_and_slopes(n):
        blocks = []
        for g in range(G_B):
            qh = q_ref[:, (n * G_B + g) * LANES:(n * G_B + g + 1) * LANES]
            for cs in range(2):
                half = (lane < D_B) if cs == 0 else (lane >= D_B)
                blocks.append(jnp.where(half, qh, 0.0))
        lhs = jnp.concatenate(blocks, axis=0)
        slopes = jnp.where(rrow < 2 * T, _alibi_slope(n_heads, n * G_B), _alibi_slope(n_heads, n * G_B + 1)) * LOG2E
        return lhs, slopes

    def cached_kv(n):
        kt = jnp.concatenate([kbuf[slot, p, n * LANES:(n + 1) * LANES, :] for p in range(P)], axis=1).astype(BF16)
        vv = jnp.concatenate([vbuf[slot, p, pl.ds(n, page, stride=KVH_B), :] for p in range(P)], axis=0).astype(BF16)
        return kt, vv

    def new_kv(n):
        zpad = jnp.zeros((LANES - T, LANES), F32)
        kn = jnp.concatenate([kn_ref[:, n * LANES:(n + 1) * LANES], zpad], axis=0).astype(BF16)
        vn = jnp.concatenate([vn_ref[:, n * LANES:(n + 1) * LANES], zpad], axis=0).astype(BF16)
        ncol = lax.broadcasted_iota(jnp.int32, (R, LANES), 1)
        keep = ncol <= lax.rem(lax.broadcasted_iota(jnp.int32, (R, LANES), 0), T)
        return kn, vn, ncol.astype(F32), keep

    def write_out(n, res, lam):
        for g in range(G_B):
            r0 = g * 2 * T
            o_ref[:, (n * G_B + g) * LANES:(n * G_B + g + 1) * LANES] = res[r0:r0 + T] - lam * res[r0 + T:r0 + 2 * T]

    def run(use_bound):
        @pl.when(c == 0)
        def _():
            if not use_bound:
                m_scr[...] = jnp.full_like(m_scr, NEG_BIG)
            l_scr[...] = jnp.zeros_like(l_scr)
            acc_scr[...] = jnp.zeros_like(acc_scr)

        kmax = math.sqrt(D_B) * jnp.max(jnp.abs(kg_ref[...]), axis=-1, keepdims=True)

        def row_shift(lhs, slopes):
            return slopes * trow1 + jnp.sqrt(jnp.sum(lhs * lhs, axis=1, keepdims=True)) * kmax

        def accumulate(n, s, vv):
            if use_bound:
                pr = jnp.exp2(s)
                l_scr[n] = l_scr[n] + _lane_tile_sum(pr)
                acc_scr[n] = acc_scr[n] + _dot(pr.astype(BF16), vv)
            else:
                _online_update(n, s, vv, m_scr, l_scr, acc_scr)

        for n in range(KVH_B):
            lhs, slopes = lhs_and_slopes(n)
            kt, vv = cached_kv(n)
            s = _dot(lhs.astype(BF16), kt) + slopes * rel
            if use_bound:
                s = s - row_shift(lhs, slopes)
            accumulate(n, s, vv)

        @pl.when(c == n_chunks - 1)
        def _():
            lam = _lam_value(lq1_ref[...], lk1_ref[...], lq2_ref[...], lk2_ref[...], lam_init)
            for n in range(KVH_B):
                lhs, slopes = lhs_and_slopes(n)
                kn, vn, ncol, keep = new_kv(n)
                s = _dot_nt(lhs.astype(BF16), kn) + slopes * ncol
                if use_bound:
                    s = s - row_shift(lhs, slopes)
                accumulate(n, jnp.where(keep, s, NEG_BIG), vn)
                den = jnp.sum(l_scr[n], axis=1, keepdims=True) if use_bound else l_scr[n]
                write_out(n, acc_scr[n] / den, lam)

    @pl.when(fast)
    def _():
        run(True)

    @pl.when(jnp.logical_not(fast))
    def _():
        run(False)


def _diff_sample(qb, kb_new, vb_new, page_table, kt_pages, v_pages, k_gain, fast_flag, lq1, lk1, lq2, lk2, lam_init,
                 T, P):
    n_seq, n_pages = page_table.shape
    page = kt_pages.shape[2]
    n_heads = qb.shape[1] // LANES
    assert n_pages % P == 0 and qb.shape[0] == n_seq * T and page == LANES
    n_chunks = n_pages // P
    R = G_B * 2 * T
    small = pl.BlockSpec(lq1.shape, lambda b, c, pt, fl: (0, 0))
    rows = lambda w: pl.BlockSpec((T, w), lambda b, c, pt, fl: (b, 0))
    grid_spec = pltpu.PrefetchScalarGridSpec(
        num_scalar_prefetch=2,
        grid=(n_seq, n_chunks),
        in_specs=[rows(qb.shape[1]), rows(kb_new.shape[1]), rows(vb_new.shape[1]), small, small, small, small, small,
                  pl.BlockSpec(memory_space=pl.ANY), pl.BlockSpec(memory_space=pl.ANY)],
        out_specs=rows(qb.shape[1]),
        scratch_shapes=[pltpu.VMEM((2, P) + kt_pages.shape[1:], F32),
                        pltpu.VMEM((2, P) + v_pages.shape[1:], F32),
                        pltpu.SemaphoreType.DMA((2,)), pltpu.SemaphoreType.DMA((2,)),
                        pltpu.VMEM((KVH_B, R, LANES), F32), pltpu.VMEM((KVH_B, R, LANES), F32),
                        pltpu.VMEM((KVH_B, R, LANES), F32)],
    )
    return pl.pallas_call(
        functools.partial(_diff_sample_kernel, P=P, page=page, n_chunks=n_chunks, n_seq=n_seq, T=T,
                          n_heads=n_heads, lam_init=lam_init),
        out_shape=jax.ShapeDtypeStruct(qb.shape, F32),
        grid_spec=grid_spec,
        compiler_params=_cparams(("arbitrary", "arbitrary")),
        name="diff_sample",
    )(page_table, fast_flag, qb, kb_new, vb_new, k_gain, lq1, lk1, lq2, lk2, kt_pages, v_pages)


def _diff_fused_kernel(qi_ref, kj_ref, pt_ref, q_ref, k_ref, v_ref, kg_ref, base_ref, lq1_ref, lk1_ref, lq2_ref,
                       lk2_ref, qs_ref, kn_ref, vn_ref, ck_ref, cv_ref, o_ref, os_ref,
                       qxt_scr, acct_scr, kbuf, vbuf, ksem, vsem, ls_scr, accs_scr,
                       *, B, P, CH, page, n_chunks, n_seq, T, n_heads, lam_init):
    n = pl.program_id(0)
    p = pl.program_id(1)
    i = qi_ref[p]
    j = kj_ref[p]
    lam = _lam_value(lq1_ref[...], lk1_ref[...], lq2_ref[...], lk2_ref[...], lam_init)
    kmax = math.sqrt(D_B) * jnp.max(jnp.abs(kg_ref[...]), axis=-1, keepdims=True)

    total = n_seq * n_chunks
    step = n * pl.num_programs(1) + p
    group = lax.rem(step, 2)
    past = n_chunks * P * page
    R = G_B * 2 * T

    def k_copy(pg, sl, pp):
        return pltpu.make_async_copy(ck_ref.at[pg], kbuf.at[sl, pp], ksem.at[sl])

    def v_copy(pg, sl, pp):
        return pltpu.make_async_copy(cv_ref.at[pg], vbuf.at[sl, pp], vsem.at[sl])

    def issue(chunk, sl):
        bb = chunk // n_chunks
        cc = lax.rem(chunk, n_chunks)
        for pp in range(P):
            pg = pt_ref[bb, cc * P + pp]
            k_copy(pg, sl, pp).start()
            v_copy(pg, sl, pp).start()

    def issue_step(s, grp):
        for u in range(CH):
            issue(jnp.minimum(s * CH + u, total - 1), grp * CH + u)

    def wait_group(grp):
        for u in range(CH):
            for pp in range(P):
                k_copy(0, grp * CH + u, pp).wait()
                v_copy(0, grp * CH + u, pp).wait()

    @pl.when(step == 0)
    def _():
        issue_step(0, 0)
        issue_step(1, 1)

    wait_group(group)

    def prefetch_ahead():
        issue_step(step + 2, group)

    def sample_chunk(u, write):
        cid = step * CH + u
        slot = group * CH + u
        valid = cid < total
        c = lax.rem(cid, n_chunks)
        first = jnp.logical_and(valid, c == 0)
        gate = valid.astype(F32)
        lane = lax.broadcasted_iota(jnp.int32, (T, LANES), 1)
        rrow = lax.broadcasted_iota(jnp.int32, (R, 1), 0)
        trow1 = lax.rem(rrow, T).astype(F32)
        col = lax.broadcasted_iota(jnp.int32, (1, P * page), 1)
        rel = (col + (c * (P * page) - past)).astype(F32)
        zpad = jnp.zeros((LANES - T, LANES), F32)
        ncol = lax.broadcasted_iota(jnp.int32, (R, LANES), 1)
        keep_new = ncol <= lax.rem(lax.broadcasted_iota(jnp.int32, (R, LANES), 0), T)
        for nh in range(KVH_B):
            blocks = []
            for g in range(G_B):
                qh = qs_ref[:, (nh * G_B + g) * LANES:(nh * G_B + g + 1) * LANES]
                for cs in range(2):
                    blocks.append(jnp.where((lane < D_B) if cs == 0 else (lane >= D_B), qh, 0.0))
            lhs = jnp.concatenate(blocks, axis=0)
            slopes = jnp.where(rrow < 2 * T, _alibi_slope(n_heads, nh * G_B),
                               _alibi_slope(n_heads, nh * G_B + 1)) * LOG2E
            shift = slopes * trow1 + jnp.sqrt(jnp.sum(lhs * lhs, axis=1, keepdims=True)) * kmax
            lhs_bf = lhs.astype(BF16)
            kt = jnp.concatenate([kbuf[slot, pp, nh * LANES:(nh + 1) * LANES, :] for pp in range(P)],
                                 axis=1).astype(BF16)
            vv = jnp.concatenate([vbuf[slot, pp, pl.ds(nh, page, stride=KVH_B), :] for pp in range(P)],
                                 axis=0).astype(BF16)
            pr = jnp.exp2(_dot(lhs_bf, kt) + (slopes * rel - shift))
            l_new = jnp.where(first, 0.0, ls_scr[nh]) + gate * _lane_tile_sum(pr)
            acc_new = jnp.where(first, 0.0, accs_scr[nh]) + gate * _dot(pr.astype(BF16), vv)
            ls_scr[nh] = l_new
            accs_scr[nh] = acc_new
            if not write:
                continue
            kn = jnp.concatenate([kn_ref[:, nh * LANES:(nh + 1) * LANES], zpad], axis=0).astype(BF16)
            vn = jnp.concatenate([vn_ref[:, nh * LANES:(nh + 1) * LANES], zpad], axis=0).astype(BF16)
            s_new = _dot_nt(lhs_bf, kn) + (slopes * ncol.astype(F32) - shift)
            pn = jnp.exp2(jnp.where(keep_new, s_new, NEG_BIG))
            den = jnp.sum(l_new + pn, axis=1, keepdims=True)
            res = (acc_new + _dot(pn.astype(BF16), vn)) / den
            for g in range(G_B):
                r0 = g * 2 * T
                os_ref[:, (nh * G_B + g) * LANES:(nh * G_B + g + 1) * LANES] = (
                    res[r0:r0 + T] - lam * res[r0 + T:r0 + 2 * T])

    @pl.when(j == 0)
    def _():
        frow = lax.broadcasted_iota(jnp.int32, (LANES, B), 0)
        frow1 = lax.broadcasted_iota(jnp.int32, (LANES, 1), 0)
        qloc = lax.broadcasted_iota(jnp.int32, (1, B), 1).astype(F32)
        for g in range(G_B):
            slope = jnp.where(n == 0, _alibi_slope(n_heads, g), _alibi_slope(n_heads, G_B + g)) * LOG2E
            c_n0 = _split3_const(_alibi_slope(n_heads, g) * LOG2E)
            c_n1 = _split3_const(_alibi_slope(n_heads, G_B + g) * LOG2E)
            ccol = jnp.zeros((LANES, 1), F32)
            for t in range(3):
                st = jnp.where(n == 0, c_n0[t], c_n1[t])
                ccol = jnp.where(frow1 == t, st * LANES, ccol)
                ccol = jnp.where(frow1 == 3 + t, st, ccol)
            qht = q_ref[:, g * LANES:(g + 1) * LANES].T
            for c in range(2):
                qmt = jnp.where((frow < D_B) if c == 0 else (frow >= D_B), qht, 0.0)
                qn = jnp.sqrt(jnp.sum(qmt * qmt, axis=0, keepdims=True))
                m1, m2, m3 = _split3(slope * qloc + qn * kmax)
                featt = jnp.where(frow == 6, -m1.astype(F32),
                                  jnp.where(frow == 7, -m2.astype(F32),
                                            jnp.where(frow == 8, -m3.astype(F32), ccol)))
                qxt_scr[2 * g + c] = jnp.concatenate([qmt, featt], axis=0).astype(BF16)
        acct_scr[...] = jnp.zeros_like(acct_scr)

    def prompt_tile(masked):
        a = ((i - j) * (B // LANES)).astype(F32)
        ind = (lax.broadcasted_iota(jnp.int32, (1, LANES), 1) < 3).astype(F32)
        featk = (base_ref[...] - a * ind).astype(BF16)
        kx = jnp.concatenate([k_ref[...].astype(BF16), featk], axis=1)
        v1t = jnp.concatenate([v_ref[...].T.astype(BF16), jnp.ones((ONES_ROWS, B), BF16)], axis=0)
        if masked:
            keep = lax.broadcasted_iota(jnp.int32, (B, B), 0) <= lax.broadcasted_iota(jnp.int32, (B, B), 1)
        for mi in range(2 * G_B):
            st = _dot(kx, qxt_scr[mi])
            if masked:
                st = jnp.where(keep, st, NEG_BIG)
            acct_scr[mi] = acct_scr[mi] + _dot(v1t, jnp.exp2(st).astype(BF16))

    @pl.when(j < i)
    def _():
        prompt_tile(False)
        for u in range(CH):
            sample_chunk(u, u == CH - 1)
        prefetch_ahead()

    @pl.when(j == i)
    def _():
        prompt_tile(True)
        for u in range(CH):
            sample_chunk(u, u == CH - 1)
        for g in range(G_B):
            a0, a1 = acct_scr[2 * g], acct_scr[2 * g + 1]
            ot = a0[:LANES] / a0[LANES:LANES + 1] - lam * (a1[:LANES] / a1[LANES:LANES + 1])
            o_ref[:, g * LANES:(g + 1) * LANES] = ot.T
        prefetch_ahead()

    @pl.when(step == pl.num_programs(0) * pl.num_programs(1) - 1)
    def _():
        wait_group(0)
        wait_group(1)


def _diff_fused(qb, kb, vb, qs, kb_new, vb_new, page_table, kt_pages, v_pages, k_gain, lq1, lk1, lq2, lk2,
                lam_init, B, T, P, CH):
    Tp = qb.shape[0]
    n_heads = qb.shape[1] // LANES
    nb = Tp // B
    n_seq, n_pages = page_table.shape
    page = kt_pages.shape[2]
    n_chunks = n_pages // P
    qi = np.concatenate([np.full(i + 1, i, np.int32) for i in range(nb)])
    kj = np.concatenate([np.arange(i + 1, dtype=np.int32) for i in range(nb)])
    n_pairs = len(qi)
    assert Tp % B == 0 and n_pages % P == 0 and page == LANES and qs.shape[0] == n_seq * T
    assert n_chunks % CH == 0 and KVH_B * n_pairs * CH >= n_seq * n_chunks, "every page chunk needs a grid step"
    base = _position_features(B)
    R = G_B * 2 * T
    const = lambda a: pl.BlockSpec(a.shape, lambda n, p, qi, kj, pt: (0,) * a.ndim)
    seq_of = lambda n, p: jnp.minimum(((n * n_pairs + p) * CH) // n_chunks, n_seq - 1)
    srow = lambda w: pl.BlockSpec((T, w), lambda n, p, qi, kj, pt: (seq_of(n, p), 0))
    grid_spec = pltpu.PrefetchScalarGridSpec(
        num_scalar_prefetch=3,
        grid=(KVH_B, n_pairs),
        in_specs=[pl.BlockSpec((B, G_B * LANES), lambda n, p, qi, kj, pt: (qi[p], n)),
                  pl.BlockSpec((B, LANES), lambda n, p, qi, kj, pt: (kj[p], n)),
                  pl.BlockSpec((B, LANES), lambda n, p, qi, kj, pt: (kj[p], n)),
                  const(k_gain), const(base), const(lq1), const(lk1), const(lq2), const(lk2),
                  srow(qs.shape[1]), srow(kb_new.shape[1]), srow(vb_new.shape[1]),
                  pl.BlockSpec(memory_space=pl.ANY), pl.BlockSpec(memory_space=pl.ANY)],
        out_specs=[pl.BlockSpec((B, G_B * LANES), lambda n, p, qi, kj, pt: (qi[p], n)), srow(qs.shape[1])],
        scratch_shapes=[pltpu.VMEM((2 * G_B, 2 * LANES, B), BF16),
                        pltpu.VMEM((2 * G_B, LANES + ONES_ROWS, B), F32),
                        pltpu.VMEM((2 * CH, P) + kt_pages.shape[1:], F32),
                        pltpu.VMEM((2 * CH, P) + v_pages.shape[1:], F32),
                        pltpu.SemaphoreType.DMA((2 * CH,)), pltpu.SemaphoreType.DMA((2 * CH,)),
                        pltpu.VMEM((KVH_B, R, LANES), F32), pltpu.VMEM((KVH_B, R, LANES), F32)],
    )
    return pl.pallas_call(
        functools.partial(_diff_fused_kernel, B=B, P=P, CH=CH, page=page, n_chunks=n_chunks, n_seq=n_seq, T=T,
                          n_heads=n_heads, lam_init=lam_init),
        out_shape=[jax.ShapeDtypeStruct(qb.shape, F32), jax.ShapeDtypeStruct(qs.shape, F32)],
        grid_spec=grid_spec,
        compiler_params=_cparams(("arbitrary", "arbitrary")),
        name="diff_fused",
    )(jnp.asarray(qi), jnp.asarray(kj), page_table, qb, kb, vb, k_gain, base, lq1, lk1, lq2, lk2,
      qs, kb_new, vb_new, kt_pages, v_pages)


def _outproj_a_kernel(x_ref, oa_ref, gsa_ref, ob_ref, gsb_ref, og_ref, sg_ref, w_ref, y_ref, *, lam_init):
    def branch(o_ref, gate_ref, gain, scale):
        parts = []
        for h in range(o_ref.shape[1] // LANES):
            sl = slice(h * LANES, (h + 1) * LANES)
            parts.append(_rms_rows(o_ref[:, sl], gain) * scale * gate_ref[:, sl])
        return jnp.concatenate(parts, axis=1).astype(BF16)

    ya = branch(oa_ref, gsa_ref, og_ref[...], 1.0)
    yb = branch(ob_ref, gsb_ref, sg_ref[...], 1.0 - lam_init)
    wa = oa_ref.shape[1]
    y_ref[...] = x_ref[...] + _dot(ya, w_ref[0:wa, :]) + _dot(yb, w_ref[wa:, :])


def _outproj_a(x, oa, gsa, ob, gsb, o_gain, subln, w_bf16, lam_init, tm):
    T, D = x.shape
    row = lambda w: pl.BlockSpec((tm, w), lambda i: (i, 0))
    return pl.pallas_call(
        functools.partial(_outproj_a_kernel, lam_init=lam_init),
        out_shape=jax.ShapeDtypeStruct((T, D), F32),
        grid=(T // tm,),
        in_specs=[row(D), row(oa.shape[1]), row(oa.shape[1]), row(ob.shape[1]), row(ob.shape[1]),
                  _full(o_gain.shape), _full(subln.shape), _full(w_bf16.shape)],
        out_specs=row(D),
        compiler_params=_cparams(("parallel",)),
        name="outproj_a",
    )(x, oa, gsa, ob, gsb, o_gain, subln, w_bf16)


def _inproj_c_kernel(x_ref, g_ref, w_ref, qg_ref, kg_ref, bd_ref, q_ref, k_ref, v_ref, gs_ref, *, wq, wkv):
    hb = _rms_rows(x_ref[...], g_ref[...]).astype(BF16)
    bd = bd_ref[...]
    q_ref[...] = _group_rms(_dot(hb, w_ref[:, 0:wq]), qg_ref[...], bd) * (HD_C ** -0.5 * LOG2E)
    k_ref[...] = _group_rms(_dot(hb, w_ref[:, wq:wq + wkv]), kg_ref[...], bd)
    v_ref[...] = _dot(hb, w_ref[:, wq + wkv:wq + 2 * wkv])
    gs_ref[...] = _silu(_dot(hb, w_ref[:, wq + 2 * wkv:])).astype(gs_ref.dtype)


def _inproj_c(x, g, w_bf16, q_gain_t, k_gain_t, bd, tm):
    T, D = x.shape
    wq, wkv = q_gain_t.shape[1], k_gain_t.shape[1]
    widths = (wq, wkv, wkv, wq)
    dtypes = (F32, F32, F32, BF16)
    row = lambda w: pl.BlockSpec((tm, w), lambda i: (i, 0))
    return pl.pallas_call(
        functools.partial(_inproj_c_kernel, wq=wq, wkv=wkv),
        out_shape=[jax.ShapeDtypeStruct((T, w), d) for w, d in zip(widths, dtypes)],
        grid=(T // tm,),
        in_specs=[row(D), _full(g.shape), _full(w_bf16.shape), _full(q_gain_t.shape), _full(k_gain_t.shape),
                  _full(bd.shape)],
        out_specs=[row(w) for w in widths],
        compiler_params=_cparams(("parallel",)),
        name="inproj_c",
    )(x, g, w_bf16, q_gain_t, k_gain_t, bd)


def _swa_softmax_pv(s, sink, v_list):
    m = sink
    for piece in s:
        m = jnp.maximum(m, jnp.max(piece, axis=1, keepdims=True))
    den = jnp.exp2(sink - m)
    out = None
    for piece, pv in zip(s, v_list):
        p = jnp.exp2(piece - m)
        den = den + jnp.sum(p, axis=1, keepdims=True)
        term = pv(p.astype(BF16))
        out = term if out is None else out + term
    return out / den


def _swa_prompt_kernel(flag_ref, q_ref, kc_ref, kp_ref, vc_ref, vp_ref, snk_ref, qg_ref, kg_ref, o_ref, *, n_heads):
    i = pl.program_id(0)
    W = q_ref.shape[0]
    n_pairs = n_heads // KVH_C
    half_rows = n_pairs * W
    lane = lax.broadcasted_iota(jnp.int32, (W, LANES), 1)
    blocks = []
    for n in range(KVH_C):
        half = (lane < HD_C) if n == 0 else (lane >= HD_C)
        for m in range(n_pairs):
            blocks.append(jnp.where(half, q_ref[:, m * LANES:(m + 1) * LANES], 0.0))
    lhs = jnp.concatenate(blocks, axis=0).astype(BF16)
    keys = jnp.concatenate([kp_ref[...], kc_ref[...]], axis=0).astype(BF16)
    vals = jnp.concatenate([vp_ref[...], vc_ref[...]], axis=0).astype(BF16)
    r = lax.broadcasted_iota(jnp.int32, (W, 2 * W), 0)
    c = lax.broadcasted_iota(jnp.int32, (W, 2 * W), 1)
    dist = r + W - c
    valid = jnp.logical_and(jnp.logical_and(dist >= 0, dist <= W), jnp.logical_or(c >= W, i > 0))
    neg_dist = jnp.where(valid, -dist.astype(F32), NEG_BIG)

    def write(res):
        for m in range(n_pairs):
            lo = res[m * W:(m + 1) * W]
            hi = res[half_rows + m * W:half_rows + (m + 1) * W]
            o_ref[:, m * LANES:(m + 1) * LANES] = jnp.where(lane < HD_C, lo, hi)

    @pl.when(flag_ref[0] != 0)
    def _():
        bound = ((LOG2E * math.sqrt(HD_C)) * jnp.max(jnp.abs(qg_ref[...]), axis=-1, keepdims=True)
                 * jnp.max(jnp.abs(kg_ref[...]), axis=-1, keepdims=True))
        biases, sink_terms = [], []
        for h in range(n_heads):
            sink_h = snk_ref[:, h:h + 1] * LOG2E
            shift = jnp.maximum(bound, sink_h)
            biases.append((_alibi_slope(n_heads, h) * LOG2E) * neg_dist - shift)
            sink_terms.append(jnp.broadcast_to(jnp.exp2(sink_h - shift), (W, LANES)))
        pr = jnp.exp2(_dot_nt(lhs, keys) + jnp.concatenate(biases, axis=0))
        vals1 = jnp.concatenate([vals, jnp.ones_like(vals)], axis=1)
        res = _dot(pr.astype(BF16), vals1)
        den = res[:, LANES:] + jnp.concatenate(sink_terms, axis=0)
        write(res[:, :LANES] / den)

    @pl.when(flag_ref[0] == 0)
    def _():
        slopes, sinks = [], []
        for h in range(n_heads):
            slopes.append(jnp.full((W, 1), _alibi_slope(n_heads, h) * LOG2E, F32))
            sinks.append(jnp.broadcast_to(snk_ref[:, h:h + 1] * LOG2E, (W, 1)))
        slope = jnp.concatenate(slopes, axis=0)
        sink = jnp.concatenate(sinks, axis=0)
        s = _dot_nt(lhs, keys) + slope * jnp.tile(neg_dist, (n_heads, 1))
        m_row = jnp.maximum(jnp.max(s, axis=1, keepdims=True), sink)
        pr = jnp.exp2(s - m_row)
        den = jnp.sum(pr, axis=1, keepdims=True) + jnp.exp2(sink - m_row)
        write(_dot(pr.astype(BF16), vals) / den)


def _swa_prompt(q, k, v, sinks, q_gain, k_gain, fast_flag):
    T, Wq = q.shape
    n_heads = Wq // HD_C
    nb = T // WINDOW
    cur = lambda w: pl.BlockSpec((WINDOW, w), lambda i, fl: (i, 0))
    prev = lambda w: pl.BlockSpec((WINDOW, w), lambda i, fl: (jnp.maximum(i - 1, 0), 0))
    const = lambda a: pl.BlockSpec(a.shape, lambda i, fl: (0,) * a.ndim)
    kw = k.shape[1]
    grid_spec = pltpu.PrefetchScalarGridSpec(
        num_scalar_prefetch=1,
        grid=(nb,),
        in_specs=[cur(Wq), cur(kw), prev(kw), cur(kw), prev(kw), const(sinks), const(q_gain), const(k_gain)],
        out_specs=cur(Wq),
    )
    return pl.pallas_call(
        functools.partial(_swa_prompt_kernel, n_heads=n_heads),
        out_shape=jax.ShapeDtypeStruct(q.shape, F32),
        grid_spec=grid_spec,
        compiler_params=_cparams(("parallel",)),
        name="swa_prompt",
    )(fast_flag, q, k, k, v, v, sinks, q_gain, k_gain)


def _swa_sample_kernel(q_ref, kn_ref, vn_ref, ckt_ref, cvt_ref, snk_ref, o_ref, *, T, n_heads, n_sub):
    n_pairs = n_heads // KVH_C
    R = n_heads * T
    lane = lax.broadcasted_iota(jnp.int32, (T, LANES), 1)
    rr = lax.broadcasted_iota(jnp.int32, (R, WINDOW), 0)
    cc = lax.broadcasted_iota(jnp.int32, (R, WINDOW), 1)
    t = lax.rem(rr, T)
    hrow = lax.broadcasted_iota(jnp.int32, (R, 1), 0) // T
    slope = jnp.exp2(-8.0 * (hrow + 1).astype(F32) / n_heads) * LOG2E
    sink = jnp.zeros((R, 1), F32)
    for h in range(n_heads):
        sink = jnp.where(hrow == h, snk_ref[:, h:h + 1], sink)
    sink = sink * LOG2E
    bias_c = jnp.where(cc >= t, -slope * (t + WINDOW - cc).astype(F32), NEG_BIG)
    bias_n = jnp.where(cc <= t, -slope * (t - cc).astype(F32), NEG_BIG)
    zpad = jnp.zeros((WINDOW - T, LANES), F32)
    half_rows = n_pairs * T
    for sq in range(n_sub):
        rows = slice(sq * T, (sq + 1) * T)
        blocks = []
        for n in range(KVH_C):
            half = (lane < HD_C) if n == 0 else (lane >= HD_C)
            for m in range(n_pairs):
                blocks.append(jnp.where(half, q_ref[rows, m * LANES:(m + 1) * LANES], 0.0))
        lhs = jnp.concatenate(blocks, axis=0).astype(BF16)
        kn = jnp.concatenate([kn_ref[rows, :], zpad], axis=0).astype(BF16)
        vn = jnp.concatenate([vn_ref[rows, :], zpad], axis=0).astype(BF16)
        ckt = ckt_ref[sq].astype(BF16)
        cvt = cvt_ref[sq].astype(BF16)
        s_c = _dot(lhs, ckt) + bias_c
        s_n = _dot_nt(lhs, kn) + bias_n
        res = _swa_softmax_pv([s_c, s_n], sink, [lambda p: _dot_nt(p, cvt), lambda p: _dot(p, vn)])
        for m in range(n_pairs):
            lo = res[m * T:(m + 1) * T]
            hi = res[half_rows + m * T:half_rows + (m + 1) * T]
            o_ref[rows, m * LANES:(m + 1) * LANES] = jnp.where(lane < HD_C, lo, hi)


def _swa_sample(q, k_new, v_new, ckt, cvt, sinks, T, n_sub):
    BT, Wq = q.shape
    B = ckt.shape[0]
    n_heads = Wq // HD_C
    assert B % n_sub == 0
    rows = lambda w: pl.BlockSpec((n_sub * T, w), lambda b: (b, 0))
    cache = pl.BlockSpec((n_sub,) + ckt.shape[1:], lambda b: (b, 0, 0))
    return pl.pallas_call(
        functools.partial(_swa_sample_kernel, T=T, n_heads=n_heads, n_sub=n_sub),
        out_shape=jax.ShapeDtypeStruct(q.shape, F32),
        grid=(B // n_sub,),
        in_specs=[rows(Wq), rows(k_new.shape[1]), rows(v_new.shape[1]), cache, cache, _full(sinks.shape)],
        out_specs=rows(Wq),
        compiler_params=_cparams(("parallel",)),
        name="swa_sample",
    )(q, k_new, v_new, ckt, cvt, sinks)


def _outproj_c_kernel(x_ref, o_ref, gs_ref, w_ref, y_ref):
    y_ref[...] = x_ref[...] + _dot((o_ref[...] * gs_ref[...]).astype(BF16), w_ref[...])


def _outproj_c(x, o, gs, w_bf16, tm):
    T, D = x.shape
    row = lambda w: pl.BlockSpec((tm, w), lambda i: (i, 0))
    return pl.pallas_call(
        _outproj_c_kernel,
        out_shape=jax.ShapeDtypeStruct((T, D), F32),
        grid=(T // tm,),
        in_specs=[row(D), row(o.shape[1]), row(o.shape[1]), _full(w_bf16.shape)],
        out_specs=row(D),
        compiler_params=_cparams(("parallel",)),
        name="outproj_c",
    )(x, o, gs, w_bf16)


TOKEN_TILE = 512
GLA_CHUNK = 128
DIFF_BLOCK = 1024
DIFF_PAGES_PER_STEP = 32
RIDE_CHUNKS_PER_STEP = 2
SWA_SEQS_PER_STEP = 8
GLA_SEQS_PER_STEP = 4
GLA_HEADS_PER_STEP = 4


def _pair_perm(n_heads):
    n_pairs = n_heads // KVH_C
    cols = []
    for m in range(n_pairs):
        for n in range(KVH_C):
            h = n * n_pairs + m
            cols.extend(range(h * HD_C, (h + 1) * HD_C))
    return np.asarray(cols, np.int32)


def kernel(x_prompt, x_sample, state_hgrn, cache_k_diff, cache_v_diff, cache_k_swa, cache_v_swa, page_table, norm_a, w_in_a, w_out_a, lb_logits, hgrn_out_gain, diff_q_gain, diff_k_gain, diff_subln_gain, lam_q1, lam_k1, lam_q2, lam_k2, norm_c, w_in_c, w_out_c, swa_q_gain, swa_k_gain, sinks):
    batch, seq, d_model = x_prompt.shape
    n_seq, t_dec, _ = x_sample.shape
    assert batch == 1
    n_even, n_odd = norm_a.shape[0], norm_c.shape[0]
    depth = n_even + n_odd
    h_a = state_hgrn.shape[2]
    n_pool, page = cache_k_diff.shape[1], cache_k_diff.shape[2]
    w_b = w_out_a.shape[1] - h_a * DV_A
    h_b = w_b // (2 * D_B)
    h_c = sinks.shape[1]

    xp = x_prompt.reshape(seq, d_model)
    xs = x_sample.reshape(n_seq * t_dec, d_model)
    tile_p, tile_s = min(TOKEN_TILE, seq), min(TOKEN_TILE, n_seq * t_dec)
    bd64 = _block_diag_mean(2 * LANES, D_B)
    bd64_c = _block_diag_mean(LANES, HD_C)
    perm = _pair_perm(h_c)

    hs_p, hs_s, kdp, vdp, kds, vds, ksp, vsp, kss, vss = ([] for _ in range(10))
    for l in range(depth):
        if l % 2 == 0:
            e = l // 2
            lam_init = 0.8 - 0.6 * math.exp(-0.3 * l)
            w_in = w_in_a[e].astype(BF16)
            w_out = w_out_a[e].astype(BF16)
            g = norm_a[e:e + 1]
            qg = jnp.tile(diff_q_gain[e:e + 1], (1, h_b * 2))
            kg = jnp.tile(diff_k_gain[e:e + 1], (1, KVH_B * 2))
            lam_args = (lam_q1[e:e + 1], lam_k1[e:e + 1], lam_q2[e:e + 1], lam_k2[e:e + 1])
            kt_pages = jnp.transpose(cache_k_diff[e], (0, 2, 3, 4, 1)).reshape(n_pool, KVH_B * 2 * D_B, page)
            v_pages = cache_v_diff[e].reshape(n_pool, page * KVH_B, 2 * D_B)

            pa = _inproj_a(xp, g, w_in, lb_logits, qg, kg, bd64, e, tile_p)
            sa = _inproj_a(xs, g, w_in, lb_logits, qg, kg, bd64, e, tile_s)
            qa_p, lf_p, kk_p, ia_p, gsa_p, qb_p, kb_p, vb_p, gsb_p = pa
            qa_s, lf_s, kk_s, ia_s, gsa_s, qb_s, kb_s, vb_s, gsb_s = sa

            lb_min = jnp.min(jnp.cumsum(jax.nn.softmax(lb_logits.astype(F32), axis=0), axis=0)[e])
            gla_flag = (-GLA_SUB * jnp.log(lb_min) <= GLA_FAST_DECAY).astype(jnp.int32).reshape(1)
            oa_p, st_p = _gla_prompt(qa_p, kk_p, ia_p, lf_p, gla_flag, GLA_CHUNK, GLA_HEADS_PER_STEP)
            oa_s, st_s = _gla_sample(qa_s, kk_s, ia_s, lf_s, state_hgrn[e], t_dec, min(GLA_SEQS_PER_STEP, n_seq))
            logit_bound = (LOG2E * math.sqrt(D_B)) * jnp.max(jnp.abs(diff_q_gain[e])) * jnp.max(jnp.abs(diff_k_gain[e]))
            fast_flag = (logit_bound <= FAST_LOGIT_BOUND).astype(jnp.int32).reshape(1)
            k_gain = diff_k_gain[e:e + 1]
            def diff_separate():
                return (_diff_prompt(qb_p, kb_p, vb_p, k_gain, fast_flag, *lam_args, lam_init, DIFF_BLOCK),
                        _diff_sample(qb_s, kb_s, vb_s, page_table, kt_pages, v_pages, k_gain, fast_flag, *lam_args,
                                     lam_init, t_dec, DIFF_PAGES_PER_STEP))

            n_pages = page_table.shape[1]
            n_blk = seq // DIFF_BLOCK if seq % DIFF_BLOCK == 0 else 0
            n_ride = 0
            if n_pages % (DIFF_PAGES_PER_STEP * RIDE_CHUNKS_PER_STEP) == 0:
                steps_per_seq = n_pages // (DIFF_PAGES_PER_STEP * RIDE_CHUNKS_PER_STEP)
                n_ride = min(n_seq, (KVH_B * n_blk * (n_blk + 1) // 2) // steps_per_seq)

            def diff_fused():
                rows = n_ride * t_dec
                ob_p_f, ob_s_f = _diff_fused(qb_p, kb_p, vb_p, qb_s[:rows], kb_s[:rows], vb_s[:rows],
                                             page_table[:n_ride], kt_pages, v_pages, k_gain, *lam_args, lam_init,
                                             DIFF_BLOCK, t_dec, DIFF_PAGES_PER_STEP, RIDE_CHUNKS_PER_STEP)
                if n_ride == n_seq:
                    return ob_p_f, ob_s_f
                rest = _diff_sample(qb_s[rows:], kb_s[rows:], vb_s[rows:], page_table[n_ride:], kt_pages, v_pages,
                                    k_gain, fast_flag, *lam_args, lam_init, t_dec, DIFF_PAGES_PER_STEP)
                return ob_p_f, jnp.concatenate([ob_s_f, rest], axis=0)

            if n_ride > 0:
                ob_p, ob_s = lax.cond(fast_flag[0] != 0, diff_fused, diff_separate)
            else:
                ob_p, ob_s = diff_separate()

            og, sg = hgrn_out_gain[e:e + 1], diff_subln_gain[e:e + 1]
            xp = _outproj_a(xp, oa_p, gsa_p, ob_p, gsb_p, og, sg, w_out, lam_init, tile_p)
            xs = _outproj_a(xs, oa_s, gsa_s, ob_s, gsb_s, og, sg, w_out, lam_init, tile_s)

            hs_p.append(st_p[None])
            hs_s.append(st_s)
            kdp.append(kb_p.reshape(batch, seq, KVH_B, 2, D_B))
            vdp.append(vb_p.reshape(batch, seq, KVH_B, 2 * D_B))
            kds.append(kb_s.reshape(n_seq, t_dec, KVH_B, 2, D_B))
            vds.append(vb_s.reshape(n_seq, t_dec, KVH_B, 2 * D_B))
        else:
            o = l // 2
            wq = h_c * HD_C
            wkv = KVH_C * HD_C
            w_full = w_in_c[o]
            w_in = jnp.concatenate([w_full[:, :wq][:, perm], w_full[:, wq:wq + 2 * wkv],
                                    w_full[:, wq + 2 * wkv:][:, perm]], axis=1).astype(BF16)
            w_out = w_out_c[o][perm, :].astype(BF16)
            g = norm_c[o:o + 1]
            qg = jnp.tile(swa_q_gain[o:o + 1], (1, h_c))
            kg = jnp.tile(swa_k_gain[o:o + 1], (1, KVH_C))
            snk = sinks[o:o + 1]
            ckt = jnp.transpose(cache_k_swa[o], (0, 2, 3, 1)).reshape(n_seq, wkv, WINDOW)
            cvt = jnp.transpose(cache_v_swa[o], (0, 2, 3, 1)).reshape(n_seq, wkv, WINDOW)

            q_p, k_p, v_p, gs_p = _inproj_c(xp, g, w_in, qg, kg, bd64_c, tile_p)
            q_s, k_s, v_s, gs_s = _inproj_c(xs, g, w_in, qg, kg, bd64_c, tile_s)
            swa_bound = ((LOG2E * math.sqrt(HD_C)) * jnp.max(jnp.abs(swa_q_gain[o]))
                         * jnp.max(jnp.abs(swa_k_gain[o])))
            swa_flag = (swa_bound <= FAST_LOGIT_BOUND).astype(jnp.int32).reshape(1)
            o_p = _swa_prompt(q_p, k_p, v_p, snk, swa_q_gain[o:o + 1], swa_k_gain[o:o + 1], swa_flag)
            o_s = _swa_sample(q_s, k_s, v_s, ckt, cvt, snk, t_dec, min(SWA_SEQS_PER_STEP, n_seq))
            xp = _outproj_c(xp, o_p, gs_p, w_out, tile_p)
            xs = _outproj_c(xs, o_s, gs_s, w_out, tile_s)

            ksp.append(k_p[-WINDOW:].reshape(batch, WINDOW, KVH_C, HD_C))
            vsp.append(v_p[-WINDOW:].reshape(batch, WINDOW, KVH_C, HD_C))
            k_new = k_s.reshape(n_seq, t_dec, KVH_C, HD_C)
            v_new = v_s.reshape(n_seq, t_dec, KVH_C, HD_C)
            kss.append(jnp.concatenate([cache_k_swa[o], k_new], axis=1)[:, -WINDOW:])
            vss.append(jnp.concatenate([cache_v_swa[o], v_new], axis=1)[:, -WINDOW:])
    return (xp.reshape(batch, seq, d_model), xs.reshape(n_seq, t_dec, d_model),
            jnp.stack(hs_p), jnp.stack(hs_s), jnp.stack(kdp), jnp.stack(vdp), jnp.stack(kds), jnp.stack(vds),
            jnp.stack(ksp), jnp.stack(vsp), jnp.stack(kss), jnp.stack(vss))
```
